```python
import jax, jax.numpy as jnp
from jax import lax
import numpy as np

D_MODEL = 1024
BATCH = 8
SEQ = 2048
DEPTH = 1

MEM_LEN = 256

GDN_DK = 128
GDN_DV = 128
GDN_HEADS = D_MODEL // GDN_DV
GDN_CONV = 4
GDN_CHUNK = 64

NSA_DH = 128
NSA_HEADS = D_MODEL // NSA_DH
NSA_KV_HEADS = 2
CMP_BLOCK = 32
CMP_STRIDE = 16
CMP_HIDDEN = 256
SEL_BLOCK = 64
SEL_TOPK = 8
WINDOW = 256
NSA_QBLOCK = 64
ROPE_THETA = 10000.0

XA_HEADS = 4
XA_DH = D_MODEL // XA_HEADS

N_GROUPS = 4
EXPERTS_PER_GROUP = 8
N_EXPERTS = N_GROUPS * EXPERTS_PER_GROUP
TOPK_IN_GROUP = 2
D_FF_EXPERT = D_MODEL // 4

DN_ALPHA = (2.0 * DEPTH) ** 0.25
DN_BETA = (8.0 * DEPTH) ** -0.25
LN_EPS = 1e-5
RMS_EPS = 1e-6

IN_SPLITS = (
    GDN_HEADS * GDN_DK,
    GDN_HEADS * GDN_DK,
    GDN_HEADS * GDN_DV,
    GDN_HEADS * GDN_DV,
    GDN_HEADS,
    GDN_HEADS,
    NSA_HEADS * NSA_DH,
    NSA_KV_HEADS * NSA_DH,
    NSA_KV_HEADS * NSA_DH,
    NSA_KV_HEADS * NSA_DH,
    NSA_KV_HEADS * NSA_DH,
    NSA_KV_HEADS * NSA_DH,
    NSA_KV_HEADS * NSA_DH,
    NSA_HEADS * 3,
    2 * D_MODEL,
)
IN_WIDTH = sum(IN_SPLITS)

kernel_name = "hybrid_gdn_nsa_memxattn_hmoe_deepnorm"


def layer_norm(x, g, b):
    xf = x.astype(jnp.float32)
    mu = jnp.mean(xf, -1, keepdims=True)
    var = jnp.mean(jnp.square(xf - mu), -1, keepdims=True)
    return ((xf - mu) * lax.rsqrt(var + LN_EPS) * g + b).astype(x.dtype)


def rms_norm(x, g):
    xf = x.astype(jnp.float32)
    return xf * lax.rsqrt(jnp.mean(jnp.square(xf), -1, keepdims=True) + RMS_EPS) * g


def l2_normalize(x):
    return x * lax.rsqrt(jnp.sum(jnp.square(x), -1, keepdims=True) + RMS_EPS)


def rope_tables(positions, dim):
    half = dim // 2
    inv_freq = ROPE_THETA ** (-jnp.arange(half, dtype=jnp.float32) / half)
    ang = positions.astype(jnp.float32)[..., None] * inv_freq
    return jnp.cos(ang)[:, :, None, :], jnp.sin(ang)[:, :, None, :]


def apply_rope(x, cos, sin):
    x1, x2 = jnp.split(x.astype(jnp.float32), 2, axis=-1)
    return jnp.concatenate([x1 * cos - x2 * sin, x2 * cos + x1 * sin], -1).astype(x.dtype)


def causal_dwconv(x, w):
    K, C = w.shape
    return lax.conv_general_dilated(x, w[:, None, :].astype(x.dtype), window_strides=(1,),
                                    padding=[(K - 1, 0)], dimension_numbers=('NWC', 'WIO', 'NWC'),
                                    feature_group_count=C)


def masked_softmax(s, mask):
    s = jnp.where(mask, s.astype(jnp.float32), -jnp.inf)
    m = jnp.max(s, axis=-1, keepdims=True)
    m = jnp.where(jnp.isfinite(m), m, 0.0)
    e = jnp.exp(s - m)
    return e / jnp.maximum(jnp.sum(e, axis=-1, keepdims=True), 1e-30)


def gated_deltanet(q, k, v, a_logit, b_logit, z, a_log, dt_bias, norm_w):
    B, S, H, dk = q.shape
    dv = v.shape[-1]
    C = GDN_CHUNK
    N = S // C
    f32 = jnp.float32
    q = l2_normalize(q.astype(f32)) * (dk ** -0.5)
    k = l2_normalize(k.astype(f32))
    v = v.astype(f32)
    beta = jax.nn.sigmoid(b_logit.astype(f32))
    g = -jnp.exp(a_log.astype(f32)) * jax.nn.softplus(a_logit.astype(f32) + dt_bias.astype(f32))

    def chunks(t):
        return t.reshape(B, N, C, H, -1).transpose(0, 3, 1, 2, 4)

    q, k, v = chunks(q), chunks(k), chunks(v)
    beta = beta.reshape(B, N, C, H).transpose(0, 3, 1, 2)
    gc = jnp.cumsum(g.reshape(B, N, C, H).transpose(0, 3, 1, 2), axis=-1)
    diff = gc[..., :, None] - gc[..., None, :]
    causal = jnp.tril(jnp.ones((C, C), dtype=bool))
    strict = jnp.tril(jnp.ones((C, C), dtype=bool), -1)
    decay = jnp.exp(jnp.where(causal, diff, -jnp.inf))
    a_mat = jnp.where(strict, jnp.einsum('bhncd,bhnmd->bhncm', k, k) * decay, 0.0) * beta[..., None]
    lhs = a_mat + jnp.eye(C, dtype=f32)
    rhs = jnp.concatenate([v * beta[..., None], k * (beta * jnp.exp(gc))[..., None]], -1)
    sol = lax.linalg.triangular_solve(lhs, rhs, left_side=True, lower=True, unit_diagonal=True)
    u, w = sol[..., :dv], sol[..., dv:]
    qk = jnp.einsum('bhncd,bhnmd->bhncm', q, k) * decay
    q_dec = q * jnp.exp(gc)[..., None]
    g_last = gc[..., -1]
    k_dec = k * jnp.exp(g_last[..., None] - gc)[..., None]

    def step(state, inp):
        u_c, w_c, qk_c, qd_c, kd_c, gl_c = inp
        v_new = u_c - jnp.einsum('bhcd,bhde->bhce', w_c, state)
        o_c = jnp.einsum('bhcd,bhde->bhce', qd_c, state) + jnp.einsum('bhcm,bhme->bhce', qk_c, v_new)
        state = state * jnp.exp(gl_c)[..., None, None] + jnp.einsum('bhcd,bhce->bhde', kd_c, v_new)
        return state, o_c

    xs = tuple(jnp.moveaxis(t, 2, 0) for t in (u, w, qk, q_dec, k_dec, g_last))
    _, o = lax.scan(step, jnp.zeros((B, H, dk, dv), f32), xs)
    o = o.transpose(1, 0, 3, 2, 4).reshape(B, S, H, dv)
    o = rms_norm(o, norm_w) * jax.nn.silu(z.astype(f32))
    return o.reshape(B, S, H * dv)


def compress_kv(k, v, pe, w1, b1, w2):
    B, S, G, dh = k.shape
    nc = (S - CMP_BLOCK) // CMP_STRIDE + 1
    idx = jnp.arange(nc)[:, None] * CMP_STRIDE + jnp.arange(CMP_BLOCK)[None, :]

    def one(t, j):
        blk = t[:, idx] + pe[j][None, None, :, None, :]
        blk = blk.transpose(0, 1, 3, 2, 4).reshape(B, nc, G, CMP_BLOCK * dh)
        return jax.nn.gelu(blk @ w1[j] + b1[j]) @ w2[j]

    return one(k, 0), one(v, 1)


def block_overlap(nc, nb):
    c0 = jnp.arange(nc) * CMP_STRIDE
    s0 = jnp.arange(nb) * SEL_BLOCK
    ov = jnp.minimum(c0[:, None] + CMP_BLOCK, s0[None, :] + SEL_BLOCK) - jnp.maximum(c0[:, None], s0[None, :])
    return (jnp.maximum(ov, 0) / CMP_BLOCK).astype(jnp.float32)


def nsa_attention(q, kc, vc, ks, vs, kw, vw, gates):
    f32 = jnp.float32
    B, S, H, dh = q.shape
    G = kc.shape[2]
    hpg = H // G
    nc = kc.shape[1]
    nb = S // SEL_BLOCK
    n_sel = min(SEL_TOPK, nb)
    QB = NSA_QBLOCK
    q = q.astype(f32) * (dh ** -0.5)
    kc, vc, gates = kc.astype(f32), vc.astype(f32), gates.astype(f32)
    cmp_end = jnp.arange(nc) * CMP_STRIDE + CMP_BLOCK - 1
    overlap = block_overlap(nc, nb)
    ks_blk = ks.astype(f32).reshape(B, nb, SEL_BLOCK, G, dh).transpose(0, 3, 1, 2, 4)
    vs_blk = vs.astype(f32).reshape(B, nb, SEL_BLOCK, G, dh).transpose(0, 3, 1, 2, 4)
    kw_pad = jnp.pad(kw.astype(f32), ((0, 0), (WINDOW, 0), (0, 0), (0, 0)))
    vw_pad = jnp.pad(vw.astype(f32), ((0, 0), (WINDOW, 0), (0, 0), (0, 0)))
    blk_ids = jnp.arange(nb)
    offs = jnp.arange(SEL_BLOCK)
    gather = jax.vmap(jax.vmap(lambda blocks, ids: blocks[ids]))

    def one_block(i):
        t0 = i * QB
        t = t0 + jnp.arange(QB)
        qb = lax.dynamic_slice_in_dim(q, t0, QB, axis=1).reshape(B, QB, G, hpg, dh)
        gb = lax.dynamic_slice_in_dim(gates, t0, QB, axis=1).reshape(B, QB, G, hpg, 3)
        p_c = masked_softmax(jnp.einsum('bqghd,bngd->bghqn', qb, kc), cmp_end[None, :] <= t[:, None])
        o_c = jnp.einsum('bghqn,bngd->bqghd', p_c, vc)
        imp = jnp.einsum('bghqn,nj->bgqj', p_c, overlap)
        cur = t // SEL_BLOCK
        valid = blk_ids[None, :] * SEL_BLOCK <= t[:, None]
        forced = (blk_ids[None, :] == 0) | (blk_ids[None, :] == cur[:, None]) | (blk_ids[None, :] == cur[:, None] - 1)
        imp = jnp.where(forced, jnp.inf, jnp.where(valid, imp, -jnp.inf))
        _, sel = lax.top_k(imp, n_sel)
        flat = sel.reshape(B, G, QB * n_sel)
        k_sel = gather(ks_blk, flat).reshape(B, G, QB, n_sel * SEL_BLOCK, dh)
        v_sel = gather(vs_blk, flat).reshape(B, G, QB, n_sel * SEL_BLOCK, dh)
        kpos = (sel[..., None] * SEL_BLOCK + offs).reshape(B, G, QB, n_sel * SEL_BLOCK)
        p_s = masked_softmax(jnp.einsum('bqghd,bgqkd->bghqk', qb, k_sel), (kpos <= t[:, None])[:, :, None])
        o_s = jnp.einsum('bghqk,bgqkd->bqghd', p_s, v_sel)
        k_w = lax.dynamic_slice_in_dim(kw_pad, t0, QB + WINDOW, axis=1)
        v_w = lax.dynamic_slice_in_dim(vw_pad, t0, QB + WINDOW, axis=1)
        wpos = t0 - WINDOW + jnp.arange(QB + WINDOW)
        rel = t[:, None] - wpos[None, :]
        wmask = (rel >= 0) & (rel < WINDOW) & (wpos[None, :] >= 0)
        p_w = masked_softmax(jnp.einsum('bqghd,bkgd->bghqk', qb, k_w), wmask)
        o_w = jnp.einsum('bghqk,bkgd->bqghd', p_w, v_w)
        o = gb[..., 0:1] * o_c + gb[..., 1:2] * o_s + gb[..., 2:3] * o_w
        return o.reshape(B, QB, H * dh)

    out = lax.map(one_block, jnp.arange(S // QB))
    return out.transpose(1, 0, 2, 3).reshape(B, S, H * dh)


def token_mixer(h, cos, sin, w_in, conv_w, a_log, dt_bias, norm_w, cmp_pe, cmp_w1, cmp_b1, cmp_w2, w_out):
    B, S, _ = h.shape
    proj = h @ w_in
    points = np.cumsum(IN_SPLITS)[:-1].tolist()
    (g_q, g_k, g_v, g_z, g_b, g_a, n_q, c_k, c_v, s_k, s_v, w_k, w_v, n_g, m_g) = jnp.split(proj, points, axis=-1)
    qkv = jax.nn.silu(causal_dwconv(jnp.concatenate([g_q, g_k, g_v], -1), conv_w))
    qkw = GDN_HEADS * GDN_DK
    y_a = gated_deltanet(qkv[..., :qkw].reshape(B, S, GDN_HEADS, GDN_DK),
                         qkv[..., qkw:2 * qkw].reshape(B, S, GDN_HEADS, GDN_DK),
                         qkv[..., 2 * qkw:].reshape(B, S, GDN_HEADS, GDN_DV),
                         g_a, g_b, g_z.reshape(B, S, GDN_HEADS, GDN_DV), a_log, dt_bias, norm_w)
    def kvh(t):
        return t.reshape(B, S, NSA_KV_HEADS, NSA_DH)
    q = apply_rope(n_q.reshape(B, S, NSA_HEADS, NSA_DH), cos, sin)
    k_cmp, v_cmp = compress_kv(apply_rope(kvh(c_k), cos, sin), kvh(c_v), cmp_pe, cmp_w1, cmp_b1, cmp_w2)
    y_b = nsa_attention(q, k_cmp, v_cmp, apply_rope(kvh(s_k), cos, sin), kvh(s_v),
                        apply_rope(kvh(w_k), cos, sin), kvh(w_v),
                        jax.nn.sigmoid(n_g.reshape(B, S, NSA_HEADS, 3)))
    gate = jax.nn.sigmoid(m_g)
    merged = gate[..., :D_MODEL] * y_a.astype(h.dtype) + gate[..., D_MODEL:] * y_b.astype(h.dtype)
    return merged @ w_out


def memory_xattn(h, mem, wq, wkv, wo):
    B, S, _ = h.shape
    M = mem.shape[1]
    q = (h @ wq).reshape(B, S, XA_HEADS, XA_DH)
    kv = mem @ wkv
    k = kv[..., :D_MODEL].reshape(B, M, XA_HEADS, XA_DH)
    v = kv[..., D_MODEL:].reshape(B, M, XA_HEADS, XA_DH)
    s = jnp.einsum('bshd,bmhd->bhsm', q, k).astype(jnp.float32) * (XA_DH ** -0.5)
    p = jax.nn.softmax(s, axis=-1).astype(v.dtype)
    o = jnp.einsum('bhsm,bmhd->bshd', p, v).reshape(B, S, XA_HEADS * XA_DH)
    return o @ wo


def hier_moe(h, w_group, b_group, w_expert, b_expert, w_gate, w_up, w_down):
    B, S, D = h.shape
    xf = h.reshape(B * S, D)
    n = xf.shape[0]
    p_group = jax.nn.softmax((xf @ w_group).astype(jnp.float32) + b_group, axis=-1)
    p_top, g_idx = lax.top_k(p_group, 1)
    g_onehot = jax.nn.one_hot(g_idx[:, 0], N_GROUPS, dtype=jnp.float32)
    e_logits = ((xf @ w_expert).astype(jnp.float32) + b_expert).reshape(n, N_GROUPS, EXPERTS_PER_GROUP)
    e_logits = jnp.einsum('ng,nge->ne', g_onehot, e_logits)
    e_top, e_idx = lax.top_k(jax.nn.softmax(e_logits, axis=-1), TOPK_IN_GROUP)
    wts = p_top * e_top / jnp.sum(e_top, -1, keepdims=True)
    expert_id = g_idx * EXPERTS_PER_GROUP + e_idx
    combine = jnp.einsum('nk,nke->ne', wts, jax.nn.one_hot(expert_id, N_EXPERTS, dtype=jnp.float32))
    out = jnp.zeros((n, D), jnp.float32)
    for grp in range(N_GROUPS):
        sl = slice(grp * EXPERTS_PER_GROUP, (grp + 1) * EXPERTS_PER_GROUP)
        hg = jax.nn.silu(jnp.einsum('nd,edf->nef', xf, w_gate[sl])) * jnp.einsum('nd,edf->nef', xf, w_up[sl])
        out = out + jnp.einsum('nef,efd->nd', hg * combine[:, sl, None].astype(hg.dtype), w_down[sl])
    return out.astype(h.dtype).reshape(B, S, D)


def setup_inputs(seed: int = 0) -> dict:
    key = jax.random.key(seed)
    ks = jax.random.split(key, 32)
    f32 = jnp.float32
    L = DEPTH

    def nrm(i, shape, scale):
        return jax.random.normal(ks[i], shape, f32) * scale

    x = nrm(0, (BATCH, SEQ, D_MODEL), 1.0)
    mem = nrm(1, (BATCH, MEM_LEN, D_MODEL), 1.0)
    positions = (jnp.arange(SEQ, dtype=jnp.int32)[None, :]
                 + jax.random.randint(ks[2], (BATCH, 1), 0, 1024, dtype=jnp.int32))
    w_in = nrm(3, (L, D_MODEL, IN_WIDTH), D_MODEL ** -0.5)
    gdn_conv_w = nrm(4, (L, GDN_CONV, 2 * GDN_HEADS * GDN_DK + GDN_HEADS * GDN_DV), GDN_CONV ** -0.5)
    gdn_a_log = jnp.log(jax.random.uniform(ks[5], (L, GDN_HEADS), f32, 1.0, 16.0))
    dt = jnp.exp(jax.random.uniform(ks[6], (L, GDN_HEADS), f32, jnp.log(1e-3), jnp.log(1e-1)))
    gdn_dt_bias = dt + jnp.log(-jnp.expm1(-dt))
    gdn_norm_w = 1.0 + nrm(7, (L, GDN_DV), 0.02)
    cmp_pe = nrm(8, (L, 2, CMP_BLOCK, NSA_DH), 0.02)
    cmp_w1 = nrm(9, (L, 2, CMP_BLOCK * NSA_DH, CMP_HIDDEN), (CMP_BLOCK * NSA_DH) ** -0.5)
    cmp_b1 = nrm(10, (L, 2, CMP_HIDDEN), 0.02)
    cmp_w2 = nrm(11, (L, 2, CMP_HIDDEN, NSA_DH), CMP_HIDDEN ** -0.5)
    w_out = nrm(12, (L, D_MODEL, D_MODEL), D_MODEL ** -0.5 * DN_BETA)
    ln1_g = 1.0 + nrm(13, (L, D_MODEL), 0.02)
    ln1_b = nrm(14, (L, D_MODEL), 0.02)
    xa_wq = nrm(15, (L, D_MODEL, XA_HEADS * XA_DH), D_MODEL ** -0.5)
    xa_wkv = nrm(16, (L, D_MODEL, 2 * XA_HEADS * XA_DH), D_MODEL ** -0.5)
    xa_wo = nrm(17, (L, XA_HEADS * XA_DH, D_MODEL), D_MODEL ** -0.5 * DN_BETA)
    ln2_g = 1.0 + nrm(18, (L, D_MODEL), 0.02)
    ln2_b = nrm(19, (L, D_MODEL), 0.02)
    moe_w_group = nrm(20, (L, D_MODEL, N_GROUPS), D_MODEL ** -0.5)
    moe_b_group = nrm(21, (L, N_GROUPS), 0.01)
    moe_w_expert = nrm(22, (L, D_MODEL, N_EXPERTS), D_MODEL ** -0.5)
    moe_b_expert = nrm(23, (L, N_EXPERTS), 0.01)
    moe_w_gate = nrm(24, (L, N_EXPERTS, D_MODEL, D_FF_EXPERT), D_MODEL ** -0.5)
    moe_w_up = nrm(25, (L, N_EXPERTS, D_MODEL, D_FF_EXPERT), D_MODEL ** -0.5)
    moe_w_down = nrm(26, (L, N_EXPERTS, D_FF_EXPERT, D_MODEL), D_FF_EXPERT ** -0.5 * DN_BETA)
    ln3_g = 1.0 + nrm(27, (L, D_MODEL), 0.02)
    ln3_b = nrm(28, (L, D_MODEL), 0.02)
    return {"x": x, "mem": mem, "positions": positions, "w_in": w_in, "gdn_conv_w": gdn_conv_w,
            "gdn_a_log": gdn_a_log, "gdn_dt_bias": gdn_dt_bias, "gdn_norm_w": gdn_norm_w,
            "cmp_pe": cmp_pe, "cmp_w1": cmp_w1, "cmp_b1": cmp_b1, "cmp_w2": cmp_w2, "w_out": w_out,
            "ln1_g": ln1_g, "ln1_b": ln1_b, "xa_wq": xa_wq, "xa_wkv": xa_wkv, "xa_wo": xa_wo,
            "ln2_g": ln2_g, "ln2_b": ln2_b, "moe_w_group": moe_w_group, "moe_b_group": moe_b_group,
            "moe_w_expert": moe_w_expert, "moe_b_expert": moe_b_expert, "moe_w_gate": moe_w_gate,
            "moe_w_up": moe_w_up, "moe_w_down": moe_w_down, "ln3_g": ln3_g, "ln3_b": ln3_b}


def reference(x, mem, positions, w_in, gdn_conv_w, gdn_a_log, gdn_dt_bias, gdn_norm_w,
              cmp_pe, cmp_w1, cmp_b1, cmp_w2, w_out, ln1_g, ln1_b, xa_wq, xa_wkv, xa_wo,
              ln2_g, ln2_b, moe_w_group, moe_b_group, moe_w_expert, moe_b_expert,
              moe_w_gate, moe_w_up, moe_w_down, ln3_g, ln3_b):
    cos, sin = rope_tables(positions, NSA_DH)
    h = x
    for l in range(DEPTH):
        mix = token_mixer(h, cos, sin, w_in[l], gdn_conv_w[l], gdn_a_log[l], gdn_dt_bias[l], gdn_norm_w[l],
                          cmp_pe[l], cmp_w1[l], cmp_b1[l], cmp_w2[l], w_out[l])
        h = layer_norm(DN_ALPHA * h + mix, ln1_g[l], ln1_b[l])
        h = layer_norm(DN_ALPHA * h + memory_xattn(h, mem, xa_wq[l], xa_wkv[l], xa_wo[l]), ln2_g[l], ln2_b[l])
        ffn = hier_moe(h, moe_w_group[l], moe_b_group[l], moe_w_expert[l], moe_b_expert[l],
                       moe_w_gate[l], moe_w_up[l], moe_w_down[l])
        h = layer_norm(DN_ALPHA * h + ffn, ln3_g[l], ln3_b[l])
    return h
```

```python
import functools

import jax
import jax.numpy as jnp
from jax import lax
from jax.experimental import pallas as pl
from jax.experimental.pallas import tpu as pltpu

F32 = jnp.float32
BF16 = jnp.bfloat16

D_MODEL = 1024
LANE = 128
GDN_HEADS = 8
GDN_D = 128
GDN_CONV = 4
GDN_CHUNK = 64
NSA_HEADS = 8
NSA_KV_HEADS = 2
NSA_HPG = NSA_HEADS // NSA_KV_HEADS
NSA_DH = 128
CMP_BLOCK = 32
CMP_STRIDE = 16
CMP_HIDDEN = 256
SEL_BLOCK = 64
SEL_SHIFT = 6
SEL_TOPK = 8
WINDOW = 256
XA_HEADS = 4
XA_DH = 256
N_GROUPS = 4
EXPERTS_PER_GROUP = 8
N_EXPERTS = 32
D_FF = 256
DEPTH = 1
DN_ALPHA = (2.0 * DEPTH) ** 0.25
LN_EPS = 1e-5
RMS_EPS = 1e-6
ROPE_THETA = 10000.0
NEG = -1e30

CB_Q, CB_K, CB_V, CB_Z = 0, 8, 16, 24
CB_MG = 32
CB_NQ = 48
CB_KV = 56
CB_SMALL = 68
PROJ_COLS = 72 * LANE
SM_BETA, SM_DECAY, SM_NGATE = 0, 8, 16

VMEM_LIMIT = 48 * 1024 * 1024


def _cp(sem, vmem=VMEM_LIMIT):
    return pltpu.CompilerParams(dimension_semantics=sem, vmem_limit_bytes=vmem)


def _bdot(a, b):
    return jnp.dot(a.astype(BF16), b.astype(BF16), preferred_element_type=F32)


def _bdot_nt(a, b):
    return lax.dot_general(a.astype(BF16), b.astype(BF16), (((1,), (1,)), ((), ())),
                           preferred_element_type=F32)


def _bdot_tn(a, b):
    return lax.dot_general(a.astype(BF16), b.astype(BF16), (((0,), (0,)), ((), ())),
                           preferred_element_type=F32)


def _split3(x):
    h = x.astype(BF16)
    r = x - h.astype(F32)
    m = r.astype(BF16)
    l = (r - m.astype(F32)).astype(BF16)
    return h, m, l


def _sigmoid(x):
    return 1.0 / (1.0 + jnp.exp(-x))


def _silu(x):
    return x * _sigmoid(x)


def _layer_norm(x, g, b):
    mu = jnp.mean(x, -1, keepdims=True)
    xc = x - mu
    var = jnp.mean(xc * xc, -1, keepdims=True)
    return xc * lax.rsqrt(var + LN_EPS) * g + b


def _lane_col(x, c):
    lane = lax.broadcasted_iota(jnp.int32, x.shape, 1)
    return jnp.sum(jnp.where(lane == c, x, 0.0), axis=-1, keepdims=True)


def _mm_kernel(x_ref, w_ref, o_ref):
    o_ref[...] = jnp.dot(x_ref[...], w_ref[...], preferred_element_type=F32).astype(o_ref.dtype)


def _matmul(x, w, out_dtype, tm, tn):
    m, k = x.shape
    n = w.shape[1]
    return pl.pallas_call(
        _mm_kernel,
        grid=(m // tm, n // tn),
        in_specs=[pl.BlockSpec((tm, k), lambda i, j: (i, 0)),
                  pl.BlockSpec((k, tn), lambda i, j: (0, j))],
        out_specs=pl.BlockSpec((tm, tn), lambda i, j: (i, j)),
        out_shape=jax.ShapeDtypeStruct((m, n), out_dtype),
        compiler_params=_cp(("parallel", "parallel")),
        name="matmul",
    )(x, w)


def _gdn_kernel(q_ref, k_ref, v_ref, z_ref, sm_ref, cwq_ref, cwk_ref, cwv_ref,
                alog_ref, dtb_ref, nw_ref, o_ref):
    h = pl.program_id(1)
    seq = q_ref.shape[0]
    c_len = GDN_CHUNK
    n_chunks = seq // c_len
    row = lax.broadcasted_iota(jnp.int32, (c_len, c_len), 0)
    col = lax.broadcasted_iota(jnp.int32, (c_len, c_len), 1)
    causal = row >= col
    strict = row > col
    eye = (row == col).astype(F32)
    tril_incl = causal.astype(BF16)
    neg_a = -jnp.exp(_lane_col(alog_ref[...], h))
    dt_bias = _lane_col(dtb_ref[...], h)
    cwq, cwk, cwv = cwq_ref[...], cwk_ref[...], cwv_ref[...]
    norm_w = nw_ref[...]

    def conv_silu(tail, cur, w):
        xc = jnp.concatenate([tail, cur], axis=0)
        acc = jnp.zeros_like(cur)
        for j in range(GDN_CONV):
            off = 8 - (GDN_CONV - 1) + j
            acc = acc + xc[off:off + c_len, :] * w[j:j + 1, :]
        return _silu(acc)

    def body(c, carry):
        state, tq, tk, tv = carry
        rows = pl.ds(pl.multiple_of(c * c_len, c_len), c_len)
        q_raw, k_raw, v_raw = q_ref[rows, :], k_ref[rows, :], v_ref[rows, :]
        q = conv_silu(tq, q_raw, cwq)
        k = conv_silu(tk, k_raw, cwk)
        v = conv_silu(tv, v_raw, cwv)
        q = q * lax.rsqrt(jnp.sum(q * q, -1, keepdims=True) + RMS_EPS) * (GDN_D ** -0.5)
        k = k * lax.rsqrt(jnp.sum(k * k, -1, keepdims=True) + RMS_EPS)
        sm = sm_ref[rows, :]
        beta = _sigmoid(_lane_col(sm, SM_BETA + h))
        a_in = _lane_col(sm, SM_DECAY + h) + dt_bias
        softplus = jnp.maximum(a_in, 0.0) + jnp.log(1.0 + jnp.exp(-jnp.abs(a_in)))
        g = neg_a * softplus
        bmat = jnp.concatenate([jnp.where(strict, g, 0.0), jnp.broadcast_to(g, (c_len, c_len))], axis=1)
        bh, bm, bl = _split3(bmat)
        dmat = (jnp.dot(tril_incl, bh, preferred_element_type=F32)
                + jnp.dot(tril_incl, bm, preferred_element_type=F32)
                + jnp.dot(tril_incl, bl, preferred_element_type=F32))
        diff = dmat[:, :c_len]
        gc = dmat[:, c_len:c_len + 1]
        g_last = jnp.sum(g, axis=0, keepdims=True)
        decay = jnp.where(causal, jnp.exp(jnp.where(causal, diff, 0.0)), 0.0)
        kk = _bdot_nt(k, k)
        a_mat = jnp.where(strict, kk * decay, 0.0) * beta
        m_pow = -a_mat
        t_inv = eye + m_pow
        for _ in range(5):
            m_pow = _bdot(m_pow, m_pow)
            t_inv = t_inv + _bdot(t_inv, m_pow)
        e_gc = jnp.exp(gc)
        u = _bdot(t_inv, v * beta)
        w = _bdot(t_inv, k * (beta * e_gc))
        qk = _bdot_nt(q, k) * decay
        q_dec = q * e_gc
        k_dec = k * jnp.exp(g_last - gc)
        v_new = u - _bdot(w, state)
        o = _bdot(q_dec, state) + _bdot(qk, v_new)
        state = state * jnp.exp(g_last) + _bdot_tn(k_dec, v_new)
        o = o * lax.rsqrt(jnp.mean(o * o, -1, keepdims=True) + RMS_EPS) * norm_w
        o_ref[rows, :] = (o * _silu(z_ref[rows, :])).astype(o_ref.dtype)
        return state, q_raw[c_len - 8:, :], k_raw[c_len - 8:, :], v_raw[c_len - 8:, :]

    zt = jnp.zeros((8, GDN_D), F32)
    lax.fori_loop(0, n_chunks, body, (jnp.zeros((GDN_D, GDN_D), F32), zt, zt, zt))


def _gdn(proj3, conv_w, a_log, dt_bias, norm_w):
    b, s, _ = proj3.shape

    def col(off):
        return pl.BlockSpec((None, s, LANE), lambda i, j, off=off: (i, 0, off + j))

    def cw(off):
        return pl.BlockSpec((GDN_CONV, LANE), lambda i, j, off=off: (0, off + j))

    full = lambda shape: pl.BlockSpec(shape, lambda i, j: (0, 0))
    return pl.pallas_call(
        _gdn_kernel,
        grid=(b, GDN_HEADS),
        in_specs=[col(CB_Q), col(CB_K), col(CB_V), col(CB_Z),
                  pl.BlockSpec((None, s, LANE), lambda i, j: (i, 0, CB_SMALL)),
                  cw(0), cw(GDN_HEADS), cw(2 * GDN_HEADS),
                  full((1, GDN_HEADS)), full((1, GDN_HEADS)), full((1, GDN_D))],
        out_specs=pl.BlockSpec((None, s, LANE), lambda i, j: (i, 0, j)),
        out_shape=jax.ShapeDtypeStruct((b, s, GDN_HEADS * GDN_D), F32),
        compiler_params=_cp(("parallel", "parallel")),
        name="gdn",
    )(proj3, proj3, proj3, proj3, proj3, conv_w, conv_w, conv_w,
      a_log.reshape(1, GDN_HEADS), dt_bias.reshape(1, GDN_HEADS), norm_w.reshape(1, GDN_D))


def _rope_kernel(nq_ref, kv0_ref, kv1_ref, kv2_ref, cos_ref, sin_ref, q_out, ckv_out, kv_out):
    cos2 = cos_ref[...]
    sin2 = sin_ref[...]

    def rope(x):
        return x * cos2 + pltpu.roll(x, NSA_DH // 2, 1) * sin2

    scale = NSA_DH ** -0.5
    q_out[...] = jnp.concatenate(
        [rope(nq_ref[:, hh * LANE:(hh + 1) * LANE]) * scale for hh in range(NSA_HEADS)],
        axis=1).astype(q_out.dtype)
    kv_refs = (kv0_ref, kv1_ref, kv2_ref)
    blk = lambda i: kv_refs[i // 4][:, (i % 4) * LANE:(i % 4 + 1) * LANE]
    ckv_out[0] = rope(blk(0)).astype(ckv_out.dtype)
    ckv_out[1] = rope(blk(1)).astype(ckv_out.dtype)
    ckv_out[2] = blk(2).astype(ckv_out.dtype)
    ckv_out[3] = blk(3).astype(ckv_out.dtype)
    kv_out[...] = jnp.concatenate(
        [rope(blk(4)), rope(blk(5)), blk(6), blk(7), rope(blk(8)), rope(blk(9)), blk(10), blk(11)],
        axis=1).astype(kv_out.dtype)


def _rope(proj3, cos2, sin2, ts=512):
    b, s, _ = proj3.shape
    kvspec = lambda k: pl.BlockSpec((None, ts, 4 * LANE), lambda i, j, k=k: (i, j, CB_KV // 4 + k))
    return pl.pallas_call(
        _rope_kernel,
        grid=(b, s // ts),
        in_specs=[pl.BlockSpec((None, ts, NSA_HEADS * LANE), lambda i, j: (i, j, CB_NQ // NSA_HEADS)),
                  kvspec(0), kvspec(1), kvspec(2),
                  pl.BlockSpec((None, ts, LANE), lambda i, j: (i, j, 0)),
                  pl.BlockSpec((None, ts, LANE), lambda i, j: (i, j, 0))],
        out_specs=[pl.BlockSpec((None, ts, NSA_HEADS * LANE), lambda i, j: (i, j, 0)),
                   pl.BlockSpec((None, 4, ts, LANE), lambda i, j: (i, 0, j, 0)),
                   pl.BlockSpec((None, ts, 8 * LANE), lambda i, j: (i, j, 0))],
        out_shape=[jax.ShapeDtypeStruct((b, s, NSA_HEADS * LANE), BF16),
                   jax.ShapeDtypeStruct((b, 4, s, LANE), BF16),
                   jax.ShapeDtypeStruct((b, s, 8 * LANE), BF16)],
        compiler_params=_cp(("parallel", "parallel")),
        name="rope",
    )(proj3, proj3, proj3, proj3, cos2, sin2)


def _compress_kernel(x_ref, pe_ref, w1_ref, b1_ref, w2_ref, o_ref):
    x = x_ref[...]
    half = CMP_STRIDE * NSA_DH
    first = jnp.dot(x, w1_ref[:half, :], preferred_element_type=F32)
    second = jnp.dot(x, w1_ref[half:, :], preferred_element_type=F32)
    n_rows = x.shape[0]
    second = pltpu.roll(second, n_rows - 1, 0)
    pe = jnp.broadcast_to(pe_ref[...], (8, pe_ref.shape[1]))
    bias = jnp.dot(pe, w1_ref[...], preferred_element_type=F32)[0:1] + b1_ref[...]
    hid = first + second + bias
    gelu = 0.5 * hid * (1.0 + jnp.tanh(0.7978845608028654 * (hid + 0.044715 * hid * hid * hid)))
    o_ref[...] = _bdot(gelu, w2_ref[...]).astype(o_ref.dtype)


def _compress(ckv_r, pe, w1, b1, w2):
    b, _, nr, wd = ckv_r.shape
    return pl.pallas_call(
        _compress_kernel,
        grid=(b, 4),
        in_specs=[pl.BlockSpec((None, None, nr, wd), lambda i, j: (i, j, 0, 0)),
                  pl.BlockSpec((None, 1, CMP_BLOCK * NSA_DH), lambda i, j: (j // 2, 0, 0)),
                  pl.BlockSpec((None, CMP_BLOCK * NSA_DH, CMP_HIDDEN), lambda i, j: (j // 2, 0, 0)),
                  pl.BlockSpec((None, 1, CMP_HIDDEN), lambda i, j: (j // 2, 0, 0)),
                  pl.BlockSpec((None, CMP_HIDDEN, NSA_DH), lambda i, j: (j // 2, 0, 0))],
        out_specs=pl.BlockSpec((None, None, nr, NSA_DH), lambda i, j: (i, j, 0, 0)),
        out_shape=jax.ShapeDtypeStruct((b, 4, nr, NSA_DH), BF16),
        compiler_params=_cp(("parallel", "arbitrary")),
        name="compress",
    )(ckv_r, pe, w1, b1, w2)


def _nsa_kernel(q_ref, kc_ref, vc_ref, ks_ref, vs_ref, kw_ref, vw_ref, sm_ref, ov_ref, o_ref, *, tq, kc_len):
    g = pl.program_id(1)
    qi = pl.program_id(2)
    t0 = qi * tq
    hpg = NSA_HPG
    rows4 = hpg * tq
    q = q_ref[...]
    qs = jnp.concatenate([q[:, hh * LANE:(hh + 1) * LANE] for hh in range(hpg)], axis=0)
    tile4 = lambda m: jnp.concatenate([m] * hpg, axis=0)

    def qpos(width):
        return t0 + (lax.broadcasted_iota(jnp.int32, (rows4, width), 0) & (tq - 1))

    lane = lax.broadcasted_iota(jnp.int32, (tq, LANE), 1)
    t = t0 + lax.broadcasted_iota(jnp.int32, (tq, LANE), 0)

    s_c = _bdot_nt(qs, kc_ref[...])
    lane4 = lax.broadcasted_iota(jnp.int32, (rows4, LANE), 1)
    cmask = lane4 * CMP_STRIDE + (CMP_BLOCK - 1) <= qpos(LANE)
    s_c = jnp.where(cmask, s_c, NEG)
    m_c = jnp.max(s_c, -1, keepdims=True)
    e_c = jnp.where(cmask, jnp.exp(s_c - m_c), 0.0)
    p_c = e_c / jnp.maximum(jnp.sum(e_c, -1, keepdims=True), 1e-30)
    o_c = _bdot(p_c, vc_ref[...])

    p_sum = p_c[0:tq]
    for hh in range(1, hpg):
        p_sum = p_sum + p_c[hh * tq:(hh + 1) * tq]
    ph, pm, plo = _split3(p_sum)
    ov = ov_ref[...]
    imp = (jnp.dot(ph, ov, preferred_element_type=F32) + jnp.dot(pm, ov, preferred_element_type=F32)
           + jnp.dot(plo, ov, preferred_element_type=F32))
    cur = t >> SEL_SHIFT
    valid = lane * SEL_BLOCK <= t
    forced = (lane == 0) | (lane == cur) | (lane == cur - 1)
    val = jnp.where(forced, jnp.inf, jnp.where(valid, imp, -jnp.inf))
    sel = jnp.zeros((tq, LANE), F32)
    for _ in range(SEL_TOPK):
        mx = jnp.max(val, -1, keepdims=True)
        idx = jnp.min(jnp.where(val == mx, lane, LANE), -1, keepdims=True)
        hit = lane == idx
        sel = jnp.where(hit, 1.0, sel)
        val = jnp.where(hit, -jnp.inf, val)
    sel_b = sel.astype(BF16)

    blk_row = lax.broadcasted_iota(jnp.int32, (LANE, kc_len), 0)
    kpos_l = lax.broadcasted_iota(jnp.int32, (LANE, kc_len), 1)
    kpos_q = lax.broadcasted_iota(jnp.int32, (rows4, kc_len), 1)
    tq_pos = qpos(kc_len)

    def sel_body(c, carry):
        m_i, l_i, acc = carry
        k0 = pl.multiple_of(c * kc_len, kc_len)
        expand = jnp.where(((k0 + kpos_l) >> SEL_SHIFT) == blk_row, 1.0, 0.0).astype(BF16)
        picked = tile4(jnp.dot(sel_b, expand, preferred_element_type=F32))
        msk = (picked > 0.5) & (k0 + kpos_q <= tq_pos)
        s = _bdot_nt(qs, ks_ref[pl.ds(k0, kc_len), :])
        s = jnp.where(msk, s, NEG)
        m_new = jnp.maximum(m_i, jnp.max(s, -1, keepdims=True))
        p = jnp.where(msk, jnp.exp(s - m_new), 0.0)
        alpha = jnp.exp(m_i - m_new)
        l_new = alpha * l_i + jnp.sum(p, -1, keepdims=True)
        acc = alpha * acc + _bdot(p, vs_ref[pl.ds(k0, kc_len), :])
        return m_new, l_new, acc

    n_kc = (t0 + tq + kc_len - 1) // kc_len
    m_s, l_s, acc_s = lax.fori_loop(
        0, n_kc, sel_body,
        (jnp.full((rows4, 1), NEG, F32), jnp.zeros((rows4, 1), F32), jnp.zeros((rows4, NSA_DH), F32)))
    o_s = acc_s / jnp.maximum(l_s, 1e-30)

    n_w = WINDOW // tq
    s_parts, v_parts = [], []
    tpos = qpos(tq)
    for j in range(n_w + 1):
        cj = qi - n_w + j
        k0 = pl.multiple_of(jnp.maximum(cj, 0) * tq, tq)
        kpos = cj * tq + lax.broadcasted_iota(jnp.int32, (rows4, tq), 1)
        rel = tpos - kpos
        wmask = (rel >= 0) & (rel < WINDOW) & (kpos >= 0)
        s_parts.append(jnp.where(wmask, _bdot_nt(qs, kw_ref[pl.ds(k0, tq), :]), NEG))
        v_parts.append(vw_ref[pl.ds(k0, tq), :])
    s_w = jnp.concatenate(s_parts, axis=1)
    m_w = jnp.max(s_w, -1, keepdims=True)
    e_w = jnp.exp(s_w - m_w)
    p_w = e_w / jnp.maximum(jnp.sum(e_w, -1, keepdims=True), 1e-30)
    o_w = _bdot(p_w, jnp.concatenate(v_parts, axis=0))

    gates = _sigmoid(sm_ref[...])
    outs = []
    for hh in range(hpg):
        base = SM_NGATE + (g * hpg + hh) * 3
        r = slice(hh * tq, (hh + 1) * tq)
        outs.append(_lane_col(gates, base) * o_c[r] + _lane_col(gates, base + 1) * o_s[r]
                    + _lane_col(gates, base + 2) * o_w[r])
    o_ref[...] = jnp.concatenate(outs, axis=1).astype(o_ref.dtype)


def _nsa(q_r, kvc, kv_r, proj3, overlap, tq=128, kc_len=256):
    b, s, _ = q_r.shape
    nr = kvc.shape[2]
    gw = NSA_HPG * LANE
    kvspec = lambda off: pl.BlockSpec((None, s, LANE), lambda i, g, j, off=off: (i, 0, off + g))
    cspec = lambda off: pl.BlockSpec((None, None, nr, NSA_DH), lambda i, g, j, off=off: (i, off + g, 0, 0))
    return pl.pallas_call(
        functools.partial(_nsa_kernel, tq=tq, kc_len=kc_len),
        grid=(b, NSA_KV_HEADS, s // tq),
        in_specs=[pl.BlockSpec((None, tq, gw), lambda i, g, j: (i, j, g)),
                  cspec(0), cspec(2),
                  kvspec(0), kvspec(2), kvspec(4), kvspec(6),
                  pl.BlockSpec((None, tq, LANE), lambda i, g, j: (i, j, CB_SMALL)),
                  pl.BlockSpec((LANE, LANE), lambda i, g, j: (0, 0))],
        out_specs=pl.BlockSpec((None, tq, gw), lambda i, g, j: (i, j, g)),
        out_shape=jax.ShapeDtypeStruct((b, s, NSA_HEADS * NSA_DH), F32),
        compiler_params=_cp(("parallel", "parallel", "arbitrary")),
        name="nsa",
    )(q_r, kvc, kvc, kv_r, kv_r, kv_r, kv_r, proj3, overlap)


def _merge_kernel(ya_ref, yb_ref, mg_ref, x_ref, w_ref, g_ref, b_ref, o_ref):
    mg = mg_ref[...]
    merged = _sigmoid(mg[:, :D_MODEL]) * ya_ref[...] + _sigmoid(mg[:, D_MODEL:]) * yb_ref[...]
    mix = _bdot(merged, w_ref[...])
    o_ref[...] = _layer_norm(DN_ALPHA * x_ref[...] + mix, g_ref[...], b_ref[...])


def _merge(y_a, y_b, proj, x2, w_out, g, b, tm=512):
    n = x2.shape[0]
    row = lambda w: pl.BlockSpec((tm, w), lambda i: (i, 0))
    full = lambda shape: pl.BlockSpec(shape, lambda i: (0, 0))
    return pl.pallas_call(
        _merge_kernel,
        grid=(n // tm,),
        in_specs=[row(D_MODEL), row(D_MODEL),
                  pl.BlockSpec((tm, 2 * D_MODEL), lambda i: (i, CB_MG * LANE // (2 * D_MODEL))),
                  row(D_MODEL), full((D_MODEL, D_MODEL)), full((1, D_MODEL)), full((1, D_MODEL))],
        out_specs=row(D_MODEL),
        out_shape=jax.ShapeDtypeStruct((n, D_MODEL), F32),
        compiler_params=_cp(("parallel",)),
        name="merge_ln1",
    )(y_a, y_b, proj, x2, w_out, g, b)


def _xattn_kernel(h_ref, kv_ref, wq_ref, wo_ref, g_ref, b_ref, o_ref):
    h = h_ref[...]
    q = _bdot(h, wq_ref[...]).astype(BF16)
    kv = kv_ref[...]
    outs = []
    for hh in range(XA_HEADS):
        sl = slice(hh * XA_DH, (hh + 1) * XA_DH)
        s = _bdot_nt(q[:, sl], kv[:, sl]) * (XA_DH ** -0.5)
        m = jnp.max(s, -1, keepdims=True)
        e = jnp.exp(s - m)
        p = e / jnp.sum(e, -1, keepdims=True)
        outs.append(_bdot(p, kv[:, D_MODEL + hh * XA_DH:D_MODEL + (hh + 1) * XA_DH]))
    o = jnp.concatenate(outs, axis=1)
    o_ref[...] = _layer_norm(DN_ALPHA * h + _bdot(o, wo_ref[...]), g_ref[...], b_ref[...])


def _xattn(h3, kv3, wq, wo, g, b, tm=512):
    bsz, s, _ = h3.shape
    m = kv3.shape[1]
    full = lambda shape: pl.BlockSpec(shape, lambda i, j: (0, 0))
    return pl.pallas_call(
        _xattn_kernel,
        grid=(bsz, s // tm),
        in_specs=[pl.BlockSpec((None, tm, D_MODEL), lambda i, j: (i, j, 0)),
                  pl.BlockSpec((None, m, 2 * D_MODEL), lambda i, j: (i, 0, 0)),
                  full((D_MODEL, D_MODEL)), full((D_MODEL, D_MODEL)), full((1, D_MODEL)), full((1, D_MODEL))],
        out_specs=pl.BlockSpec((None, tm, D_MODEL), lambda i, j: (i, j, 0)),
        out_shape=jax.ShapeDtypeStruct((bsz, s, D_MODEL), F32),
        compiler_params=_cp(("parallel", "parallel")),
        name="xattn_ln2",
    )(h3, kv3, wq, wo, g, b)


def _route(x, wr_h, wr_l, br):
    xh = x.astype(BF16)
    xl = (x - xh.astype(F32)).astype(BF16)
    logits = (jnp.dot(xh, wr_h, preferred_element_type=F32) + jnp.dot(xh, wr_l, preferred_element_type=F32)
              + jnp.dot(xl, wr_h, preferred_element_type=F32)) + br
    lane = lax.broadcasted_iota(jnp.int32, logits.shape, 1)
    gmask = lane < N_GROUPS
    gl = jnp.where(gmask, logits, -jnp.inf)
    gmax = jnp.max(gl, -1, keepdims=True)
    g_idx = jnp.min(jnp.where(gl == gmax, lane, LANE), -1, keepdims=True)
    p_top = 1.0 / jnp.sum(jnp.where(gmask, jnp.exp(gl - gmax), 0.0), -1, keepdims=True)
    e_lane = lane - N_GROUPS
    emask = (e_lane >= 0) & (e_lane < N_EXPERTS) & ((e_lane >> 3) == g_idx)
    el = jnp.where(emask, logits, -jnp.inf)
    emax = jnp.max(el, -1, keepdims=True)
    ee = jnp.where(emask, jnp.exp(el - emax), 0.0)
    pe = ee / jnp.sum(ee, -1, keepdims=True)
    pv = jnp.where(emask, pe, -jnp.inf)
    m1 = jnp.max(pv, -1, keepdims=True)
    i1 = jnp.min(jnp.where(pv == m1, lane, LANE), -1, keepdims=True)
    pv2 = jnp.where(lane == i1, -jnp.inf, pv)
    m2 = jnp.max(pv2, -1, keepdims=True)
    i2 = jnp.min(jnp.where(pv2 == m2, lane, LANE), -1, keepdims=True)
    denom = m1 + m2
    return jnp.where(lane == i1, p_top * m1 / denom, 0.0) + jnp.where(lane == i2, p_top * m2 / denom, 0.0)


def _moe_kernel(h_ref, wrh_ref, wrl_ref, br_ref, wg_ref, wu_ref, wd_ref, g_ref, b_ref, o_ref,
                xb_ref, comb_ref, acc_ref):
    e = pl.program_id(1)

    @pl.when(e == 0)
    def _():
        x = h_ref[...]
        xb_ref[...] = x.astype(BF16)
        comb_ref[...] = _route(x, wrh_ref[...], wrl_ref[...], br_ref[...])
        acc_ref[...] = jnp.zeros_like(acc_ref)

    xb = xb_ref[...]
    cw = _lane_col(comb_ref[...], e + N_GROUPS)
    gate = jnp.dot(xb, wg_ref[...], preferred_element_type=F32)
    up = jnp.dot(xb, wu_ref[...], preferred_element_type=F32)
    acc_ref[...] += _bdot(_silu(gate) * up * cw, wd_ref[...])

    @pl.when(e == N_EXPERTS - 1)
    def _():
        o_ref[...] = _layer_norm(DN_ALPHA * h_ref[...] + acc_ref[...], g_ref[...], b_ref[...])


def _moe(h2, wr_h, wr_l, br, wg, wu, wd, g, b, tm=1024):
    n = h2.shape[0]
    full = lambda shape: pl.BlockSpec(shape, lambda i, e: (0, 0))
    return pl.pallas_call(
        _moe_kernel,
        grid=(n // tm, N_EXPERTS),
        in_specs=[pl.BlockSpec((tm, D_MODEL), lambda i, e: (i, 0)),
                  full((D_MODEL, LANE)), full((D_MODEL, LANE)), full((1, LANE)),
                  pl.BlockSpec((None, D_MODEL, D_FF), lambda i, e: (e, 0, 0)),
                  pl.BlockSpec((None, D_MODEL, D_FF), lambda i, e: (e, 0, 0)),
                  pl.BlockSpec((None, D_FF, D_MODEL), lambda i, e: (e, 0, 0)),
                  full((1, D_MODEL)), full((1, D_MODEL))],
        out_specs=pl.BlockSpec((tm, D_MODEL), lambda i, e: (i, 0)),
        out_shape=jax.ShapeDtypeStruct((n, D_MODEL), F32),
        scratch_shapes=[pltpu.VMEM((tm, D_MODEL), BF16), pltpu.VMEM((tm, LANE), F32),
                        pltpu.VMEM((tm, D_MODEL), F32)],
        compiler_params=_cp(("parallel", "arbitrary")),
        name="moe_ln3",
    )(h2, wr_h, wr_l, br, wg, wu, wd, g, b)


def _regroup_w_in(w):
    sizes = (1024, 1024, 1024, 1024, 8, 8, 1024, 256, 256, 256, 256, 256, 256, 24, 2048)
    offs = [0]
    for sz in sizes:
        offs.append(offs[-1] + sz)
    seg = lambda i: w[:, offs[i]:offs[i + 1]]
    small = jnp.concatenate([seg(4), seg(5), seg(13)], axis=1)
    small = jnp.pad(small, ((0, 0), (0, LANE - small.shape[1])))
    big = jnp.concatenate([seg(0), seg(1), seg(2), seg(3), seg(14), seg(6), seg(7), seg(8), seg(9), seg(10),
                           seg(11), seg(12), small], axis=1)
    return jnp.pad(big, ((0, 0), (0, PROJ_COLS - big.shape[1])))


def _overlap_matrix(s):
    nb = s // SEL_BLOCK
    c0 = jnp.arange(LANE) * CMP_STRIDE
    s0 = jnp.arange(LANE) * SEL_BLOCK
    ov = jnp.minimum(c0[:, None] + CMP_BLOCK, s0[None, :] + SEL_BLOCK) - jnp.maximum(c0[:, None], s0[None, :])
    ov = jnp.maximum(ov, 0).astype(F32) / CMP_BLOCK
    nc = (s - CMP_BLOCK) // CMP_STRIDE + 1
    keep = (jnp.arange(LANE)[:, None] < nc) & (jnp.arange(LANE)[None, :] < nb)
    return jnp.where(keep, ov, 0.0).astype(BF16)


def _layer(h, mem, cos2, sin2, w_in, conv_w, a_log, dt_bias, norm_w, cmp_pe, cmp_w1, cmp_b1, cmp_w2, w_out,
           ln1_g, ln1_b, xa_wq, xa_wkv, xa_wo, ln2_g, ln2_b, w_group, b_group, w_expert, b_expert,
           w_gate, w_up, w_down, ln3_g, ln3_b):
    b, s, d = h.shape
    n = b * s
    x2 = h.reshape(n, d)
    proj = _matmul(x2.astype(BF16), _regroup_w_in(w_in).astype(BF16), F32, tm=1024, tn=512)
    proj3 = proj.reshape(b, s, PROJ_COLS)

    y_a = _gdn(proj3, conv_w, a_log, dt_bias, norm_w)

    q_r, ckv_r, kv_r = _rope(proj3, cos2, sin2)
    ckv_r = ckv_r.reshape(b, 4, s // CMP_STRIDE, CMP_STRIDE * NSA_DH)
    kvc = _compress(ckv_r, cmp_pe.reshape(2, 1, CMP_BLOCK * NSA_DH).astype(BF16), cmp_w1.astype(BF16),
                    cmp_b1.reshape(2, 1, CMP_HIDDEN), cmp_w2.astype(BF16))
    y_b = _nsa(q_r, kvc, kv_r, proj3, _overlap_matrix(s))

    row = lambda v: v.reshape(1, -1)
    h1 = _merge(y_a.reshape(n, d), y_b.reshape(n, d), proj, x2, w_out.astype(BF16), row(ln1_g), row(ln1_b))

    m = mem.shape[1]
    kv = _matmul(mem.reshape(b * m, d).astype(BF16), xa_wkv.astype(BF16), BF16, tm=512, tn=512)
    h2 = _xattn(h1.reshape(b, s, d), kv.reshape(b, m, 2 * d), xa_wq.astype(BF16), xa_wo.astype(BF16),
                row(ln2_g), row(ln2_b))

    wr = jnp.pad(jnp.concatenate([w_group, w_expert], axis=1), ((0, 0), (0, LANE - N_GROUPS - N_EXPERTS)))
    wr_h = wr.astype(BF16)
    wr_l = (wr - wr_h.astype(F32)).astype(BF16)
    br = jnp.pad(jnp.concatenate([b_group, b_expert]), (0, LANE - N_GROUPS - N_EXPERTS)).reshape(1, LANE)
    h3 = _moe(h2.reshape(n, d), wr_h, wr_l, br, w_gate.astype(BF16), w_up.astype(BF16), w_down.astype(BF16),
              row(ln3_g), row(ln3_b))
    return h3.reshape(b, s, d)


def kernel(x, mem, positions, w_in, gdn_conv_w, gdn_a_log, gdn_dt_bias, gdn_norm_w, cmp_pe, cmp_w1, cmp_b1, cmp_w2, w_out, ln1_g, ln1_b, xa_wq, xa_wkv, xa_wo, ln2_g, ln2_b, moe_w_group, moe_b_group, moe_w_expert, moe_b_expert, moe_w_gate, moe_w_up, moe_w_down, ln3_g, ln3_b):
    half = NSA_DH // 2
    inv_freq = ROPE_THETA ** (-jnp.arange(half, dtype=F32) / half)
    ang = positions.astype(F32)[..., None] * inv_freq
    cos, sin = jnp.cos(ang), jnp.sin(ang)
    cos2 = jnp.concatenate([cos, cos], -1)
    sin2 = jnp.concatenate([-sin, sin], -1)
    h = x
    for l in range(DEPTH):
        h = _layer(h, mem, cos2, sin2, w_in[l], gdn_conv_w[l], gdn_a_log[l], gdn_dt_bias[l], gdn_norm_w[l],
                   cmp_pe[l], cmp_w1[l], cmp_b1[l], cmp_w2[l], w_out[l], ln1_g[l], ln1_b[l],
                   xa_wq[l], xa_wkv[l], xa_wo[l], ln2_g[l], ln2_b[l], moe_w_group[l], moe_b_group[l],
                   moe_w_expert[l], moe_b_expert[l], moe_w_gate[l], moe_w_up[l], moe_w_down[l],
                   ln3_g[l], ln3_b[l])
    return h
```

```python
import functools

import jax
import jax.numpy as jnp
from jax import lax
from jax.experimental import pallas as pl
from jax.experimental.pallas import tpu as pltpu

F32 = jnp.float32
BF16 = jnp.bfloat16

D_MODEL = 1024
LANE = 128
GDN_HEADS = 8
GDN_D = 128
GDN_CONV = 4
GDN_CHUNK = 64
NSA_HEADS = 8
NSA_KV_HEADS = 2
NSA_HPG = NSA_HEADS // NSA_KV_HEADS
NSA_DH = 128
CMP_BLOCK = 32
CMP_STRIDE = 16
CMP_HIDDEN = 256
SEL_BLOCK = 64
SEL_SHIFT = 6
SEL_TOPK = 8
WINDOW = 256
XA_HEADS = 4
XA_DH = 256
N_GROUPS = 4
EXPERTS_PER_GROUP = 8
N_EXPERTS = 32
D_FF = 256
DEPTH = 1
DN_ALPHA = (2.0 * DEPTH) ** 0.25
LN_EPS = 1e-5
RMS_EPS = 1e-6
ROPE_THETA = 10000.0
NEG = -1e30

CB_Q, CB_K, CB_V, CB_Z = 0, 8, 16, 24
CB_MG = 32
CB_NQ = 48
CB_KV = 56
CB_SMALL = 68
PROJ_COLS = 72 * LANE
SM_BETA, SM_DECAY, SM_NGATE = 0, 8, 16

VMEM_LIMIT = 48 * 1024 * 1024


def _cp(sem, vmem=VMEM_LIMIT):
    return pltpu.CompilerParams(dimension_semantics=sem, vmem_limit_bytes=vmem)


def _bdot(a, b):
    return jnp.dot(a.astype(BF16), b.astype(BF16), preferred_element_type=F32)


def _bdot_nt(a, b):
    return lax.dot_general(a.astype(BF16), b.astype(BF16), (((1,), (1,)), ((), ())),
                           preferred_element_type=F32)


def _bdot_tn(a, b):
    return lax.dot_general(a.astype(BF16), b.astype(BF16), (((0,), (0,)), ((), ())),
                           preferred_element_type=F32)


def _split3(x):
    h = x.astype(BF16)
    r = x - h.astype(F32)
    m = r.astype(BF16)
    l = (r - m.astype(F32)).astype(BF16)
    return h, m, l


def _sigmoid(x):
    return 1.0 / (1.0 + jnp.exp(-x))


def _silu(x):
    return x * _sigmoid(x)


def _layer_norm(x, g, b):
    mu = jnp.mean(x, -1, keepdims=True)
    xc = x - mu
    var = jnp.mean(xc * xc, -1, keepdims=True)
    return xc * lax.rsqrt(var + LN_EPS) * g + b


def _lane_col(x, c):
    lane = lax.broadcasted_iota(jnp.int32, x.shape, 1)
    return jnp.sum(jnp.where(lane == c, x, 0.0), axis=-1, keepdims=True)


def _mm_kernel(x_ref, w_ref, o_ref):
    o_ref[...] = jnp.dot(x_ref[...], w_ref[...], preferred_element_type=F32).astype(o_ref.dtype)


def _matmul(x, w, out_dtype, tm, tn):
    m, k = x.shape
    n = w.shape[1]
    return pl.pallas_call(
        _mm_kernel,
        grid=(m // tm, n // tn),
        in_specs=[pl.BlockSpec((tm, k), lambda i, j: (i, 0)),
                  pl.BlockSpec((k, tn), lambda i, j: (0, j))],
        out_specs=pl.BlockSpec((tm, tn), lambda i, j: (i, j)),
        out_shape=jax.ShapeDtypeStruct((m, n), out_dtype),
        compiler_params=_cp(("parallel", "parallel")),
        name="matmul",
    )(x, w)


def _gdn_kernel(q_ref, k_ref, v_ref, z_ref, sm_ref, cwq_ref, cwk_ref, cwv_ref,
                alog_ref, dtb_ref, nw_ref, o_ref, state_ref, tail_ref):
    c_len = q_ref.shape[0]
    width = GDN_HEADS * GDN_D

    @pl.when(pl.program_id(1) == 0)
    def _():
        state_ref[...] = jnp.zeros_like(state_ref)
        tail_ref[...] = jnp.zeros_like(tail_ref)

    row = lax.broadcasted_iota(jnp.int32, (c_len, c_len), 0)
    col = lax.broadcasted_iota(jnp.int32, (c_len, c_len), 1)
    causal = row >= col
    strict = row > col
    eye = jnp.where(row == col, 1.0, 0.0)
    tril_incl = jnp.where(causal, 1.0, 0.0).astype(BF16)

    def conv_silu(idx, x_ref, w_ref):
        cur = x_ref[...]
        xc = jnp.concatenate([tail_ref[idx], cur], axis=0)
        tail_ref[idx] = cur[c_len - 8:, :]
        w = w_ref[...]
        acc = xc[8:, :] * w[GDN_CONV - 1:GDN_CONV, :]
        for j in range(GDN_CONV - 1):
            off = 8 - (GDN_CONV - 1) + j
            acc = acc + xc[off:off + c_len, :] * w[j:j + 1, :]
        return _silu(acc)

    q_all = conv_silu(0, q_ref, cwq_ref)
    k_all = conv_silu(1, k_ref, cwk_ref)
    v_all = conv_silu(2, v_ref, cwv_ref)

    sm = sm_ref[...]
    beta_all = _sigmoid(sm)
    a_in = sm + dtb_ref[...]
    softplus = jnp.maximum(a_in, 0.0) + jnp.log(1.0 + jnp.exp(-jnp.abs(a_in)))
    g_all = -jnp.exp(alog_ref[...]) * softplus
    gh, gm, gl = _split3(g_all)
    gc_all = (jnp.dot(tril_incl, gh, preferred_element_type=F32)
              + jnp.dot(tril_incl, gm, preferred_element_type=F32)
              + jnp.dot(tril_incl, gl, preferred_element_type=F32))
    gc_t = gc_all.T

    heads = range(GDN_HEADS)
    hs = lambda x, h: x[:, h * GDN_D:(h + 1) * GDN_D]
    q = [hs(q_all, h) for h in heads]
    k = [hs(k_all, h) for h in heads]
    v = [hs(v_all, h) for h in heads]
    q = [x * lax.rsqrt(jnp.sum(x * x, -1, keepdims=True) + RMS_EPS) * (GDN_D ** -0.5) for x in q]
    k = [x * lax.rsqrt(jnp.sum(x * x, -1, keepdims=True) + RMS_EPS) for x in k]
    beta = [beta_all[:, SM_BETA + h:SM_BETA + h + 1] for h in heads]
    gc = [gc_all[:, SM_DECAY + h:SM_DECAY + h + 1] for h in heads]
    g_last = [x[c_len - 1:c_len, :] for x in gc]
    diff = [gc[h] - gc_t[SM_DECAY + h:SM_DECAY + h + 1, :] for h in heads]
    decay = [jnp.where(causal, jnp.exp(jnp.where(causal, d, 0.0)), 0.0) for d in diff]
    kk = [_bdot_nt(x, x) for x in k]
    m_pow = [-(jnp.where(strict, kk[h] * decay[h], 0.0) * beta[h]) for h in heads]
    t_inv = [eye + m for m in m_pow]
    for _ in range((c_len - 1).bit_length() - 1):
        m_pow = [_bdot(m, m) for m in m_pow]
        t_inv = [t + _bdot(t, m) for t, m in zip(t_inv, m_pow)]
    e_gc = [jnp.exp(x) for x in gc]
    u = [_bdot(t_inv[h], v[h] * beta[h]) for h in heads]
    w = [_bdot(t_inv[h], k[h] * (beta[h] * e_gc[h])) for h in heads]
    qk = [_bdot_nt(q[h], k[h]) * decay[h] for h in heads]
    q_dec = [q[h] * e_gc[h] for h in heads]
    k_dec = [k[h] * jnp.exp(g_last[h] - gc[h]) for h in heads]
    state = [state_ref[h] for h in heads]
    v_new = [u[h] - _bdot(w[h], state[h]) for h in heads]
    o = [_bdot(q_dec[h], state[h]) + _bdot(qk[h], v_new[h]) for h in heads]
    for h in heads:
        state_ref[h] = state[h] * jnp.exp(g_last[h]) + _bdot_tn(k_dec[h], v_new[h])
    nw = nw_ref[...]
    o = [x * lax.rsqrt(jnp.mean(x * x, -1, keepdims=True) + RMS_EPS) * nw for x in o]
    o_ref[...] = (jnp.concatenate(o, axis=1) * _silu(z_ref[...])).astype(o_ref.dtype)


def _gdn(proj3, conv_w, a_log, dt_bias, norm_w, c_len=128):
    b, s, _ = proj3.shape
    width = GDN_HEADS * GDN_D
    col = lambda off: pl.BlockSpec((None, c_len, width), lambda i, j, off=off: (i, j, off // GDN_HEADS))
    cw = lambda k: pl.BlockSpec((GDN_CONV, width), lambda i, j, k=k: (0, k))
    full = lambda shape: pl.BlockSpec(shape, lambda i, j: (0, 0))
    pad = lambda v: jnp.pad(v, (SM_DECAY, LANE - SM_DECAY - GDN_HEADS)).reshape(1, LANE)
    return pl.pallas_call(
        _gdn_kernel,
        grid=(b, s // c_len),
        in_specs=[col(CB_Q), col(CB_K), col(CB_V), col(CB_Z),
                  pl.BlockSpec((None, c_len, LANE), lambda i, j: (i, j, CB_SMALL)),
                  cw(0), cw(1), cw(2),
                  full((1, LANE)), full((1, LANE)), full((1, GDN_D))],
        out_specs=pl.BlockSpec((None, c_len, width), lambda i, j: (i, j, 0)),
        out_shape=jax.ShapeDtypeStruct((b, s, width), F32),
        scratch_shapes=[pltpu.VMEM((GDN_HEADS, GDN_D, GDN_D), F32), pltpu.VMEM((3, 8, width), F32)],
        compiler_params=_cp(("parallel", "arbitrary")),
        name="gdn",
    )(proj3, proj3, proj3, proj3, proj3, conv_w, conv_w, conv_w,
      pad(a_log), pad(dt_bias), norm_w.reshape(1, GDN_D))


def _rope_kernel(nq_ref, kv0_ref, kv1_ref, kv2_ref, cos_ref, sin_ref, q_out, ckv_out, kv_out):
    cos2 = cos_ref[...]
    sin2 = sin_ref[...]

    def rope(x):
        return x * cos2 + pltpu.roll(x, NSA_DH // 2, 1) * sin2

    scale = NSA_DH ** -0.5
    q_out[...] = jnp.concatenate(
        [rope(nq_ref[:, hh * LANE:(hh + 1) * LANE]) * scale for hh in range(NSA_HEADS)],
        axis=1).astype(q_out.dtype)
    kv_refs = (kv0_ref, kv1_ref, kv2_ref)
    blk = lambda i: kv_refs[i // 4][:, (i % 4) * LANE:(i % 4 + 1) * LANE]
    ckv_out[0] = rope(blk(0)).astype(ckv_out.dtype)
    ckv_out[1] = rope(blk(1)).astype(ckv_out.dtype)
    ckv_out[2] = blk(2).astype(ckv_out.dtype)
    ckv_out[3] = blk(3).astype(ckv_out.dtype)
    kv_out[...] = jnp.concatenate(
        [rope(blk(4)), rope(blk(5)), blk(6), blk(7), rope(blk(8)), rope(blk(9)), blk(10), blk(11)],
        axis=1).astype(kv_out.dtype)


def _rope(proj3, cos2, sin2, ts=512):
    b, s, _ = proj3.shape
    kvspec = lambda k: pl.BlockSpec((None, ts, 4 * LANE), lambda i, j, k=k: (i, j, CB_KV // 4 + k))
    return pl.pallas_call(
        _rope_kernel,
        grid=(b, s // ts),
        in_specs=[pl.BlockSpec((None, ts, NSA_HEADS * LANE), lambda i, j: (i, j, CB_NQ // NSA_HEADS)),
                  kvspec(0), kvspec(1), kvspec(2),
                  pl.BlockSpec((None, ts, LANE), lambda i, j: (i, j, 0)),
                  pl.BlockSpec((None, ts, LANE), lambda i, j: (i, j, 0))],
        out_specs=[pl.BlockSpec((None, ts, NSA_HEADS * LANE), lambda i, j: (i, j, 0)),
                   pl.BlockSpec((None, 4, ts, LANE), lambda i, j: (i, 0, j, 0)),
                   pl.BlockSpec((None, ts, 8 * LANE), lambda i, j: (i, j, 0))],
        out_shape=[jax.ShapeDtypeStruct((b, s, NSA_HEADS * LANE), BF16),
                   jax.ShapeDtypeStruct((b, 4, s, LANE), BF16),
                   jax.ShapeDtypeStruct((b, s, 8 * LANE), BF16)],
        compiler_params=_cp(("parallel", "parallel")),
        name="rope",
    )(proj3, proj3, proj3, proj3, cos2, sin2)


def _compress_kernel(x_ref, pe_ref, w1_ref, b1_ref, w2_ref, o_ref):
    x = x_ref[...]
    half = CMP_STRIDE * NSA_DH
    first = jnp.dot(x, w1_ref[:half, :], preferred_element_type=F32)
    second = jnp.dot(x, w1_ref[half:, :], preferred_element_type=F32)
    n_rows = x.shape[0]
    second = pltpu.roll(second, n_rows - 1, 0)
    pe = jnp.broadcast_to(pe_ref[...], (8, pe_ref.shape[1]))
    bias = jnp.dot(pe, w1_ref[...], preferred_element_type=F32)[0:1] + b1_ref[...]
    hid = first + second + bias
    gelu = 0.5 * hid * (1.0 + jnp.tanh(0.7978845608028654 * (hid + 0.044715 * hid * hid * hid)))
    o_ref[...] = _bdot(gelu, w2_ref[...]).astype(o_ref.dtype)


def _compress(ckv_r, pe, w1, b1, w2):
    b, _, nr, wd = ckv_r.shape
    return pl.pallas_call(
        _compress_kernel,
        grid=(b, 4),
        in_specs=[pl.BlockSpec((None, None, nr, wd), lambda i, j: (i, j, 0, 0)),
                  pl.BlockSpec((None, 1, CMP_BLOCK * NSA_DH), lambda i, j: (j // 2, 0, 0)),
                  pl.BlockSpec((None, CMP_BLOCK * NSA_DH, CMP_HIDDEN), lambda i, j: (j // 2, 0, 0)),
                  pl.BlockSpec((None, 1, CMP_HIDDEN), lambda i, j: (j // 2, 0, 0)),
                  pl.BlockSpec((None, CMP_HIDDEN, NSA_DH), lambda i, j: (j // 2, 0, 0))],
        out_specs=pl.BlockSpec((None, None, nr, NSA_DH), lambda i, j: (i, j, 0, 0)),
        out_shape=jax.ShapeDtypeStruct((b, 4, nr, NSA_DH), BF16),
        compiler_params=_cp(("parallel", "arbitrary")),
        name="compress",
    )(ckv_r, pe, w1, b1, w2)


def _nsa_kernel(q_ref, kc_ref, vc_ref, ks_ref, vs_ref, kw_ref, vw_ref, sm_ref, ov_ref, o_ref, *, tq, kc_len):
    g = pl.program_id(1)
    qi = pl.program_id(2)
    t0 = qi * tq
    hpg = NSA_HPG
    rows4 = hpg * tq
    q = q_ref[...]
    qs = jnp.concatenate([q[:, hh * LANE:(hh + 1) * LANE] for hh in range(hpg)], axis=0)
    tile4 = lambda m: jnp.concatenate([m] * hpg, axis=0)

    def qpos(width):
        return t0 + (lax.broadcasted_iota(jnp.int32, (rows4, width), 0) & (tq - 1))

    lane = lax.broadcasted_iota(jnp.int32, (tq, LANE), 1)
    t = t0 + lax.broadcasted_iota(jnp.int32, (tq, LANE), 0)

    s_c = _bdot_nt(qs, kc_ref[...])
    lane4 = lax.broadcasted_iota(jnp.int32, (rows4, LANE), 1)
    cmask = lane4 * CMP_STRIDE + (CMP_BLOCK - 1) <= qpos(LANE)
    s_c = jnp.where(cmask, s_c, NEG)
    m_c = jnp.max(s_c, -1, keepdims=True)
    e_c = jnp.where(cmask, jnp.exp(s_c - m_c), 0.0)
    p_c = e_c / jnp.maximum(jnp.sum(e_c, -1, keepdims=True), 1e-30)
    o_c = _bdot(p_c, vc_ref[...])

    p_sum = p_c[0:tq]
    for hh in range(1, hpg):
        p_sum = p_sum + p_c[hh * tq:(hh + 1) * tq]
    ph, pm, plo = _split3(p_sum)
    ov = ov_ref[...]
    imp = (jnp.dot(ph, ov, preferred_element_type=F32) + jnp.dot(pm, ov, preferred_element_type=F32)
           + jnp.dot(plo, ov, preferred_element_type=F32))
    cur = t >> SEL_SHIFT
    valid = lane * SEL_BLOCK <= t
    forced = (lane == 0) | (lane == cur) | (lane == cur - 1)
    val = jnp.where(forced, jnp.inf, jnp.where(valid, imp, -jnp.inf))
    sel = jnp.zeros((tq, LANE), F32)
    for _ in range(SEL_TOPK):
        mx = jnp.max(val, -1, keepdims=True)
        idx = jnp.min(jnp.where(val == mx, lane, LANE), -1, keepdims=True)
        hit = lane == idx
        sel = jnp.where(hit, 1.0, sel)
        val = jnp.where(hit, -jnp.inf, val)
    sel_b = sel.astype(BF16)

    blk_row = lax.broadcasted_iota(jnp.int32, (LANE, kc_len), 0)
    kpos_l = lax.broadcasted_iota(jnp.int32, (LANE, kc_len), 1)
    kpos_q = lax.broadcasted_iota(jnp.int32, (rows4, kc_len), 1)
    tq_pos = qpos(kc_len)

    def sel_body(c, carry):
        m_i, l_i, acc = carry
        k0 = pl.multiple_of(c * kc_len, kc_len)
        expand = jnp.where(((k0 + kpos_l) >> SEL_SHIFT) == blk_row, 1.0, 0.0).astype(BF16)
        picked = tile4(jnp.dot(sel_b, expand, preferred_element_type=F32))
        msk = (picked > 0.5) & (k0 + kpos_q <= tq_pos)
        s = _bdot_nt(qs, ks_ref[pl.ds(k0, kc_len), :])
        s = jnp.where(msk, s, NEG)
        m_new = jnp.maximum(m_i, jnp.max(s, -1, keepdims=True))
        p = jnp.where(msk, jnp.exp(s - m_new), 0.0)
        alpha = jnp.exp(m_i - m_new)
        l_new = alpha * l_i + jnp.sum(p, -1, keepdims=True)
        acc = alpha * acc + _bdot(p, vs_ref[pl.ds(k0, kc_len), :])
        return m_new, l_new, acc

    n_kc = (t0 + tq + kc_len - 1) // kc_len
    m_s, l_s, acc_s = lax.fori_loop(
        0, n_kc, sel_body,
        (jnp.full((rows4, 1), NEG, F32), jnp.zeros((rows4, 1), F32), jnp.zeros((rows4, NSA_DH), F32)))
    o_s = acc_s / jnp.maximum(l_s, 1e-30)

    n_w = WINDOW // tq
    s_parts, v_parts = [], []
    tpos = qpos(tq)
    for j in range(n_w + 1):
        cj = qi - n_w + j
        k0 = pl.multiple_of(jnp.maximum(cj, 0) * tq, tq)
        kpos = cj * tq + lax.broadcasted_iota(jnp.int32, (rows4, tq), 1)
        rel = tpos - kpos
        wmask = (rel >= 0) & (rel < WINDOW) & (kpos >= 0)
        s_parts.append(jnp.where(wmask, _bdot_nt(qs, kw_ref[pl.ds(k0, tq), :]), NEG))
        v_parts.append(vw_ref[pl.ds(k0, tq), :])
    s_w = jnp.concatenate(s_parts, axis=1)
    m_w = jnp.max(s_w, -1, keepdims=True)
    e_w = jnp.exp(s_w - m_w)
    p_w = e_w / jnp.maximum(jnp.sum(e_w, -1, keepdims=True), 1e-30)
    o_w = _bdot(p_w, jnp.concatenate(v_parts, axis=0))

    gates = _sigmoid(sm_ref[...])
    outs = []
    for hh in range(hpg):
        base = SM_NGATE + (g * hpg + hh) * 3
        r = slice(hh * tq, (hh + 1) * tq)
        outs.append(_lane_col(gates, base) * o_c[r] + _lane_col(gates, base + 1) * o_s[r]
                    + _lane_col(gates, base + 2) * o_w[r])
    o_ref[...] = jnp.concatenate(outs, axis=1).astype(o_ref.dtype)


def _nsa(q_r, kvc, kv_r, proj3, overlap, tq=128, kc_len=256):
    b, s, _ = q_r.shape
    nr = kvc.shape[2]
    gw = NSA_HPG * LANE
    kvspec = lambda off: pl.BlockSpec((None, s, LANE), lambda i, g, j, off=off: (i, 0, off + g))
    cspec = lambda off: pl.BlockSpec((None, None, nr, NSA_DH), lambda i, g, j, off=off: (i, off + g, 0, 0))
    return pl.pallas_call(
        functools.partial(_nsa_kernel, tq=tq, kc_len=kc_len),
        grid=(b, NSA_KV_HEADS, s // tq),
        in_specs=[pl.BlockSpec((None, tq, gw), lambda i, g, j: (i, j, g)),
                  cspec(0), cspec(2),
                  kvspec(0), kvspec(2), kvspec(4), kvspec(6),
                  pl.BlockSpec((None, tq, LANE), lambda i, g, j: (i, j, CB_SMALL)),
                  pl.BlockSpec((LANE, LANE), lambda i, g, j: (0, 0))],
        out_specs=pl.BlockSpec((None, tq, gw), lambda i, g, j: (i, j, g)),
        out_shape=jax.ShapeDtypeStruct((b, s, NSA_HEADS * NSA_DH), F32),
        compiler_params=_cp(("parallel", "parallel", "arbitrary")),
        name="nsa",
    )(q_r, kvc, kvc, kv_r, kv_r, kv_r, kv_r, proj3, overlap)


def _merge_kernel(ya_ref, yb_ref, mg_ref, x_ref, w_ref, g_ref, b_ref, o_ref):
    mg = mg_ref[...]
    merged = _sigmoid(mg[:, :D_MODEL]) * ya_ref[...] + _sigmoid(mg[:, D_MODEL:]) * yb_ref[...]
    mix = _bdot(merged, w_ref[...])
    o_ref[...] = _layer_norm(DN_ALPHA * x_ref[...] + mix, g_ref[...], b_ref[...])


def _merge(y_a, y_b, proj, x2, w_out, g, b, tm=512):
    n = x2.shape[0]
    row = lambda w: pl.BlockSpec((tm, w), lambda i: (i, 0))
    full = lambda shape: pl.BlockSpec(shape, lambda i: (0, 0))
    return pl.pallas_call(
        _merge_kernel,
        grid=(n // tm,),
        in_specs=[row(D_MODEL), row(D_MODEL),
                  pl.BlockSpec((tm, 2 * D_MODEL), lambda i: (i, CB_MG * LANE // (2 * D_MODEL))),
                  row(D_MODEL), full((D_MODEL, D_MODEL)), full((1, D_MODEL)), full((1, D_MODEL))],
        out_specs=row(D_MODEL),
        out_shape=jax.ShapeDtypeStruct((n, D_MODEL), F32),
        compiler_params=_cp(("parallel",)),
        name="merge_ln1",
    )(y_a, y_b, proj, x2, w_out, g, b)


def _xattn_kernel(h_ref, kv_ref, wq_ref, wo_ref, g_ref, b_ref, o_ref):
    h = h_ref[...]
    q = _bdot(h, wq_ref[...]).astype(BF16)
    kv = kv_ref[...]
    outs = []
    for hh in range(XA_HEADS):
        sl = slice(hh * XA_DH, (hh + 1) * XA_DH)
        s = _bdot_nt(q[:, sl], kv[:, sl]) * (XA_DH ** -0.5)
        m = jnp.max(s, -1, keepdims=True)
        e = jnp.exp(s - m)
        p = e / jnp.sum(e, -1, keepdims=True)
        outs.append(_bdot(p, kv[:, D_MODEL + hh * XA_DH:D_MODEL + (hh + 1) * XA_DH]))
    o = jnp.concatenate(outs, axis=1)
    o_ref[...] = _layer_norm(DN_ALPHA * h + _bdot(o, wo_ref[...]), g_ref[...], b_ref[...])


def _xattn(h3, kv3, wq, wo, g, b, tm=512):
    bsz, s, _ = h3.shape
    m = kv3.shape[1]
    full = lambda shape: pl.BlockSpec(shape, lambda i, j: (0, 0))
    return pl.pallas_call(
        _xattn_kernel,
        grid=(bsz, s // tm),
        in_specs=[pl.BlockSpec((None, tm, D_MODEL), lambda i, j: (i, j, 0)),
                  pl.BlockSpec((None, m, 2 * D_MODEL), lambda i, j: (i, 0, 0)),
                  full((D_MODEL, D_MODEL)), full((D_MODEL, D_MODEL)), full((1, D_MODEL)), full((1, D_MODEL))],
        out_specs=pl.BlockSpec((None, tm, D_MODEL), lambda i, j: (i, j, 0)),
        out_shape=jax.ShapeDtypeStruct((bsz, s, D_MODEL), F32),
        compiler_params=_cp(("parallel", "parallel")),
        name="xattn_ln2",
    )(h3, kv3, wq, wo, g, b)


def _route(x, wr_h, wr_l, br):
    xh = x.astype(BF16)
    xl = (x - xh.astype(F32)).astype(BF16)
    logits = (jnp.dot(xh, wr_h, preferred_element_type=F32) + jnp.dot(xh, wr_l, preferred_element_type=F32)
              + jnp.dot(xl, wr_h, preferred_element_type=F32)) + br
    lane = lax.broadcasted_iota(jnp.int32, logits.shape, 1)
    gmask = lane < N_GROUPS
    gl = jnp.where(gmask, logits, -jnp.inf)
    gmax = jnp.max(gl, -1, keepdims=True)
    g_idx = jnp.min(jnp.where(gl == gmax, lane, LANE), -1, keepdims=True)
    p_top = 1.0 / jnp.sum(jnp.where(gmask, jnp.exp(gl - gmax), 0.0), -1, keepdims=True)
    e_lane = lane - N_GROUPS
    emask = (e_lane >= 0) & (e_lane < N_EXPERTS) & ((e_lane >> 3) == g_idx)
    el = jnp.where(emask, logits, -jnp.inf)
    emax = jnp.max(el, -1, keepdims=True)
    ee = jnp.where(emask, jnp.exp(el - emax), 0.0)
    pe = ee / jnp.sum(ee, -1, keepdims=True)
    pv = jnp.where(emask, pe, -jnp.inf)
    m1 = jnp.max(pv, -1, keepdims=True)
    i1 = jnp.min(jnp.where(pv == m1, lane, LANE), -1, keepdims=True)
    pv2 = jnp.where(lane == i1, -jnp.inf, pv)
    m2 = jnp.max(pv2, -1, keepdims=True)
    i2 = jnp.min(jnp.where(pv2 == m2, lane, LANE), -1, keepdims=True)
    denom = m1 + m2
    return jnp.where(lane == i1, p_top * m1 / denom, 0.0) + jnp.where(lane == i2, p_top * m2 / denom, 0.0)


def _moe_kernel(h_ref, wrh_ref, wrl_ref, br_ref, wg_ref, wu_ref, wd_ref, g_ref, b_ref, o_ref,
                xb_ref, comb_ref, acc_ref):
    e = pl.program_id(1)

    @pl.when(e == 0)
    def _():
        x = h_ref[...]
        xb_ref[...] = x.astype(BF16)
        comb_ref[...] = _route(x, wrh_ref[...], wrl_ref[...], br_ref[...])
        acc_ref[...] = jnp.zeros_like(acc_ref)

    xb = xb_ref[...]
    cw = _lane_col(comb_ref[...], e + N_GROUPS)
    gate = jnp.dot(xb, wg_ref[...], preferred_element_type=F32)
    up = jnp.dot(xb, wu_ref[...], preferred_element_type=F32)
    acc_ref[...] += _bdot(_silu(gate) * up * cw, wd_ref[...])

    @pl.when(e == N_EXPERTS - 1)
    def _():
        o_ref[...] = _layer_norm(DN_ALPHA * h_ref[...] + acc_ref[...], g_ref[...], b_ref[...])


def _moe(h2, wr_h, wr_l, br, wg, wu, wd, g, b, tm=1024):
    n = h2.shape[0]
    full = lambda shape: pl.BlockSpec(shape, lambda i, e: (0, 0))
    return pl.pallas_call(
        _moe_kernel,
        grid=(n // tm, N_EXPERTS),
        in_specs=[pl.BlockSpec((tm, D_MODEL), lambda i, e: (i, 0)),
                  full((D_MODEL, LANE)), full((D_MODEL, LANE)), full((1, LANE)),
                  pl.BlockSpec((None, D_MODEL, D_FF), lambda i, e: (e, 0, 0)),
                  pl.BlockSpec((None, D_MODEL, D_FF), lambda i, e: (e, 0, 0)),
                  pl.BlockSpec((None, D_FF, D_MODEL), lambda i, e: (e, 0, 0)),
                  full((1, D_MODEL)), full((1, D_MODEL))],
        out_specs=pl.BlockSpec((tm, D_MODEL), lambda i, e: (i, 0)),
        out_shape=jax.ShapeDtypeStruct((n, D_MODEL), F32),
        scratch_shapes=[pltpu.VMEM((tm, D_MODEL), BF16), pltpu.VMEM((tm, LANE), F32),
                        pltpu.VMEM((tm, D_MODEL), F32)],
        compiler_params=_cp(("parallel", "arbitrary")),
        name="moe_ln3",
    )(h2, wr_h, wr_l, br, wg, wu, wd, g, b)


def _regroup_w_in(w):
    sizes = (1024, 1024, 1024, 1024, 8, 8, 1024, 256, 256, 256, 256, 256, 256, 24, 2048)
    offs = [0]
    for sz in sizes:
        offs.append(offs[-1] + sz)
    seg = lambda i: w[:, offs[i]:offs[i + 1]]
    small = jnp.concatenate([seg(4), seg(5), seg(13)], axis=1)
    small = jnp.pad(small, ((0, 0), (0, LANE - small.shape[1])))
    big = jnp.concatenate([seg(0), seg(1), seg(2), seg(3), seg(14), seg(6), seg(7), seg(8), seg(9), seg(10),
                           seg(11), seg(12), small], axis=1)
    return jnp.pad(big, ((0, 0), (0, PROJ_COLS - big.shape[1])))


def _overlap_matrix(s):
    nb = s // SEL_BLOCK
    c0 = jnp.arange(LANE) * CMP_STRIDE
    s0 = jnp.arange(LANE) * SEL_BLOCK
    ov = jnp.minimum(c0[:, None] + CMP_BLOCK, s0[None, :] + SEL_BLOCK) - jnp.maximum(c0[:, None], s0[None, :])
    ov = jnp.maximum(ov, 0).astype(F32) / CMP_BLOCK
    nc = (s - CMP_BLOCK) // CMP_STRIDE + 1
    keep = (jnp.arange(LANE)[:, None] < nc) & (jnp.arange(LANE)[None, :] < nb)
    return jnp.where(keep, ov, 0.0).astype(BF16)


def _layer(h, mem, cos2, sin2, w_in, conv_w, a_log, dt_bias, norm_w, cmp_pe, cmp_w1, cmp_b1, cmp_w2, w_out,
           ln1_g, ln1_b, xa_wq, xa_wkv, xa_wo, ln2_g, ln2_b, w_group, b_group, w_expert, b_expert,
           w_gate, w_up, w_down, ln3_g, ln3_b):
    b, s, d = h.shape
    n = b * s
    x2 = h.reshape(n, d)
    proj = _matmul(x2.astype(BF16), _regroup_w_in(w_in).astype(BF16), F32, tm=1024, tn=512)
    proj3 = proj.reshape(b, s, PROJ_COLS)

    y_a = _gdn(proj3, conv_w, a_log, dt_bias, norm_w)

    q_r, ckv_r, kv_r = _rope(proj3, cos2, sin2)
    ckv_r = ckv_r.reshape(b, 4, s // CMP_STRIDE, CMP_STRIDE * NSA_DH)
    kvc = _compress(ckv_r, cmp_pe.reshape(2, 1, CMP_BLOCK * NSA_DH).astype(BF16), cmp_w1.astype(BF16),
                    cmp_b1.reshape(2, 1, CMP_HIDDEN), cmp_w2.astype(BF16))
    y_b = _nsa(q_r, kvc, kv_r, proj3, _overlap_matrix(s))

    row = lambda v: v.reshape(1, -1)
    h1 = _merge(y_a.reshape(n, d), y_b.reshape(n, d), proj, x2, w_out.astype(BF16), row(ln1_g), row(ln1_b))

    m = mem.shape[1]
    kv = _matmul(mem.reshape(b * m, d).astype(BF16), xa_wkv.astype(BF16), BF16, tm=512, tn=512)
    h2 = _xattn(h1.reshape(b, s, d), kv.reshape(b, m, 2 * d), xa_wq.astype(BF16), xa_wo.astype(BF16),
                row(ln2_g), row(ln2_b))

    wr = jnp.pad(jnp.concatenate([w_group, w_expert], axis=1), ((0, 0), (0, LANE - N_GROUPS - N_EXPERTS)))
    wr_h = wr.astype(BF16)
    wr_l = (wr - wr_h.astype(F32)).astype(BF16)
    br = jnp.pad(jnp.concatenate([b_group, b_expert]), (0, LANE - N_GROUPS - N_EXPERTS)).reshape(1, LANE)
    h3 = _moe(h2.reshape(n, d), wr_h, wr_l, br, w_gate.astype(BF16), w_up.astype(BF16), w_down.astype(BF16),
              row(ln3_g), row(ln3_b))
    return h3.reshape(b, s, d)


def kernel(x, mem, positions, w_in, gdn_conv_w, gdn_a_log, gdn_dt_bias, gdn_norm_w, cmp_pe, cmp_w1, cmp_b1, cmp_w2, w_out, ln1_g, ln1_b, xa_wq, xa_wkv, xa_wo, ln2_g, ln2_b, moe_w_group, moe_b_group, moe_w_expert, moe_b_expert, moe_w_gate, moe_w_up, moe_w_down, ln3_g, ln3_b):
    half = NSA_DH // 2
    inv_freq = ROPE_THETA ** (-jnp.arange(half, dtype=F32) / half)
    ang = positions.astype(F32)[..., None] * inv_freq
    cos, sin = jnp.cos(ang), jnp.sin(ang)
    cos2 = jnp.concatenate([cos, cos], -1)
    sin2 = jnp.concatenate([-sin, sin], -1)
    h = x
    for l in range(DEPTH):
        h = _layer(h, mem, cos2, sin2, w_in[l], gdn_conv_w[l], gdn_a_log[l], gdn_dt_bias[l], gdn_norm_w[l],
                   cmp_pe[l], cmp_w1[l], cmp_b1[l], cmp_w2[l], w_out[l], ln1_g[l], ln1_b[l],
                   xa_wq[l], xa_wkv[l], xa_wo[l], ln2_g[l], ln2_b[l], moe_w_group[l], moe_b_group[l],
                   moe_w_expert[l], moe_b_expert[l], moe_w_gate[l], moe_w_up[l], moe_w_down[l],
                   ln3_g[l], ln3_b[l])
    return h
```

```python
import functools

import jax
import jax.numpy as jnp
from jax import lax
from jax.experimental import pallas as pl
from jax.experimental.pallas import tpu as pltpu

F32 = jnp.float32
BF16 = jnp.bfloat16

D_MODEL = 1024
LANE = 128
GDN_HEADS = 8
GDN_D = 128
GDN_CONV = 4
GDN_CHUNK = 64
NSA_HEADS = 8
NSA_KV_HEADS = 2
NSA_HPG = NSA_HEADS // NSA_KV_HEADS
NSA_DH = 128
CMP_BLOCK = 32
CMP_STRIDE = 16
CMP_HIDDEN = 256
SEL_BLOCK = 64
SEL_SHIFT = 6
SEL_TOPK = 8
WINDOW = 256
XA_HEADS = 4
XA_DH = 256
N_GROUPS = 4
EXPERTS_PER_GROUP = 8
N_EXPERTS = 32
D_FF = 256
DEPTH = 1
DN_ALPHA = (2.0 * DEPTH) ** 0.25
LN_EPS = 1e-5
RMS_EPS = 1e-6
ROPE_THETA = 10000.0
NEG = -1e30
MASK_BIG = 1e30

CB_Q, CB_K, CB_V, CB_Z = 0, 8, 16, 24
CB_MG = 32
CB_NQ = 48
CB_KV = 56
SM_BETA, SM_DECAY, SM_NGATE = 0, 8, 16

VMEM_LIMIT = 48 * 1024 * 1024


def _cp(sem, vmem=VMEM_LIMIT):
    return pltpu.CompilerParams(dimension_semantics=sem, vmem_limit_bytes=vmem)


def _bdot(a, b):
    return jnp.dot(a.astype(BF16), b.astype(BF16), preferred_element_type=F32)


def _bdot_nt(a, b):
    return lax.dot_general(a.astype(BF16), b.astype(BF16), (((1,), (1,)), ((), ())),
                           preferred_element_type=F32)


def _bdot_tn(a, b):
    return lax.dot_general(a.astype(BF16), b.astype(BF16), (((0,), (0,)), ((), ())),
                           preferred_element_type=F32)


def _split3(x):
    h = x.astype(BF16)
    r = x - h.astype(F32)
    m = r.astype(BF16)
    l = (r - m.astype(F32)).astype(BF16)
    return h, m, l


def _sigmoid(x):
    return 1.0 / (1.0 + jnp.exp(-x))


def _silu(x):
    return x * _sigmoid(x)


def _layer_norm(x, g, b):
    mu = jnp.mean(x, -1, keepdims=True)
    xc = x - mu
    var = jnp.mean(xc * xc, -1, keepdims=True)
    return xc * lax.rsqrt(var + LN_EPS) * g + b


def _lane_col(x, c):
    lane = lax.broadcasted_iota(jnp.int32, x.shape, 1)
    return jnp.sum(jnp.where(lane == c, x, 0.0), axis=-1, keepdims=True)


def _mm_kernel(x_ref, w_ref, o_ref):
    o_ref[...] = jnp.dot(x_ref[...], w_ref[...], preferred_element_type=F32).astype(o_ref.dtype)


def _matmul(x, w, out_dtype, tm, tn):
    m, k = x.shape
    n = w.shape[1]
    return pl.pallas_call(
        _mm_kernel,
        grid=(m // tm, n // tn),
        in_specs=[pl.BlockSpec((tm, k), lambda i, j: (i, 0)),
                  pl.BlockSpec((k, tn), lambda i, j: (0, j))],
        out_specs=pl.BlockSpec((tm, tn), lambda i, j: (i, j)),
        out_shape=jax.ShapeDtypeStruct((m, n), out_dtype),
        compiler_params=_cp(("parallel", "parallel")),
        name="matmul",
    )(x, w)


def _inproj_kernel(x_ref, w_ref, ws_ref, o_ref, os_ref, xb_ref):
    @pl.when(pl.program_id(1) == 0)
    def _():
        xb_ref[...] = x_ref[...].astype(BF16)
        os_ref[...] = jnp.dot(xb_ref[...], ws_ref[...], preferred_element_type=F32)

    o_ref[...] = jnp.dot(xb_ref[...], w_ref[...], preferred_element_type=F32).astype(o_ref.dtype)


def _inproj(x2, w_big, w_small, tm=2048, tn=512):
    m, k = x2.shape
    n = w_big.shape[1]
    return pl.pallas_call(
        _inproj_kernel,
        grid=(m // tm, n // tn),
        in_specs=[pl.BlockSpec((tm, k), lambda i, j: (i, 0)),
                  pl.BlockSpec((k, tn), lambda i, j: (0, j)),
                  pl.BlockSpec((k, LANE), lambda i, j: (0, 0))],
        out_specs=[pl.BlockSpec((tm, tn), lambda i, j: (i, j)),
                   pl.BlockSpec((tm, LANE), lambda i, j: (i, 0))],
        out_shape=[jax.ShapeDtypeStruct((m, n), BF16), jax.ShapeDtypeStruct((m, LANE), F32)],
        scratch_shapes=[pltpu.VMEM((tm, k), BF16)],
        compiler_params=_cp(("parallel", "arbitrary")),
        name="inproj",
    )(x2, w_big, w_small)


def _gdn_kernel(q_ref, k_ref, v_ref, z_ref, sm_ref, cwq_ref, cwk_ref, cwv_ref,
                alog_ref, dtb_ref, nw_ref, o_ref, state_ref, tail_ref):
    c_len = q_ref.shape[0]
    width = GDN_HEADS * GDN_D

    @pl.when(pl.program_id(1) == 0)
    def _():
        state_ref[...] = jnp.zeros_like(state_ref)
        tail_ref[...] = jnp.zeros_like(tail_ref)

    row = lax.broadcasted_iota(jnp.int32, (c_len, c_len), 0)
    col = lax.broadcasted_iota(jnp.int32, (c_len, c_len), 1)
    causal = row >= col
    strict = row > col
    eye = jnp.where(row == col, 1.0, 0.0)
    tril_incl = jnp.where(causal, 1.0, 0.0).astype(BF16)

    def conv_silu(idx, x_ref, w_ref):
        cur = x_ref[...].astype(F32)
        xc = jnp.concatenate([tail_ref[idx], cur], axis=0)
        tail_ref[idx] = cur[c_len - 8:, :]
        w = w_ref[...]
        acc = xc[8:, :] * w[GDN_CONV - 1:GDN_CONV, :]
        for j in range(GDN_CONV - 1):
            off = 8 - (GDN_CONV - 1) + j
            acc = acc + xc[off:off + c_len, :] * w[j:j + 1, :]
        return _silu(acc)

    q_all = conv_silu(0, q_ref, cwq_ref)
    k_all = conv_silu(1, k_ref, cwk_ref)
    v_all = conv_silu(2, v_ref, cwv_ref)

    sm = sm_ref[...]
    beta_all = _sigmoid(sm)
    a_in = sm + dtb_ref[...]
    softplus = jnp.maximum(a_in, 0.0) + jnp.log(1.0 + jnp.exp(-jnp.abs(a_in)))
    g_all = -jnp.exp(alog_ref[...]) * softplus
    gh, gm, gl = _split3(g_all)
    gc_all = (jnp.dot(tril_incl, gh, preferred_element_type=F32)
              + jnp.dot(tril_incl, gm, preferred_element_type=F32)
              + jnp.dot(tril_incl, gl, preferred_element_type=F32))
    gc_t = gc_all.T

    heads = range(GDN_HEADS)
    hs = lambda x, h: x[:, h * GDN_D:(h + 1) * GDN_D]
    q = [hs(q_all, h) for h in heads]
    k = [hs(k_all, h) for h in heads]
    v = [hs(v_all, h) for h in heads]
    q = [x * lax.rsqrt(jnp.sum(x * x, -1, keepdims=True) + RMS_EPS) * (GDN_D ** -0.5) for x in q]
    k = [x * lax.rsqrt(jnp.sum(x * x, -1, keepdims=True) + RMS_EPS) for x in k]
    beta = [beta_all[:, SM_BETA + h:SM_BETA + h + 1] for h in heads]
    gc = [gc_all[:, SM_DECAY + h:SM_DECAY + h + 1] for h in heads]
    g_last = [x[c_len - 1:c_len, :] for x in gc]
    diff = [gc[h] - gc_t[SM_DECAY + h:SM_DECAY + h + 1, :] for h in heads]
    decay = [jnp.where(causal, jnp.exp(jnp.where(causal, d, 0.0)), 0.0) for d in diff]
    kk = [_bdot_nt(x, x) for x in k]
    m_pow = [-(jnp.where(strict, kk[h] * decay[h], 0.0) * beta[h]) for h in heads]
    t_inv = [eye + m for m in m_pow]
    for _ in range((c_len - 1).bit_length() - 1):
        m_pow = [_bdot(m, m) for m in m_pow]
        t_inv = [t + _bdot(t, m) for t, m in zip(t_inv, m_pow)]
    e_gc = [jnp.exp(x) for x in gc]
    u = [_bdot(t_inv[h], v[h] * beta[h]) for h in heads]
    w = [_bdot(t_inv[h], k[h] * (beta[h] * e_gc[h])) for h in heads]
    qk = [_bdot_nt(q[h], k[h]) * decay[h] for h in heads]
    q_dec = [q[h] * e_gc[h] for h in heads]
    k_dec = [k[h] * jnp.exp(g_last[h] - gc[h]) for h in heads]
    state = [state_ref[h] for h in heads]
    v_new = [u[h] - _bdot(w[h], state[h]) for h in heads]
    o = [_bdot(q_dec[h], state[h]) + _bdot(qk[h], v_new[h]) for h in heads]
    for h in heads:
        state_ref[h] = state[h] * jnp.exp(g_last[h]) + _bdot_tn(k_dec[h], v_new[h])
    nw = nw_ref[...]
    o = [x * lax.rsqrt(jnp.mean(x * x, -1, keepdims=True) + RMS_EPS) * nw for x in o]
    o_ref[...] = (jnp.concatenate(o, axis=1) * _silu(z_ref[...].astype(F32))).astype(o_ref.dtype)


def _gdn(proj3, small3, conv_w, a_log, dt_bias, norm_w, c_len=128):
    b, s, _ = proj3.shape
    width = GDN_HEADS * GDN_D
    col = lambda off: pl.BlockSpec((None, c_len, width), lambda i, j, off=off: (i, j, off // GDN_HEADS))
    cw = lambda k: pl.BlockSpec((GDN_CONV, width), lambda i, j, k=k: (0, k))
    full = lambda shape: pl.BlockSpec(shape, lambda i, j: (0, 0))
    pad = lambda v: jnp.pad(v, (SM_DECAY, LANE - SM_DECAY - GDN_HEADS)).reshape(1, LANE)
    return pl.pallas_call(
        _gdn_kernel,
        grid=(b, s // c_len),
        in_specs=[col(CB_Q), col(CB_K), col(CB_V), col(CB_Z),
                  pl.BlockSpec((None, c_len, LANE), lambda i, j: (i, j, 0)),
                  cw(0), cw(1), cw(2),
                  full((1, LANE)), full((1, LANE)), full((1, GDN_D))],
        out_specs=pl.BlockSpec((None, c_len, width), lambda i, j: (i, j, 0)),
        out_shape=jax.ShapeDtypeStruct((b, s, width), BF16),
        scratch_shapes=[pltpu.VMEM((GDN_HEADS, GDN_D, GDN_D), F32), pltpu.VMEM((3, 8, width), F32)],
        compiler_params=_cp(("parallel", "arbitrary")),
        name="gdn",
    )(proj3, proj3, proj3, proj3, small3, conv_w, conv_w, conv_w,
      pad(a_log), pad(dt_bias), norm_w.reshape(1, GDN_D))


def _rope_kernel(nq_ref, kv0_ref, kv1_ref, kv2_ref, cos_ref, sin_ref, q_out, ckv_out, kv_out):
    cos2 = cos_ref[...]
    sin2 = sin_ref[...]

    def rope(x):
        x = x.astype(F32)
        return x * cos2 + pltpu.roll(x, NSA_DH // 2, 1) * sin2

    scale = NSA_DH ** -0.5
    q_out[...] = jnp.concatenate(
        [rope(nq_ref[:, hh * LANE:(hh + 1) * LANE]) * scale for hh in range(NSA_HEADS)],
        axis=1).astype(q_out.dtype)
    kv_refs = (kv0_ref, kv1_ref, kv2_ref)
    blk = lambda i: kv_refs[i // 4][:, (i % 4) * LANE:(i % 4 + 1) * LANE]
    ckv_out[0] = rope(blk(0)).astype(ckv_out.dtype)
    ckv_out[1] = rope(blk(1)).astype(ckv_out.dtype)
    ckv_out[2] = blk(2).astype(ckv_out.dtype)
    ckv_out[3] = blk(3).astype(ckv_out.dtype)
    kv_out[...] = jnp.concatenate(
        [rope(blk(4)), rope(blk(5)), blk(6), blk(7), rope(blk(8)), rope(blk(9)), blk(10), blk(11)],
        axis=1).astype(kv_out.dtype)


def _rope(proj3, cos2, sin2, ts=512):
    b, s, _ = proj3.shape
    kvspec = lambda k: pl.BlockSpec((None, ts, 4 * LANE), lambda i, j, k=k: (i, j, CB_KV // 4 + k))
    return pl.pallas_call(
        _rope_kernel,
        grid=(b, s // ts),
        in_specs=[pl.BlockSpec((None, ts, NSA_HEADS * LANE), lambda i, j: (i, j, CB_NQ // NSA_HEADS)),
                  kvspec(0), kvspec(1), kvspec(2),
                  pl.BlockSpec((None, ts, LANE), lambda i, j: (i, j, 0)),
                  pl.BlockSpec((None, ts, LANE), lambda i, j: (i, j, 0))],
        out_specs=[pl.BlockSpec((None, ts, NSA_HEADS * LANE), lambda i, j: (i, j, 0)),
                   pl.BlockSpec((None, 4, ts, LANE), lambda i, j: (i, 0, j, 0)),
                   pl.BlockSpec((None, ts, 8 * LANE), lambda i, j: (i, j, 0))],
        out_shape=[jax.ShapeDtypeStruct((b, s, NSA_HEADS * LANE), BF16),
                   jax.ShapeDtypeStruct((b, 4, s, LANE), BF16),
                   jax.ShapeDtypeStruct((b, s, 8 * LANE), BF16)],
        compiler_params=_cp(("parallel", "parallel")),
        name="rope",
    )(proj3, proj3, proj3, proj3, cos2, sin2)


def _compress_kernel(x_ref, pe_ref, w1_ref, b1_ref, w2_ref, o_ref):
    x = x_ref[...]
    half = CMP_STRIDE * NSA_DH
    first = jnp.dot(x, w1_ref[:half, :], preferred_element_type=F32)
    second = jnp.dot(x, w1_ref[half:, :], preferred_element_type=F32)
    n_rows = x.shape[0]
    second = pltpu.roll(second, n_rows - 1, 0)
    pe = jnp.broadcast_to(pe_ref[...], (8, pe_ref.shape[1]))
    bias = jnp.dot(pe, w1_ref[...], preferred_element_type=F32)[0:1] + b1_ref[...]
    hid = first + second + bias
    gelu = 0.5 * hid * (1.0 + jnp.tanh(0.7978845608028654 * (hid + 0.044715 * hid * hid * hid)))
    o_ref[...] = _bdot(gelu, w2_ref[...]).astype(o_ref.dtype)


def _compress(ckv_r, pe, w1, b1, w2):
    b, _, nr, wd = ckv_r.shape
    return pl.pallas_call(
        _compress_kernel,
        grid=(b, 4),
        in_specs=[pl.BlockSpec((None, None, nr, wd), lambda i, j: (i, j, 0, 0)),
                  pl.BlockSpec((None, 1, CMP_BLOCK * NSA_DH), lambda i, j: (j // 2, 0, 0)),
                  pl.BlockSpec((None, CMP_BLOCK * NSA_DH, CMP_HIDDEN), lambda i, j: (j // 2, 0, 0)),
                  pl.BlockSpec((None, 1, CMP_HIDDEN), lambda i, j: (j // 2, 0, 0)),
                  pl.BlockSpec((None, CMP_HIDDEN, NSA_DH), lambda i, j: (j // 2, 0, 0))],
        out_specs=pl.BlockSpec((None, None, nr, NSA_DH), lambda i, j: (i, j, 0, 0)),
        out_shape=jax.ShapeDtypeStruct((b, 4, nr, NSA_DH), BF16),
        compiler_params=_cp(("parallel", "arbitrary")),
        name="compress",
    )(ckv_r, pe, w1, b1, w2)


def _nsa_kernel(q_ref, kc_ref, vc_ref, ks_ref, vs_ref, kw_ref, vw_ref, eh_ref, sm_ref, ovt_ref, o_ref, *, tq):
    g = pl.program_id(1)
    qi = pl.program_id(2)
    t0 = qi * tq
    hpg = NSA_HPG
    rows4 = hpg * tq
    nb = eh_ref.shape[0] // SEL_BLOCK
    q = q_ref[...]
    qs = jnp.concatenate([q[:, hh * LANE:(hh + 1) * LANE] for hh in range(hpg)], axis=0)
    tile4 = lambda m: jnp.concatenate([m] * hpg, axis=0)

    def qpos(width):
        return t0 + (lax.broadcasted_iota(jnp.int32, (rows4, width), 0) & (tq - 1))

    lane4 = lax.broadcasted_iota(jnp.int32, (rows4, LANE), 1)
    cmask = lane4 * CMP_STRIDE + (CMP_BLOCK - 1) <= qpos(LANE)
    s_c = jnp.where(cmask, _bdot_nt(qs, kc_ref[...]), NEG)
    e_c = jnp.where(cmask, jnp.exp(s_c - jnp.max(s_c, -1, keepdims=True)), 0.0)
    p_c = e_c * (1.0 / jnp.maximum(jnp.sum(e_c, -1, keepdims=True), 1e-30))
    o_c = _bdot(p_c, vc_ref[...])

    p_sum = p_c[0:tq]
    for hh in range(1, hpg):
        p_sum = p_sum + p_c[hh * tq:(hh + 1) * tq]
    ovt = ovt_ref[...]
    imp_t = sum(lax.dot_general(ovt, part, (((1,), (1,)), ((), ())), preferred_element_type=F32)
                for part in _split3(p_sum))[:nb]
    blk = lax.broadcasted_iota(jnp.int32, (nb, tq), 0)
    t = t0 + lax.broadcasted_iota(jnp.int32, (nb, tq), 1)
    cur = t >> SEL_SHIFT
    forced = (blk == 0) | (blk == cur) | (blk == cur - 1)
    val = jnp.where(forced, jnp.inf, jnp.where(blk * SEL_BLOCK <= t, imp_t, -jnp.inf))
    rank = jnp.zeros((nb, tq), F32)
    for i in range(nb):
        vi = val[i:i + 1, :]
        rank = rank + jnp.where((vi > val) | ((vi == val) & (blk > i)), 1.0, 0.0)
    drop_t = jnp.where(rank < SEL_TOPK, 0.0, MASK_BIG)
    drop = jnp.concatenate([drop_t, jnp.zeros((LANE - nb, tq), F32)], axis=0).T
    q_aug = jnp.concatenate([qs, tile4((-drop).astype(BF16))], axis=1)

    ones_col = jnp.where(lax.broadcasted_iota(jnp.int32, (tq, LANE), 1) == 0, 1.0, 0.0).astype(BF16)
    kpos_d = lax.broadcasted_iota(jnp.int32, (rows4, tq), 1)
    row_d = lax.broadcasted_iota(jnp.int32, (rows4, tq), 0) & (tq - 1)

    prev0 = pl.multiple_of(jnp.maximum(qi - 1, 0) * tq, tq)
    own0 = pl.multiple_of(t0, tq)
    prev_lo = row_d + jnp.where(qi > 0, 0, tq)
    s_w = jnp.concatenate([jnp.where(kpos_d > prev_lo, _bdot_nt(qs, kw_ref[pl.ds(prev0, tq), :]), NEG),
                           jnp.where(kpos_d <= row_d, _bdot_nt(qs, kw_ref[pl.ds(own0, tq), :]), NEG)], axis=1)
    e_w = jnp.exp(s_w - jnp.max(s_w, -1, keepdims=True))
    v_w = jnp.concatenate([jnp.concatenate([vw_ref[pl.ds(prev0, tq), :], ones_col], axis=1),
                           jnp.concatenate([vw_ref[pl.ds(own0, tq), :], ones_col], axis=1)], axis=0)
    acc_w = _bdot(e_w, v_w)
    o_w = acc_w[:, :NSA_DH] * (1.0 / jnp.maximum(acc_w[:, NSA_DH:NSA_DH + 1], 1e-30))


    def sel_chunk(c, carry, diagonal):
        m_i, acc = carry
        k0 = pl.multiple_of(c * tq, tq)
        k_aug = jnp.concatenate([ks_ref[pl.ds(k0, tq), :], eh_ref[pl.ds(k0, tq), :]], axis=1)
        v_aug = jnp.concatenate([vs_ref[pl.ds(k0, tq), :], ones_col], axis=1)
        s = lax.dot_general(q_aug, k_aug, (((1,), (1,)), ((), ())), preferred_element_type=F32)
        if diagonal:
            s = jnp.where(kpos_d <= row_d, s, -MASK_BIG)
        m_new = jnp.maximum(m_i, jnp.max(s, -1, keepdims=True))
        p = jnp.exp(s - m_new)
        acc = jnp.exp(m_i - m_new) * acc + _bdot(p, v_aug)
        return m_new, acc

    carry = lax.fori_loop(0, qi, functools.partial(sel_chunk, diagonal=False),
                          (jnp.full((rows4, 1), -MASK_BIG, F32), jnp.zeros((rows4, 2 * NSA_DH), F32)))
    _, acc_s = sel_chunk(qi, carry, True)
    o_s = acc_s[:, :NSA_DH] * (1.0 / jnp.maximum(acc_s[:, NSA_DH:NSA_DH + 1], 1e-30))

    gates = _sigmoid(sm_ref[...])
    outs = []
    for hh in range(hpg):
        base = SM_NGATE + (g * hpg + hh) * 3
        r = slice(hh * tq, (hh + 1) * tq)
        outs.append(_lane_col(gates, base) * o_c[r] + _lane_col(gates, base + 1) * o_s[r]
                    + _lane_col(gates, base + 2) * o_w[r])
    o_ref[...] = jnp.concatenate(outs, axis=1).astype(o_ref.dtype)


def _nsa(q_r, kvc, kv_r, small3):
    b, s, _ = q_r.shape
    nr = kvc.shape[2]
    nb = s // SEL_BLOCK
    tq = WINDOW
    assert s % tq == 0 and nb <= LANE and nr <= LANE
    gw = NSA_HPG * LANE
    eh = (jnp.arange(s)[:, None] // SEL_BLOCK == jnp.arange(LANE)[None, :]).astype(BF16)
    kvspec = lambda off: pl.BlockSpec((None, s, LANE), lambda i, g, j, off=off: (i, 0, off + g))
    cspec = lambda off: pl.BlockSpec((None, None, nr, NSA_DH), lambda i, g, j, off=off: (i, off + g, 0, 0))
    return pl.pallas_call(
        functools.partial(_nsa_kernel, tq=tq),
        grid=(b, NSA_KV_HEADS, s // tq),
        in_specs=[pl.BlockSpec((None, tq, gw), lambda i, g, j: (i, j, g)),
                  cspec(0), cspec(2),
                  kvspec(0), kvspec(2), kvspec(4), kvspec(6),
                  pl.BlockSpec((s, LANE), lambda i, g, j: (0, 0)),
                  pl.BlockSpec((None, tq, LANE), lambda i, g, j: (i, j, 0)),
                  pl.BlockSpec((LANE, LANE), lambda i, g, j: (0, 0))],
        out_specs=pl.BlockSpec((None, tq, gw), lambda i, g, j: (i, j, g)),
        out_shape=jax.ShapeDtypeStruct((b, s, NSA_HEADS * NSA_DH), BF16),
        compiler_params=_cp(("parallel", "parallel", "arbitrary")),
        name="nsa",
    )(q_r, kvc, kvc, kv_r, kv_r, kv_r, kv_r, eh, small3, _overlap_matrix(s).T)


def _merge_kernel(ya_ref, yb_ref, mg_ref, x_ref, w_ref, g_ref, b_ref, o_ref):
    mg = mg_ref[...].astype(F32)
    merged = (_sigmoid(mg[:, :D_MODEL]) * ya_ref[...].astype(F32)
              + _sigmoid(mg[:, D_MODEL:]) * yb_ref[...].astype(F32))
    mix = _bdot(merged, w_ref[...])
    o_ref[...] = _layer_norm(DN_ALPHA * x_ref[...] + mix, g_ref[...], b_ref[...])


def _merge(y_a, y_b, proj, x2, w_out, g, b, tm=512):
    n = x2.shape[0]
    row = lambda w: pl.BlockSpec((tm, w), lambda i: (i, 0))
    full = lambda shape: pl.BlockSpec(shape, lambda i: (0, 0))
    return pl.pallas_call(
        _merge_kernel,
        grid=(n // tm,),
        in_specs=[row(D_MODEL), row(D_MODEL),
                  pl.BlockSpec((tm, 2 * D_MODEL), lambda i: (i, CB_MG * LANE // (2 * D_MODEL))),
                  row(D_MODEL), full((D_MODEL, D_MODEL)), full((1, D_MODEL)), full((1, D_MODEL))],
        out_specs=row(D_MODEL),
        out_shape=jax.ShapeDtypeStruct((n, D_MODEL), F32),
        compiler_params=_cp(("parallel",)),
        name="merge_ln1",
    )(y_a, y_b, proj, x2, w_out, g, b)


def _xattn_kernel(h_ref, kv_ref, wq_ref, wo_ref, g_ref, b_ref, o_ref):
    h = h_ref[...]
    q = _bdot(h, wq_ref[...]).astype(BF16)
    kv = kv_ref[...]
    outs = []
    for hh in range(XA_HEADS):
        sl = slice(hh * XA_DH, (hh + 1) * XA_DH)
        s = _bdot_nt(q[:, sl], kv[:, sl]) * (XA_DH ** -0.5)
        m = jnp.max(s, -1, keepdims=True)
        e = jnp.exp(s - m)
        p = e / jnp.sum(e, -1, keepdims=True)
        outs.append(_bdot(p, kv[:, D_MODEL + hh * XA_DH:D_MODEL + (hh + 1) * XA_DH]))
    o = jnp.concatenate(outs, axis=1)
    o_ref[...] = _layer_norm(DN_ALPHA * h + _bdot(o, wo_ref[...]), g_ref[...], b_ref[...])


def _xattn(h3, kv3, wq, wo, g, b, tm=512):
    bsz, s, _ = h3.shape
    m = kv3.shape[1]
    full = lambda shape: pl.BlockSpec(shape, lambda i, j: (0, 0))
    return pl.pallas_call(
        _xattn_kernel,
        grid=(bsz, s // tm),
        in_specs=[pl.BlockSpec((None, tm, D_MODEL), lambda i, j: (i, j, 0)),
                  pl.BlockSpec((None, m, 2 * D_MODEL), lambda i, j: (i, 0, 0)),
                  full((D_MODEL, D_MODEL)), full((D_MODEL, D_MODEL)), full((1, D_MODEL)), full((1, D_MODEL))],
        out_specs=pl.BlockSpec((None, tm, D_MODEL), lambda i, j: (i, j, 0)),
        out_shape=jax.ShapeDtypeStruct((bsz, s, D_MODEL), F32),
        compiler_params=_cp(("parallel", "parallel")),
        name="xattn_ln2",
    )(h3, kv3, wq, wo, g, b)


def _route(x, wr_h, wr_l, br):
    xh = x.astype(BF16)
    xl = (x - xh.astype(F32)).astype(BF16)
    logits = (jnp.dot(xh, wr_h, preferred_element_type=F32) + jnp.dot(xh, wr_l, preferred_element_type=F32)
              + jnp.dot(xl, wr_h, preferred_element_type=F32)) + br
    lane = lax.broadcasted_iota(jnp.int32, logits.shape, 1)
    gmask = lane < N_GROUPS
    gl = jnp.where(gmask, logits, -jnp.inf)
    gmax = jnp.max(gl, -1, keepdims=True)
    g_idx = jnp.min(jnp.where(gl == gmax, lane, LANE), -1, keepdims=True)
    p_top = 1.0 / jnp.sum(jnp.where(gmask, jnp.exp(gl - gmax), 0.0), -1, keepdims=True)
    e_lane = lane - N_GROUPS
    emask = (e_lane >= 0) & (e_lane < N_EXPERTS) & ((e_lane >> 3) == g_idx)
    el = jnp.where(emask, logits, -jnp.inf)
    emax = jnp.max(el, -1, keepdims=True)
    ee = jnp.where(emask, jnp.exp(el - emax), 0.0)
    pe = ee / jnp.sum(ee, -1, keepdims=True)
    pv = jnp.where(emask, pe, -jnp.inf)
    m1 = jnp.max(pv, -1, keepdims=True)
    i1 = jnp.min(jnp.where(pv == m1, lane, LANE), -1, keepdims=True)
    pv2 = jnp.where(lane == i1, -jnp.inf, pv)
    m2 = jnp.max(pv2, -1, keepdims=True)
    i2 = jnp.min(jnp.where(pv2 == m2, lane, LANE), -1, keepdims=True)
    denom = m1 + m2
    return jnp.where(lane == i1, p_top * m1 / denom, 0.0) + jnp.where(lane == i2, p_top * m2 / denom, 0.0)


def _moe_kernel(h_ref, wrh_ref, wrl_ref, br_ref, wg_ref, wu_ref, wd_ref, g_ref, b_ref, o_ref,
                xb_ref, comb_ref, acc_ref):
    e = pl.program_id(1)

    @pl.when(e == 0)
    def _():
        x = h_ref[...]
        xb_ref[...] = x.astype(BF16)
        comb_ref[...] = _route(x, wrh_ref[...], wrl_ref[...], br_ref[...])
        acc_ref[...] = jnp.zeros_like(acc_ref)

    xb = xb_ref[...]
    cw = _lane_col(comb_ref[...], e + N_GROUPS)
    gate = jnp.dot(xb, wg_ref[...], preferred_element_type=F32)
    up = jnp.dot(xb, wu_ref[...], preferred_element_type=F32)
    acc_ref[...] += _bdot(_silu(gate) * up * cw, wd_ref[...])

    @pl.when(e == N_EXPERTS - 1)
    def _():
        o_ref[...] = _layer_norm(DN_ALPHA * h_ref[...] + acc_ref[...], g_ref[...], b_ref[...])


def _moe(h2, wr_h, wr_l, br, wg, wu, wd, g, b, tm=1024):
    n = h2.shape[0]
    full = lambda shape: pl.BlockSpec(shape, lambda i, e: (0, 0))
    return pl.pallas_call(
        _moe_kernel,
        grid=(n // tm, N_EXPERTS),
        in_specs=[pl.BlockSpec((tm, D_MODEL), lambda i, e: (i, 0)),
                  full((D_MODEL, LANE)), full((D_MODEL, LANE)), full((1, LANE)),
                  pl.BlockSpec((None, D_MODEL, D_FF), lambda i, e: (e, 0, 0)),
                  pl.BlockSpec((None, D_MODEL, D_FF), lambda i, e: (e, 0, 0)),
                  pl.BlockSpec((None, D_FF, D_MODEL), lambda i, e: (e, 0, 0)),
                  full((1, D_MODEL)), full((1, D_MODEL))],
        out_specs=pl.BlockSpec((tm, D_MODEL), lambda i, e: (i, 0)),
        out_shape=jax.ShapeDtypeStruct((n, D_MODEL), F32),
        scratch_shapes=[pltpu.VMEM((tm, D_MODEL), BF16), pltpu.VMEM((tm, LANE), F32),
                        pltpu.VMEM((tm, D_MODEL), F32)],
        compiler_params=_cp(("parallel", "arbitrary")),
        name="moe_ln3",
    )(h2, wr_h, wr_l, br, wg, wu, wd, g, b)


def _regroup_w_in(w):
    sizes = (1024, 1024, 1024, 1024, 8, 8, 1024, 256, 256, 256, 256, 256, 256, 24, 2048)
    offs = [0]
    for sz in sizes:
        offs.append(offs[-1] + sz)
    seg = lambda i: w[:, offs[i]:offs[i + 1]]
    small = jnp.concatenate([seg(4), seg(5), seg(13)], axis=1)
    small = jnp.pad(small, ((0, 0), (0, LANE - small.shape[1])))
    big = jnp.concatenate([seg(0), seg(1), seg(2), seg(3), seg(14), seg(6), seg(7), seg(8), seg(9), seg(10),
                           seg(11), seg(12)], axis=1)
    return big.astype(BF16), small.astype(BF16)


def _overlap_matrix(s):
    nb = s // SEL_BLOCK
    c0 = jnp.arange(LANE) * CMP_STRIDE
    s0 = jnp.arange(LANE) * SEL_BLOCK
    ov = jnp.minimum(c0[:, None] + CMP_BLOCK, s0[None, :] + SEL_BLOCK) - jnp.maximum(c0[:, None], s0[None, :])
    ov = jnp.maximum(ov, 0).astype(F32) / CMP_BLOCK
    nc = (s - CMP_BLOCK) // CMP_STRIDE + 1
    keep = (jnp.arange(LANE)[:, None] < nc) & (jnp.arange(LANE)[None, :] < nb)
    return jnp.where(keep, ov, 0.0).astype(BF16)


def _layer(h, mem, cos2, sin2, w_in, conv_w, a_log, dt_bias, norm_w, cmp_pe, cmp_w1, cmp_b1, cmp_w2, w_out,
           ln1_g, ln1_b, xa_wq, xa_wkv, xa_wo, ln2_g, ln2_b, w_group, b_group, w_expert, b_expert,
           w_gate, w_up, w_down, ln3_g, ln3_b):
    b, s, d = h.shape
    n = b * s
    x2 = h.reshape(n, d)
    proj, small = _inproj(x2, *_regroup_w_in(w_in))
    proj3 = proj.reshape(b, s, -1)
    small3 = small.reshape(b, s, LANE)

    y_a = _gdn(proj3, small3, conv_w, a_log, dt_bias, norm_w)

    q_r, ckv_r, kv_r = _rope(proj3, cos2, sin2)
    ckv_r = ckv_r.reshape(b, 4, s // CMP_STRIDE, CMP_STRIDE * NSA_DH)
    kvc = _compress(ckv_r, cmp_pe.reshape(2, 1, CMP_BLOCK * NSA_DH).astype(BF16), cmp_w1.astype(BF16),
                    cmp_b1.reshape(2, 1, CMP_HIDDEN), cmp_w2.astype(BF16))
    y_b = _nsa(q_r, kvc, kv_r, small3)

    row = lambda v: v.reshape(1, -1)
    h1 = _merge(y_a.reshape(n, d), y_b.reshape(n, d), proj, x2, w_out.astype(BF16), row(ln1_g), row(ln1_b))

    m = mem.shape[1]
    kv = _matmul(mem.reshape(b * m, d).astype(BF16), xa_wkv.astype(BF16), BF16, tm=512, tn=512)
    h2 = _xattn(h1.reshape(b, s, d), kv.reshape(b, m, 2 * d), xa_wq.astype(BF16), xa_wo.astype(BF16),
                row(ln2_g), row(ln2_b))

    wr = jnp.pad(jnp.concatenate([w_group, w_expert], axis=1), ((0, 0), (0, LANE - N_GROUPS - N_EXPERTS)))
    wr_h = wr.astype(BF16)
    wr_l = (wr - wr_h.astype(F32)).astype(BF16)
    br = jnp.pad(jnp.concatenate([b_group, b_expert]), (0, LANE - N_GROUPS - N_EXPERTS)).reshape(1, LANE)
    h3 = _moe(h2.reshape(n, d), wr_h, wr_l, br, w_gate.astype(BF16), w_up.astype(BF16), w_down.astype(BF16),
              row(ln3_g), row(ln3_b))
    return h3.reshape(b, s, d)


def kernel(x, mem, positions, w_in, gdn_conv_w, gdn_a_log, gdn_dt_bias, gdn_norm_w, cmp_pe, cmp_w1, cmp_b1, cmp_w2, w_out, ln1_g, ln1_b, xa_wq, xa_wkv, xa_wo, ln2_g, ln2_b, moe_w_group, moe_b_group, moe_w_expert, moe_b_expert, moe_w_gate, moe_w_up, moe_w_down, ln3_g, ln3_b):
    half = NSA_DH // 2
    inv_freq = ROPE_THETA ** (-jnp.arange(half, dtype=F32) / half)
    ang = positions.astype(F32)[..., None] * inv_freq
    cos, sin = jnp.cos(ang), jnp.sin(ang)
    cos2 = jnp.concatenate([cos, cos], -1)
    sin2 = jnp.concatenate([-sin, sin], -1)
    h = x
    for l in range(DEPTH):
        h = _layer(h, mem, cos2, sin2, w_in[l], gdn_conv_w[l], gdn_a_log[l], gdn_dt_bias[l], gdn_norm_w[l],
                   cmp_pe[l], cmp_w1[l], cmp_b1[l], cmp_w2[l], w_out[l], ln1_g[l], ln1_b[l],
                   xa_wq[l], xa_wkv[l], xa_wo[l], ln2_g[l], ln2_b[l], moe_w_group[l], moe_b_group[l],
                   moe_w_expert[l], moe_b_expert[l], moe_w_gate[l], moe_w_up[l], moe_w_down[l],
                   ln3_g[l], ln3_b[l])
    return h
```

```python
import functools

import jax
import jax.numpy as jnp
from jax import lax
from jax.experimental import pallas as pl
from jax.experimental.pallas import tpu as pltpu

F32 = jnp.float32
BF16 = jnp.bfloat16

D_MODEL = 1024
LANE = 128
GDN_HEADS = 8
GDN_D = 128
GDN_CONV = 4
GDN_CHUNK = 64
NSA_HEADS = 8
NSA_KV_HEADS = 2
NSA_HPG = NSA_HEADS // NSA_KV_HEADS
NSA_DH = 128
CMP_BLOCK = 32
CMP_STRIDE = 16
CMP_HIDDEN = 256
SEL_BLOCK = 64
SEL_SHIFT = 6
SEL_TOPK = 8
WINDOW = 256
XA_HEADS = 4
XA_DH = 256
N_GROUPS = 4
EXPERTS_PER_GROUP = 8
N_EXPERTS = 32
D_FF = 256
DEPTH = 1
DN_ALPHA = (2.0 * DEPTH) ** 0.25
LN_EPS = 1e-5
RMS_EPS = 1e-6
ROPE_THETA = 10000.0
NEG = -1e30
MASK_BIG = 1e30

CB_Q, CB_K, CB_V, CB_Z = 0, 8, 16, 24
CB_MG = 32
CB_NQ = 48
CB_KV = 56
SM_BETA, SM_DECAY, SM_NGATE = 0, 8, 16

VMEM_LIMIT = 48 * 1024 * 1024


def _cp(sem, vmem=VMEM_LIMIT):
    return pltpu.CompilerParams(dimension_semantics=sem, vmem_limit_bytes=vmem)


def _bdot(a, b):
    return jnp.dot(a.astype(BF16), b.astype(BF16), preferred_element_type=F32)


def _bdot_nt(a, b):
    return lax.dot_general(a.astype(BF16), b.astype(BF16), (((1,), (1,)), ((), ())),
                           preferred_element_type=F32)


def _bdot_tn(a, b):
    return lax.dot_general(a.astype(BF16), b.astype(BF16), (((0,), (0,)), ((), ())),
                           preferred_element_type=F32)


def _split3(x):
    h = x.astype(BF16)
    r = x - h.astype(F32)
    m = r.astype(BF16)
    l = (r - m.astype(F32)).astype(BF16)
    return h, m, l


def _sigmoid(x):
    return 1.0 / (1.0 + jnp.exp(-x))


def _silu(x):
    return x * _sigmoid(x)


def _layer_norm(x, g, b):
    mu = jnp.mean(x, -1, keepdims=True)
    xc = x - mu
    var = jnp.mean(xc * xc, -1, keepdims=True)
    return xc * lax.rsqrt(var + LN_EPS) * g + b


def _lane_col(x, c):
    lane = lax.broadcasted_iota(jnp.int32, x.shape, 1)
    return jnp.sum(jnp.where(lane == c, x, 0.0), axis=-1, keepdims=True)


def _mm_kernel(x_ref, w_ref, o_ref):
    o_ref[...] = jnp.dot(x_ref[...], w_ref[...], preferred_element_type=F32).astype(o_ref.dtype)


def _matmul(x, w, out_dtype, tm, tn):
    m, k = x.shape
    n = w.shape[1]
    return pl.pallas_call(
        _mm_kernel,
        grid=(m // tm, n // tn),
        in_specs=[pl.BlockSpec((tm, k), lambda i, j: (i, 0)),
                  pl.BlockSpec((k, tn), lambda i, j: (0, j))],
        out_specs=pl.BlockSpec((tm, tn), lambda i, j: (i, j)),
        out_shape=jax.ShapeDtypeStruct((m, n), out_dtype),
        compiler_params=_cp(("parallel", "parallel")),
        name="matmul",
    )(x, w)


def _inproj_kernel(x_ref, w_ref, ws_ref, o_ref, os_ref, xb_ref):
    @pl.when(pl.program_id(1) == 0)
    def _():
        xb_ref[...] = x_ref[...].astype(BF16)
        os_ref[...] = jnp.dot(xb_ref[...], ws_ref[...], preferred_element_type=F32)

    o_ref[...] = jnp.dot(xb_ref[...], w_ref[...], preferred_element_type=F32).astype(o_ref.dtype)


def _inproj(x2, w_big, w_small, tm=2048, tn=512):
    m, k = x2.shape
    n = w_big.shape[1]
    return pl.pallas_call(
        _inproj_kernel,
        grid=(m // tm, n // tn),
        in_specs=[pl.BlockSpec((tm, k), lambda i, j: (i, 0)),
                  pl.BlockSpec((k, tn), lambda i, j: (0, j)),
                  pl.BlockSpec((k, LANE), lambda i, j: (0, 0))],
        out_specs=[pl.BlockSpec((tm, tn), lambda i, j: (i, j)),
                   pl.BlockSpec((tm, LANE), lambda i, j: (i, 0))],
        out_shape=[jax.ShapeDtypeStruct((m, n), BF16), jax.ShapeDtypeStruct((m, LANE), F32)],
        scratch_shapes=[pltpu.VMEM((tm, k), BF16)],
        compiler_params=_cp(("parallel", "arbitrary")),
        name="inproj",
    )(x2, w_big, w_small)


def _gdn_kernel(q_ref, k_ref, v_ref, z_ref, sm_ref, cwq_ref, cwk_ref, cwv_ref,
                alog_ref, dtb_ref, nw_ref, o_ref, state_ref, tail_ref):
    c_len = q_ref.shape[0]
    width = GDN_HEADS * GDN_D

    @pl.when(pl.program_id(1) == 0)
    def _():
        state_ref[...] = jnp.zeros_like(state_ref)
        tail_ref[...] = jnp.zeros_like(tail_ref)

    row = lax.broadcasted_iota(jnp.int32, (c_len, c_len), 0)
    col = lax.broadcasted_iota(jnp.int32, (c_len, c_len), 1)
    causal = row >= col
    strict = row > col
    eye = jnp.where(row == col, 1.0, 0.0)
    tril_incl = jnp.where(causal, 1.0, 0.0).astype(BF16)

    def conv_silu(idx, x_ref, w_ref):
        cur = x_ref[...].astype(F32)
        xc = jnp.concatenate([tail_ref[idx], cur], axis=0)
        tail_ref[idx] = cur[c_len - 8:, :]
        w = w_ref[...]
        acc = xc[8:, :] * w[GDN_CONV - 1:GDN_CONV, :]
        for j in range(GDN_CONV - 1):
            off = 8 - (GDN_CONV - 1) + j
            acc = acc + xc[off:off + c_len, :] * w[j:j + 1, :]
        return _silu(acc)

    q_all = conv_silu(0, q_ref, cwq_ref)
    k_all = conv_silu(1, k_ref, cwk_ref)
    v_all = conv_silu(2, v_ref, cwv_ref)

    sm = sm_ref[...]
    beta_all = _sigmoid(sm)
    a_in = sm + dtb_ref[...]
    softplus = jnp.maximum(a_in, 0.0) + jnp.log(1.0 + jnp.exp(-jnp.abs(a_in)))
    g_all = -jnp.exp(alog_ref[...]) * softplus
    gh, gm, gl = _split3(g_all)
    gc_all = (jnp.dot(tril_incl, gh, preferred_element_type=F32)
              + jnp.dot(tril_incl, gm, preferred_element_type=F32)
              + jnp.dot(tril_incl, gl, preferred_element_type=F32))
    gc_t = gc_all.T

    heads = range(GDN_HEADS)
    hs = lambda x, h: x[:, h * GDN_D:(h + 1) * GDN_D]
    q = [hs(q_all, h) for h in heads]
    k = [hs(k_all, h) for h in heads]
    v = [hs(v_all, h) for h in heads]
    q = [x * lax.rsqrt(jnp.sum(x * x, -1, keepdims=True) + RMS_EPS) * (GDN_D ** -0.5) for x in q]
    k = [x * lax.rsqrt(jnp.sum(x * x, -1, keepdims=True) + RMS_EPS) for x in k]
    beta = [beta_all[:, SM_BETA + h:SM_BETA + h + 1] for h in heads]
    gc = [gc_all[:, SM_DECAY + h:SM_DECAY + h + 1] for h in heads]
    g_last = [x[c_len - 1:c_len, :] for x in gc]
    diff = [gc[h] - gc_t[SM_DECAY + h:SM_DECAY + h + 1, :] for h in heads]
    decay = [jnp.where(causal, jnp.exp(jnp.where(causal, d, 0.0)), 0.0) for d in diff]
    kk = [_bdot_nt(x, x) for x in k]
    m_pow = [-(jnp.where(strict, kk[h] * decay[h], 0.0) * beta[h]) for h in heads]
    t_inv = [eye + m for m in m_pow]
    for _ in range((c_len - 1).bit_length() - 1):
        m_pow = [_bdot(m, m) for m in m_pow]
        t_inv = [t + _bdot(t, m) for t, m in zip(t_inv, m_pow)]
    e_gc = [jnp.exp(x) for x in gc]
    u = [_bdot(t_inv[h], v[h] * beta[h]) for h in heads]
    w = [_bdot(t_inv[h], k[h] * (beta[h] * e_gc[h])) for h in heads]
    qk = [_bdot_nt(q[h], k[h]) * decay[h] for h in heads]
    q_dec = [q[h] * e_gc[h] for h in heads]
    k_dec = [k[h] * jnp.exp(g_last[h] - gc[h]) for h in heads]
    state = [state_ref[h] for h in heads]
    v_new = [u[h] - _bdot(w[h], state[h]) for h in heads]
    o = [_bdot(q_dec[h], state[h]) + _bdot(qk[h], v_new[h]) for h in heads]
    for h in heads:
        state_ref[h] = state[h] * jnp.exp(g_last[h]) + _bdot_tn(k_dec[h], v_new[h])
    nw = nw_ref[...]
    o = [x * lax.rsqrt(jnp.mean(x * x, -1, keepdims=True) + RMS_EPS) * nw for x in o]
    o_ref[...] = (jnp.concatenate(o, axis=1) * _silu(z_ref[...].astype(F32))).astype(o_ref.dtype)


def _gdn(proj3, small3, conv_w, a_log, dt_bias, norm_w, c_len=128):
    b, s, _ = proj3.shape
    width = GDN_HEADS * GDN_D
    col = lambda off: pl.BlockSpec((None, c_len, width), lambda i, j, off=off: (i, j, off // GDN_HEADS))
    cw = lambda k: pl.BlockSpec((GDN_CONV, width), lambda i, j, k=k: (0, k))
    full = lambda shape: pl.BlockSpec(shape, lambda i, j: (0, 0))
    pad = lambda v: jnp.pad(v, (SM_DECAY, LANE - SM_DECAY - GDN_HEADS)).reshape(1, LANE)
    return pl.pallas_call(
        _gdn_kernel,
        grid=(b, s // c_len),
        in_specs=[col(CB_Q), col(CB_K), col(CB_V), col(CB_Z),
                  pl.BlockSpec((None, c_len, LANE), lambda i, j: (i, j, 0)),
                  cw(0), cw(1), cw(2),
                  full((1, LANE)), full((1, LANE)), full((1, GDN_D))],
        out_specs=pl.BlockSpec((None, c_len, width), lambda i, j: (i, j, 0)),
        out_shape=jax.ShapeDtypeStruct((b, s, width), BF16),
        scratch_shapes=[pltpu.VMEM((GDN_HEADS, GDN_D, GDN_D), F32), pltpu.VMEM((3, 8, width), F32)],
        compiler_params=_cp(("parallel", "arbitrary")),
        name="gdn",
    )(proj3, proj3, proj3, proj3, small3, conv_w, conv_w, conv_w,
      pad(a_log), pad(dt_bias), norm_w.reshape(1, GDN_D))


def _rope_kernel(nq_ref, kv0_ref, kv1_ref, kv2_ref, cos_ref, sin_ref, q_out, ckv_out, kv_out):
    cos2 = cos_ref[...]
    sin2 = sin_ref[...]

    def rope(x):
        x = x.astype(F32)
        return x * cos2 + pltpu.roll(x, NSA_DH // 2, 1) * sin2

    scale = NSA_DH ** -0.5
    q_out[...] = jnp.concatenate(
        [rope(nq_ref[:, hh * LANE:(hh + 1) * LANE]) * scale for hh in range(NSA_HEADS)],
        axis=1).astype(q_out.dtype)
    kv_refs = (kv0_ref, kv1_ref, kv2_ref)
    blk = lambda i: kv_refs[i // 4][:, (i % 4) * LANE:(i % 4 + 1) * LANE]
    ckv_out[0] = rope(blk(0)).astype(ckv_out.dtype)
    ckv_out[1] = rope(blk(1)).astype(ckv_out.dtype)
    ckv_out[2] = blk(2).astype(ckv_out.dtype)
    ckv_out[3] = blk(3).astype(ckv_out.dtype)
    kv_out[...] = jnp.concatenate(
        [rope(blk(4)), rope(blk(5)), blk(6), blk(7), rope(blk(8)), rope(blk(9)), blk(10), blk(11)],
        axis=1).astype(kv_out.dtype)


def _rope(proj3, cos2, sin2, ts=512):
    b, s, _ = proj3.shape
    kvspec = lambda k: pl.BlockSpec((None, ts, 4 * LANE), lambda i, j, k=k: (i, j, CB_KV // 4 + k))
    return pl.pallas_call(
        _rope_kernel,
        grid=(b, s // ts),
        in_specs=[pl.BlockSpec((None, ts, NSA_HEADS * LANE), lambda i, j: (i, j, CB_NQ // NSA_HEADS)),
                  kvspec(0), kvspec(1), kvspec(2),
                  pl.BlockSpec((None, ts, LANE), lambda i, j: (i, j, 0)),
                  pl.BlockSpec((None, ts, LANE), lambda i, j: (i, j, 0))],
        out_specs=[pl.BlockSpec((None, ts, NSA_HEADS * LANE), lambda i, j: (i, j, 0)),
                   pl.BlockSpec((None, 4, ts, LANE), lambda i, j: (i, 0, j, 0)),
                   pl.BlockSpec((None, ts, 8 * LANE), lambda i, j: (i, j, 0))],
        out_shape=[jax.ShapeDtypeStruct((b, s, NSA_HEADS * LANE), BF16),
                   jax.ShapeDtypeStruct((b, 4, s, LANE), BF16),
                   jax.ShapeDtypeStruct((b, s, 8 * LANE), BF16)],
        compiler_params=_cp(("parallel", "parallel")),
        name="rope",
    )(proj3, proj3, proj3, proj3, cos2, sin2)


def _compress_kernel(x_ref, pe_ref, w1_ref, b1_ref, w2_ref, o_ref):
    x = x_ref[...]
    half = CMP_STRIDE * NSA_DH
    first = jnp.dot(x, w1_ref[:half, :], preferred_element_type=F32)
    second = jnp.dot(x, w1_ref[half:, :], preferred_element_type=F32)
    n_rows = x.shape[0]
    second = pltpu.roll(second, n_rows - 1, 0)
    pe = jnp.broadcast_to(pe_ref[...], (8, pe_ref.shape[1]))
    bias = jnp.dot(pe, w1_ref[...], preferred_element_type=F32)[0:1] + b1_ref[...]
    hid = first + second + bias
    gelu = 0.5 * hid * (1.0 + jnp.tanh(0.7978845608028654 * (hid + 0.044715 * hid * hid * hid)))
    o_ref[...] = _bdot(gelu, w2_ref[...]).astype(o_ref.dtype)


def _compress(ckv_r, pe, w1, b1, w2):
    b, _, nr, wd = ckv_r.shape
    return pl.pallas_call(
        _compress_kernel,
        grid=(b, 4),
        in_specs=[pl.BlockSpec((None, None, nr, wd), lambda i, j: (i, j, 0, 0)),
                  pl.BlockSpec((None, 1, CMP_BLOCK * NSA_DH), lambda i, j: (j // 2, 0, 0)),
                  pl.BlockSpec((None, CMP_BLOCK * NSA_DH, CMP_HIDDEN), lambda i, j: (j // 2, 0, 0)),
                  pl.BlockSpec((None, 1, CMP_HIDDEN), lambda i, j: (j // 2, 0, 0)),
                  pl.BlockSpec((None, CMP_HIDDEN, NSA_DH), lambda i, j: (j // 2, 0, 0))],
        out_specs=pl.BlockSpec((None, None, nr, NSA_DH), lambda i, j: (i, j, 0, 0)),
        out_shape=jax.ShapeDtypeStruct((b, 4, nr, NSA_DH), BF16),
        compiler_params=_cp(("parallel", "arbitrary")),
        name="compress",
    )(ckv_r, pe, w1, b1, w2)


def _nsa_kernel(q_ref, kc_ref, vc_ref, ks_ref, vs_ref, kw_ref, vw_ref, eh_ref, sm_ref, ovt_ref, o_ref, *, tq):
    g = pl.program_id(1)
    qi = pl.program_id(2)
    t0 = qi * tq
    hpg = NSA_HPG
    rows4 = hpg * tq
    nb = eh_ref.shape[0] // SEL_BLOCK
    q = q_ref[...]
    qs = jnp.concatenate([q[:, hh * LANE:(hh + 1) * LANE] for hh in range(hpg)], axis=0)
    tile4 = lambda m: jnp.concatenate([m] * hpg, axis=0)

    def qpos(width):
        return t0 + (lax.broadcasted_iota(jnp.int32, (rows4, width), 0) & (tq - 1))

    lane4 = lax.broadcasted_iota(jnp.int32, (rows4, LANE), 1)
    cmask = lane4 * CMP_STRIDE + (CMP_BLOCK - 1) <= qpos(LANE)
    s_c = jnp.where(cmask, _bdot_nt(qs, kc_ref[...]), NEG)
    e_c = jnp.where(cmask, jnp.exp(s_c - jnp.max(s_c, -1, keepdims=True)), 0.0)
    p_c = e_c * (1.0 / jnp.maximum(jnp.sum(e_c, -1, keepdims=True), 1e-30))
    o_c = _bdot(p_c, vc_ref[...])

    p_sum = p_c[0:tq]
    for hh in range(1, hpg):
        p_sum = p_sum + p_c[hh * tq:(hh + 1) * tq]
    ovt = ovt_ref[...]
    imp_t = sum(lax.dot_general(ovt, part, (((1,), (1,)), ((), ())), preferred_element_type=F32)
                for part in _split3(p_sum))[:nb]
    blk = lax.broadcasted_iota(jnp.int32, (nb, tq), 0)
    t = t0 + lax.broadcasted_iota(jnp.int32, (nb, tq), 1)
    cur = t >> SEL_SHIFT
    forced = (blk == 0) | (blk == cur) | (blk == cur - 1)
    val = jnp.where(forced, jnp.inf, jnp.where(blk * SEL_BLOCK <= t, imp_t, -jnp.inf))
    rank = jnp.zeros((nb, tq), F32)
    for i in range(nb):
        vi = val[i:i + 1, :]
        rank = rank + jnp.where((vi > val) | ((vi == val) & (blk > i)), 1.0, 0.0)
    drop_t = jnp.where(rank < SEL_TOPK, 0.0, MASK_BIG)
    drop = jnp.concatenate([drop_t, jnp.zeros((LANE - nb, tq), F32)], axis=0).T
    q_aug = jnp.concatenate([qs, tile4((-drop).astype(BF16))], axis=1)

    ones_col = jnp.where(lax.broadcasted_iota(jnp.int32, (tq, LANE), 1) == 0, 1.0, 0.0).astype(BF16)
    kpos_d = lax.broadcasted_iota(jnp.int32, (rows4, tq), 1)
    row_d = lax.broadcasted_iota(jnp.int32, (rows4, tq), 0) & (tq - 1)

    prev0 = pl.multiple_of(jnp.maximum(qi - 1, 0) * tq, tq)
    own0 = pl.multiple_of(t0, tq)
    prev_lo = row_d + jnp.where(qi > 0, 0, tq)
    s_w = jnp.concatenate([jnp.where(kpos_d > prev_lo, _bdot_nt(qs, kw_ref[pl.ds(prev0, tq), :]), NEG),
                           jnp.where(kpos_d <= row_d, _bdot_nt(qs, kw_ref[pl.ds(own0, tq), :]), NEG)], axis=1)
    e_w = jnp.exp(s_w - jnp.max(s_w, -1, keepdims=True))
    v_w = jnp.concatenate([jnp.concatenate([vw_ref[pl.ds(prev0, tq), :], ones_col], axis=1),
                           jnp.concatenate([vw_ref[pl.ds(own0, tq), :], ones_col], axis=1)], axis=0)
    acc_w = _bdot(e_w, v_w)
    o_w = acc_w[:, :NSA_DH] * (1.0 / jnp.maximum(acc_w[:, NSA_DH:NSA_DH + 1], 1e-30))


    def sel_chunk(c, carry, diagonal):
        m_i, acc = carry
        k0 = pl.multiple_of(c * tq, tq)
        k_aug = jnp.concatenate([ks_ref[pl.ds(k0, tq), :], eh_ref[pl.ds(k0, tq), :]], axis=1)
        v_aug = jnp.concatenate([vs_ref[pl.ds(k0, tq), :], ones_col], axis=1)
        s = lax.dot_general(q_aug, k_aug, (((1,), (1,)), ((), ())), preferred_element_type=F32)
        if diagonal:
            s = jnp.where(kpos_d <= row_d, s, -MASK_BIG)
        m_new = jnp.maximum(m_i, jnp.max(s, -1, keepdims=True))
        p = jnp.exp(s - m_new)
        acc = jnp.exp(m_i - m_new) * acc + _bdot(p, v_aug)
        return m_new, acc

    carry = lax.fori_loop(0, qi, functools.partial(sel_chunk, diagonal=False),
                          (jnp.full((rows4, 1), -MASK_BIG, F32), jnp.zeros((rows4, 2 * NSA_DH), F32)))
    _, acc_s = sel_chunk(qi, carry, True)
    o_s = acc_s[:, :NSA_DH] * (1.0 / jnp.maximum(acc_s[:, NSA_DH:NSA_DH + 1], 1e-30))

    gates = _sigmoid(sm_ref[...])
    outs = []
    for hh in range(hpg):
        base = SM_NGATE + (g * hpg + hh) * 3
        r = slice(hh * tq, (hh + 1) * tq)
        outs.append(_lane_col(gates, base) * o_c[r] + _lane_col(gates, base + 1) * o_s[r]
                    + _lane_col(gates, base + 2) * o_w[r])
    o_ref[...] = jnp.concatenate(outs, axis=1).astype(o_ref.dtype)


def _nsa(q_r, kvc, kv_r, small3):
    b, s, _ = q_r.shape
    nr = kvc.shape[2]
    nb = s // SEL_BLOCK
    tq = WINDOW
    assert s % tq == 0 and nb <= LANE and nr <= LANE
    gw = NSA_HPG * LANE
    eh = (jnp.arange(s)[:, None] // SEL_BLOCK == jnp.arange(LANE)[None, :]).astype(BF16)
    kvspec = lambda off: pl.BlockSpec((None, s, LANE), lambda i, g, j, off=off: (i, 0, off + g))
    cspec = lambda off: pl.BlockSpec((None, None, nr, NSA_DH), lambda i, g, j, off=off: (i, off + g, 0, 0))
    return pl.pallas_call(
        functools.partial(_nsa_kernel, tq=tq),
        grid=(b, NSA_KV_HEADS, s // tq),
        in_specs=[pl.BlockSpec((None, tq, gw), lambda i, g, j: (i, j, g)),
                  cspec(0), cspec(2),
                  kvspec(0), kvspec(2), kvspec(4), kvspec(6),
                  pl.BlockSpec((s, LANE), lambda i, g, j: (0, 0)),
                  pl.BlockSpec((None, tq, LANE), lambda i, g, j: (i, j, 0)),
                  pl.BlockSpec((LANE, LANE), lambda i, g, j: (0, 0))],
        out_specs=pl.BlockSpec((None, tq, gw), lambda i, g, j: (i, j, g)),
        out_shape=jax.ShapeDtypeStruct((b, s, NSA_HEADS * NSA_DH), BF16),
        compiler_params=_cp(("parallel", "parallel", "arbitrary")),
        name="nsa",
    )(q_r, kvc, kvc, kv_r, kv_r, kv_r, kv_r, eh, small3, _overlap_matrix(s).T)


def _merge_kernel(ya_ref, yb_ref, mg_ref, x_ref, w_ref, g_ref, b_ref, o_ref):
    mg = mg_ref[...].astype(F32)
    merged = (_sigmoid(mg[:, :D_MODEL]) * ya_ref[...].astype(F32)
              + _sigmoid(mg[:, D_MODEL:]) * yb_ref[...].astype(F32))
    mix = _bdot(merged, w_ref[...])
    o_ref[...] = _layer_norm(DN_ALPHA * x_ref[...] + mix, g_ref[...], b_ref[...])


def _merge(y_a, y_b, proj, x2, w_out, g, b, tm=512):
    n = x2.shape[0]
    row = lambda w: pl.BlockSpec((tm, w), lambda i: (i, 0))
    full = lambda shape: pl.BlockSpec(shape, lambda i: (0, 0))
    return pl.pallas_call(
        _merge_kernel,
        grid=(n // tm,),
        in_specs=[row(D_MODEL), row(D_MODEL),
                  pl.BlockSpec((tm, 2 * D_MODEL), lambda i: (i, CB_MG * LANE // (2 * D_MODEL))),
                  row(D_MODEL), full((D_MODEL, D_MODEL)), full((1, D_MODEL)), full((1, D_MODEL))],
        out_specs=row(D_MODEL),
        out_shape=jax.ShapeDtypeStruct((n, D_MODEL), F32),
        compiler_params=_cp(("parallel",)),
        name="merge_ln1",
    )(y_a, y_b, proj, x2, w_out, g, b)


def _xattn_kernel(h_ref, kv_ref, wq_ref, wo_ref, g_ref, b_ref, o_ref):
    h = h_ref[...]
    q = _bdot(h, wq_ref[...]).astype(BF16)
    kv = kv_ref[...]
    outs = []
    for hh in range(XA_HEADS):
        sl = slice(hh * XA_DH, (hh + 1) * XA_DH)
        s = _bdot_nt(q[:, sl], kv[:, sl]) * (XA_DH ** -0.5)
        m = jnp.max(s, -1, keepdims=True)
        e = jnp.exp(s - m)
        p = e / jnp.sum(e, -1, keepdims=True)
        outs.append(_bdot(p, kv[:, D_MODEL + hh * XA_DH:D_MODEL + (hh + 1) * XA_DH]))
    o = jnp.concatenate(outs, axis=1)
    o_ref[...] = _layer_norm(DN_ALPHA * h + _bdot(o, wo_ref[...]), g_ref[...], b_ref[...])


def _xattn(h3, kv3, wq, wo, g, b, tm=512):
    bsz, s, _ = h3.shape
    m = kv3.shape[1]
    full = lambda shape: pl.BlockSpec(shape, lambda i, j: (0, 0))
    return pl.pallas_call(
        _xattn_kernel,
        grid=(bsz, s // tm),
        in_specs=[pl.BlockSpec((None, tm, D_MODEL), lambda i, j: (i, j, 0)),
                  pl.BlockSpec((None, m, 2 * D_MODEL), lambda i, j: (i, 0, 0)),
                  full((D_MODEL, D_MODEL)), full((D_MODEL, D_MODEL)), full((1, D_MODEL)), full((1, D_MODEL))],
        out_specs=pl.BlockSpec((None, tm, D_MODEL), lambda i, j: (i, j, 0)),
        out_shape=jax.ShapeDtypeStruct((bsz, s, D_MODEL), F32),
        compiler_params=_cp(("parallel", "parallel")),
        name="xattn_ln2",
    )(h3, kv3, wq, wo, g, b)


def _route(x, wr_h, wr_l, br):
    xh = x.astype(BF16)
    xl = (x - xh.astype(F32)).astype(BF16)
    logits = (jnp.dot(xh, wr_h, preferred_element_type=F32) + jnp.dot(xh, wr_l, preferred_element_type=F32)
              + jnp.dot(xl, wr_h, preferred_element_type=F32)) + br
    lane = lax.broadcasted_iota(jnp.int32, logits.shape, 1)
    gmask = lane < N_GROUPS
    gl = jnp.where(gmask, logits, -jnp.inf)
    gmax = jnp.max(gl, -1, keepdims=True)
    g_idx = jnp.min(jnp.where(gl == gmax, lane, LANE), -1, keepdims=True)
    p_top = 1.0 / jnp.sum(jnp.where(gmask, jnp.exp(gl - gmax), 0.0), -1, keepdims=True)
    e_lane = lane - N_GROUPS
    emask = (e_lane >= 0) & (e_lane < N_EXPERTS) & ((e_lane >> 3) == g_idx)
    el = jnp.where(emask, logits, -jnp.inf)
    emax = jnp.max(el, -1, keepdims=True)
    ee = jnp.where(emask, jnp.exp(el - emax), 0.0)
    pe = ee / jnp.sum(ee, -1, keepdims=True)
    pv = jnp.where(emask, pe, -jnp.inf)
    m1 = jnp.max(pv, -1, keepdims=True)
    i1 = jnp.min(jnp.where(pv == m1, lane, LANE), -1, keepdims=True)
    pv2 = jnp.where(lane == i1, -jnp.inf, pv)
    m2 = jnp.max(pv2, -1, keepdims=True)
    i2 = jnp.min(jnp.where(pv2 == m2, lane, LANE), -1, keepdims=True)
    denom = m1 + m2
    return i1, i2, p_top * m1 / denom, p_top * m2 / denom


META_E1, META_E2, META_R1, META_R2, META_W1, META_W2 = 0, 1, 2, 3, 4, 5


def _router_kernel(h_ref, wrh_ref, wrl_ref, br_ref, meta_ref, cnt_ref, carry_ref):
    @pl.when(pl.program_id(0) == 0)
    def _():
        carry_ref[...] = jnp.zeros_like(carry_ref)

    tm = h_ref.shape[0]
    i1, i2, w1, w2 = _route(h_ref[...], wrh_ref[...], wrl_ref[...], br_ref[...])
    lane = lax.broadcasted_iota(jnp.int32, (tm, LANE), 1)
    onehot = jnp.where((lane == i1) | (lane == i2), 1.0, 0.0)
    earlier = jnp.where(lax.broadcasted_iota(jnp.int32, (tm, tm), 1) < lax.broadcasted_iota(jnp.int32, (tm, tm), 0),
                        1.0, 0.0).astype(BF16)
    before = jnp.dot(earlier, onehot.astype(BF16), preferred_element_type=F32) + carry_ref[...]
    r1 = jnp.sum(jnp.where(lane == i1, before, 0.0), -1, keepdims=True)
    r2 = jnp.sum(jnp.where(lane == i2, before, 0.0), -1, keepdims=True)
    carry_ref[...] += jnp.sum(onehot, axis=0, keepdims=True)
    cnt_ref[...] = carry_ref[...]
    cols = ((META_E1, (i1 - N_GROUPS).astype(F32)), (META_E2, (i2 - N_GROUPS).astype(F32)),
            (META_R1, r1), (META_R2, r2), (META_W1, w1), (META_W2, w2))
    meta = jnp.zeros((tm, LANE), F32)
    for c, v in cols:
        meta = jnp.where(lane == c, v, meta)
    meta_ref[...] = meta


def _router(h2, wr_h, wr_l, br, tm=1024):
    n = h2.shape[0]
    full = lambda shape: pl.BlockSpec(shape, lambda i: (0, 0))
    return pl.pallas_call(
        _router_kernel,
        grid=(n // tm,),
        in_specs=[pl.BlockSpec((tm, D_MODEL), lambda i: (i, 0)),
                  full((D_MODEL, LANE)), full((D_MODEL, LANE)), full((1, LANE))],
        out_specs=[pl.BlockSpec((tm, LANE), lambda i: (i, 0)), full((1, LANE))],
        out_shape=[jax.ShapeDtypeStruct((n, LANE), F32), jax.ShapeDtypeStruct((1, LANE), F32)],
        scratch_shapes=[pltpu.VMEM((1, LANE), F32)],
        compiler_params=_cp(("arbitrary",)),
        name="moe_router",
    )(h2, wr_h, wr_l, br)


def _row_copies(n_rows, pos_refs, make_copy):
    def body(r, carry):
        for p in pos_refs:
            make_copy(r, p[r]).start()
        return carry

    lax.fori_loop(0, n_rows, body, 0, unroll=8)


def _dispatch_kernel(pos1_ref, pos2_ref, h_ref, xs_ref, sem):
    tm = h_ref.shape[0]
    _row_copies(tm, (pos1_ref, pos2_ref),
                lambda r, p: pltpu.make_async_copy(h_ref.at[pl.ds(r, 1)], xs_ref.at[pl.ds(p, 1)], sem))
    for _ in range(2):
        pltpu.make_async_copy(h_ref, xs_ref.at[pl.ds(0, tm)], sem).wait()


def _dispatch(h2, pos1, pos2, n_rows, tm=512):
    n = h2.shape[0]
    smem = pl.BlockSpec((tm,), lambda i: (i,), memory_space=pltpu.SMEM)
    return pl.pallas_call(
        _dispatch_kernel,
        grid=(n // tm,),
        in_specs=[smem, smem, pl.BlockSpec((tm, D_MODEL), lambda i: (i, 0))],
        out_specs=pl.BlockSpec(memory_space=pl.ANY),
        out_shape=jax.ShapeDtypeStruct((n_rows, D_MODEL), F32),
        scratch_shapes=[pltpu.SemaphoreType.DMA],
        compiler_params=_cp(("arbitrary",)),
        name="moe_dispatch",
    )(pos1, pos2, h2)


def _ffn_kernel(tile_ref, exp_ref, lo_ref, hi_ref, first_ref, xs_ref, wg_ref, wu_ref, wd_ref, ys_ref):
    i = pl.program_id(0)
    tm = xs_ref.shape[0]
    rows = lax.broadcasted_iota(jnp.int32, (tm, 1), 0)
    mine = (rows >= lo_ref[i]) & (rows < hi_ref[i])
    x = xs_ref[...].astype(BF16)
    gate = jnp.dot(x, wg_ref[...].astype(BF16), preferred_element_type=F32)
    up = jnp.dot(x, wu_ref[...].astype(BF16), preferred_element_type=F32)
    y = _bdot(_silu(gate) * up, wd_ref[...])

    @pl.when(first_ref[i] == 1)
    def _():
        ys_ref[...] = jnp.where(mine, y, 0.0)

    @pl.when(first_ref[i] == 0)
    def _():
        ys_ref[...] = jnp.where(mine, y, ys_ref[...])


def _ffn(xs, items, wg, wu, wd, tm):
    n_rows = xs.shape[0]
    wspec = lambda shape: pl.BlockSpec((None,) + shape, lambda i, tl, ex, lo, hi, fi: (ex[i], 0, 0))
    row_tile = pl.BlockSpec((tm, D_MODEL), lambda i, tl, ex, lo, hi, fi: (tl[i], 0))
    return pl.pallas_call(
        _ffn_kernel,
        grid_spec=pltpu.PrefetchScalarGridSpec(
            num_scalar_prefetch=5,
            grid=(items[0].shape[0],),
            in_specs=[row_tile, wspec((D_MODEL, D_FF)), wspec((D_MODEL, D_FF)), wspec((D_FF, D_MODEL))],
            out_specs=row_tile),
        out_shape=jax.ShapeDtypeStruct((n_rows, D_MODEL), F32),
        compiler_params=_cp(("arbitrary",)),
        name="moe_ffn",
    )(*items, xs, wg, wu, wd)


def _combine_kernel(pos1_ref, pos2_ref, h_ref, meta_ref, ys_ref, g_ref, b_ref, o_ref, buf_ref, sem):
    tm = h_ref.shape[0]
    bufs = (buf_ref.at[0], buf_ref.at[1])
    for pos_ref, buf in zip((pos1_ref, pos2_ref), bufs):
        _row_copies(tm, (pos_ref,),
                    lambda r, p, buf=buf: pltpu.make_async_copy(ys_ref.at[pl.ds(p, 1)], buf.at[pl.ds(r, 1)], sem))
    for buf in bufs:
        pltpu.make_async_copy(ys_ref.at[pl.ds(0, tm)], buf, sem).wait()
    meta = meta_ref[...]
    ffn = meta[:, META_W1:META_W1 + 1] * buf_ref[0] + meta[:, META_W2:META_W2 + 1] * buf_ref[1]
    o_ref[...] = _layer_norm(DN_ALPHA * h_ref[...] + ffn, g_ref[...], b_ref[...])


def _combine(h2, meta, ys, pos1, pos2, g, b, tm=256):
    n = h2.shape[0]
    smem = pl.BlockSpec((tm,), lambda i: (i,), memory_space=pltpu.SMEM)
    full = lambda shape: pl.BlockSpec(shape, lambda i: (0, 0))
    return pl.pallas_call(
        _combine_kernel,
        grid=(n // tm,),
        in_specs=[smem, smem, pl.BlockSpec((tm, D_MODEL), lambda i: (i, 0)),
                  pl.BlockSpec((tm, LANE), lambda i: (i, 0)),
                  pl.BlockSpec(memory_space=pl.ANY), full((1, D_MODEL)), full((1, D_MODEL))],
        out_specs=pl.BlockSpec((tm, D_MODEL), lambda i: (i, 0)),
        out_shape=jax.ShapeDtypeStruct((n, D_MODEL), F32),
        scratch_shapes=[pltpu.VMEM((2, tm, D_MODEL), F32), pltpu.SemaphoreType.DMA],
        compiler_params=_cp(("arbitrary",)),
        name="moe_combine_ln3",
    )(pos1, pos2, h2, meta, ys, g, b)


def _moe(h2, wr_h, wr_l, br, wg, wu, wd, g, b, tile=256):
    n = h2.shape[0]
    i32 = jnp.int32
    meta, counts = _router(h2, wr_h, wr_l, br)
    cnt = counts[0, N_GROUPS:N_GROUPS + N_EXPERTS].astype(i32)
    seg_end = jnp.cumsum(cnt)
    seg_start = seg_end - cnt
    col = lambda c: meta[:, c].astype(i32)
    pos1 = seg_start[col(META_E1)] + col(META_R1)
    pos2 = seg_start[col(META_E2)] + col(META_R2)
    first_tile = seg_start // tile
    n_items_e = jnp.where(cnt > 0, (seg_end - 1) // tile - first_tile + 1, 0)
    item_end = jnp.cumsum(n_items_e)
    n_items = (2 * n) // tile + N_EXPERTS - 1
    idx = jnp.minimum(jnp.arange(n_items, dtype=i32), item_end[-1] - 1)
    exp = jnp.sum(idx[:, None] >= item_end[None, :], axis=1).astype(i32)
    til = first_tile[exp] + idx - (item_end - n_items_e)[exp]
    lo = jnp.maximum(seg_start[exp], til * tile) - til * tile
    hi = jnp.minimum(seg_end[exp], (til + 1) * tile) - til * tile
    repeat = jnp.arange(n_items, dtype=i32) >= item_end[-1]
    hi = jnp.where(repeat, lo, hi)
    first = jnp.concatenate([jnp.ones((1,), i32), (til[1:] != til[:-1]).astype(i32)])
    xs = _dispatch(h2, pos1, pos2, 2 * n)
    ys = _ffn(xs, (til.astype(i32), exp, lo.astype(i32), hi.astype(i32), first), wg, wu, wd, tile)
    return _combine(h2, meta, ys, pos1, pos2, g, b)


def _regroup_w_in(w):
    sizes = (1024, 1024, 1024, 1024, 8, 8, 1024, 256, 256, 256, 256, 256, 256, 24, 2048)
    offs = [0]
    for sz in sizes:
        offs.append(offs[-1] + sz)
    seg = lambda i: w[:, offs[i]:offs[i + 1]]
    small = jnp.concatenate([seg(4), seg(5), seg(13)], axis=1)
    small = jnp.pad(small, ((0, 0), (0, LANE - small.shape[1])))
    big = jnp.concatenate([seg(0), seg(1), seg(2), seg(3), seg(14), seg(6), seg(7), seg(8), seg(9), seg(10),
                           seg(11), seg(12)], axis=1)
    return big.astype(BF16), small.astype(BF16)


def _overlap_matrix(s):
    nb = s // SEL_BLOCK
    c0 = jnp.arange(LANE) * CMP_STRIDE
    s0 = jnp.arange(LANE) * SEL_BLOCK
    ov = jnp.minimum(c0[:, None] + CMP_BLOCK, s0[None, :] + SEL_BLOCK) - jnp.maximum(c0[:, None], s0[None, :])
    ov = jnp.maximum(ov, 0).astype(F32) / CMP_BLOCK
    nc = (s - CMP_BLOCK) // CMP_STRIDE + 1
    keep = (jnp.arange(LANE)[:, None] < nc) & (jnp.arange(LANE)[None, :] < nb)
    return jnp.where(keep, ov, 0.0).astype(BF16)


def _layer(h, mem, cos2, sin2, w_in, conv_w, a_log, dt_bias, norm_w, cmp_pe, cmp_w1, cmp_b1, cmp_w2, w_out,
           ln1_g, ln1_b, xa_wq, xa_wkv, xa_wo, ln2_g, ln2_b, w_group, b_group, w_expert, b_expert,
           w_gate, w_up, w_down, ln3_g, ln3_b):
    b, s, d = h.shape
    n = b * s
    x2 = h.reshape(n, d)
    proj, small = _inproj(x2, *_regroup_w_in(w_in))
    proj3 = proj.reshape(b, s, -1)
    small3 = small.reshape(b, s, LANE)

    y_a = _gdn(proj3, small3, conv_w, a_log, dt_bias, norm_w)

    q_r, ckv_r, kv_r = _rope(proj3, cos2, sin2)
    ckv_r = ckv_r.reshape(b, 4, s // CMP_STRIDE, CMP_STRIDE * NSA_DH)
    kvc = _compress(ckv_r, cmp_pe.reshape(2, 1, CMP_BLOCK * NSA_DH).astype(BF16), cmp_w1.astype(BF16),
                    cmp_b1.reshape(2, 1, CMP_HIDDEN), cmp_w2.astype(BF16))
    y_b = _nsa(q_r, kvc, kv_r, small3)

    row = lambda v: v.reshape(1, -1)
    h1 = _merge(y_a.reshape(n, d), y_b.reshape(n, d), proj, x2, w_out.astype(BF16), row(ln1_g), row(ln1_b))

    m = mem.shape[1]
    kv = _matmul(mem.reshape(b * m, d).astype(BF16), xa_wkv.astype(BF16), BF16, tm=512, tn=512)
    h2 = _xattn(h1.reshape(b, s, d), kv.reshape(b, m, 2 * d), xa_wq.astype(BF16), xa_wo.astype(BF16),
                row(ln2_g), row(ln2_b))

    wr = jnp.pad(jnp.concatenate([w_group, w_expert], axis=1), ((0, 0), (0, LANE - N_GROUPS - N_EXPERTS)))
    wr_h = wr.astype(BF16)
    wr_l = (wr - wr_h.astype(F32)).astype(BF16)
    br = jnp.pad(jnp.concatenate([b_group, b_expert]), (0, LANE - N_GROUPS - N_EXPERTS)).reshape(1, LANE)
    h3 = _moe(h2.reshape(n, d), wr_h, wr_l, br, w_gate, w_up, w_down, row(ln3_g), row(ln3_b))
    return h3.reshape(b, s, d)


def kernel(x, mem, positions, w_in, gdn_conv_w, gdn_a_log, gdn_dt_bias, gdn_norm_w, cmp_pe, cmp_w1, cmp_b1, cmp_w2, w_out, ln1_g, ln1_b, xa_wq, xa_wkv, xa_wo, ln2_g, ln2_b, moe_w_group, moe_b_group, moe_w_expert, moe_b_expert, moe_w_gate, moe_w_up, moe_w_down, ln3_g, ln3_b):
    half = NSA_DH // 2
    inv_freq = ROPE_THETA ** (-jnp.arange(half, dtype=F32) / half)
    ang = positions.astype(F32)[..., None] * inv_freq
    cos, sin = jnp.cos(ang), jnp.sin(ang)
    cos2 = jnp.concatenate([cos, cos], -1)
    sin2 = jnp.concatenate([-sin, sin], -1)
    h = x
    for l in range(DEPTH):
        h = _layer(h, mem, cos2, sin2, w_in[l], gdn_conv_w[l], gdn_a_log[l], gdn_dt_bias[l], gdn_norm_w[l],
                   cmp_pe[l], cmp_w1[l], cmp_b1[l], cmp_w2[l], w_out[l], ln1_g[l], ln1_b[l],
                   xa_wq[l], xa_wkv[l], xa_wo[l], ln2_g[l], ln2_b[l], moe_w_group[l], moe_b_group[l],
                   moe_w_expert[l], moe_b_expert[l], moe_w_gate[l], moe_w_up[l], moe_w_down[l],
                   ln3_g[l], ln3_b[l])
    return h
```

```python
import functools

import jax
import jax.numpy as jnp
from jax import lax
from jax.experimental import pallas as pl
from jax.experimental.pallas import tpu as pltpu

F32 = jnp.float32
BF16 = jnp.bfloat16

D_MODEL = 1024
LANE = 128
GDN_HEADS = 8
GDN_D = 128
GDN_CONV = 4
GDN_CHUNK = 64
NSA_HEADS = 8
NSA_KV_HEADS = 2
NSA_HPG = NSA_HEADS // NSA_KV_HEADS
NSA_DH = 128
CMP_BLOCK = 32
CMP_STRIDE = 16
CMP_HIDDEN = 256
SEL_BLOCK = 64
SEL_SHIFT = 6
SEL_TOPK = 8
WINDOW = 256
XA_HEADS = 4
XA_DH = 256
N_GROUPS = 4
EXPERTS_PER_GROUP = 8
N_EXPERTS = 32
D_FF = 256
DEPTH = 1
DN_ALPHA = (2.0 * DEPTH) ** 0.25
LN_EPS = 1e-5
RMS_EPS = 1e-6
ROPE_THETA = 10000.0
NEG = -1e30
MASK_BIG = 1e30

CB_Q, CB_K, CB_V, CB_Z = 0, 8, 16, 24
CB_MG = 32
CB_NQ = 48
CB_KV = 56
SM_BETA, SM_DECAY, SM_NGATE = 0, 8, 16

VMEM_LIMIT = 48 * 1024 * 1024


def _cp(sem, vmem=VMEM_LIMIT):
    return pltpu.CompilerParams(dimension_semantics=sem, vmem_limit_bytes=vmem)


def _bdot(a, b):
    return jnp.dot(a.astype(BF16), b.astype(BF16), preferred_element_type=F32)


def _bdot_nt(a, b):
    return lax.dot_general(a.astype(BF16), b.astype(BF16), (((1,), (1,)), ((), ())),
                           preferred_element_type=F32)


def _bdot_tn(a, b):
    return lax.dot_general(a.astype(BF16), b.astype(BF16), (((0,), (0,)), ((), ())),
                           preferred_element_type=F32)


def _split3(x):
    h = x.astype(BF16)
    r = x - h.astype(F32)
    m = r.astype(BF16)
    l = (r - m.astype(F32)).astype(BF16)
    return h, m, l


def _sigmoid(x):
    return 1.0 / (1.0 + jnp.exp(-x))


def _silu(x):
    return x * _sigmoid(x)


def _layer_norm(x, g, b):
    mu = jnp.mean(x, -1, keepdims=True)
    xc = x - mu
    var = jnp.mean(xc * xc, -1, keepdims=True)
    return xc * lax.rsqrt(var + LN_EPS) * g + b


def _lane_col(x, c):
    lane = lax.broadcasted_iota(jnp.int32, x.shape, 1)
    return jnp.sum(jnp.where(lane == c, x, 0.0), axis=-1, keepdims=True)


def _mm_kernel(x_ref, w_ref, o_ref):
    o_ref[...] = jnp.dot(x_ref[...], w_ref[...], preferred_element_type=F32).astype(o_ref.dtype)


def _matmul(x, w, out_dtype, tm, tn):
    m, k = x.shape
    n = w.shape[1]
    return pl.pallas_call(
        _mm_kernel,
        grid=(m // tm, n // tn),
        in_specs=[pl.BlockSpec((tm, k), lambda i, j: (i, 0)),
                  pl.BlockSpec((k, tn), lambda i, j: (0, j))],
        out_specs=pl.BlockSpec((tm, tn), lambda i, j: (i, j)),
        out_shape=jax.ShapeDtypeStruct((m, n), out_dtype),
        compiler_params=_cp(("parallel", "parallel")),
        name="matmul",
    )(x, w)


def _inproj_kernel(x_ref, w_ref, ws_ref, o_ref, os_ref, xb_ref):
    @pl.when(pl.program_id(1) == 0)
    def _():
        xb_ref[...] = x_ref[...].astype(BF16)
        os_ref[...] = jnp.dot(xb_ref[...], ws_ref[...], preferred_element_type=F32)

    o_ref[...] = jnp.dot(xb_ref[...], w_ref[...], preferred_element_type=F32).astype(o_ref.dtype)


def _inproj(x2, w_big, w_small, tm=2048, tn=512):
    m, k = x2.shape
    n = w_big.shape[1]
    return pl.pallas_call(
        _inproj_kernel,
        grid=(m // tm, n // tn),
        in_specs=[pl.BlockSpec((tm, k), lambda i, j: (i, 0)),
                  pl.BlockSpec((k, tn), lambda i, j: (0, j)),
                  pl.BlockSpec((k, LANE), lambda i, j: (0, 0))],
        out_specs=[pl.BlockSpec((tm, tn), lambda i, j: (i, j)),
                   pl.BlockSpec((tm, LANE), lambda i, j: (i, 0))],
        out_shape=[jax.ShapeDtypeStruct((m, n), BF16), jax.ShapeDtypeStruct((m, LANE), F32)],
        scratch_shapes=[pltpu.VMEM((tm, k), BF16)],
        compiler_params=_cp(("parallel", "arbitrary")),
        name="inproj",
    )(x2, w_big, w_small)


def _gdn_kernel(q_ref, k_ref, v_ref, z_ref, sm_ref, cwq_ref, cwk_ref, cwv_ref,
                alog_ref, dtb_ref, nw_ref, o_ref, state_ref, tail_ref):
    c_len = q_ref.shape[0]

    @pl.when(pl.program_id(1) == 0)
    def _():
        state_ref[...] = jnp.zeros_like(state_ref)
        tail_ref[...] = jnp.zeros_like(tail_ref)

    row = lax.broadcasted_iota(jnp.int32, (c_len, c_len), 0)
    col = lax.broadcasted_iota(jnp.int32, (c_len, c_len), 1)
    causal = row >= col
    strict = row > col
    eye = jnp.where(row == col, 1.0, 0.0)
    tril_incl = jnp.where(causal, 1.0, 0.0).astype(BF16)

    def conv_silu(idx, x_ref, w_ref):
        cur = x_ref[...].astype(F32)
        xc = jnp.concatenate([tail_ref[idx], cur], axis=0)
        tail_ref[idx] = cur[c_len - 8:, :]
        w = w_ref[...]
        acc = xc[8:, :] * w[GDN_CONV - 1:GDN_CONV, :]
        for j in range(GDN_CONV - 1):
            off = 8 - (GDN_CONV - 1) + j
            acc = acc + xc[off:off + c_len, :] * w[j:j + 1, :]
        return _silu(acc)

    q_all = conv_silu(0, q_ref, cwq_ref)
    k_all = conv_silu(1, k_ref, cwk_ref)
    v_all = conv_silu(2, v_ref, cwv_ref)

    sm = sm_ref[...]
    beta_all = _sigmoid(sm)
    a_in = sm + dtb_ref[...]
    softplus = jnp.maximum(a_in, 0.0) + jnp.log(1.0 + jnp.exp(-jnp.abs(a_in)))
    g_all = -jnp.exp(alog_ref[...]) * softplus
    gh, gm, gl = _split3(g_all)
    gc_all = (jnp.dot(tril_incl, gh, preferred_element_type=F32)
              + jnp.dot(tril_incl, gm, preferred_element_type=F32)
              + jnp.dot(tril_incl, gl, preferred_element_type=F32))
    gc_t = gc_all.T

    heads = range(GDN_HEADS)
    hs = lambda x, h: x[:, h * GDN_D:(h + 1) * GDN_D]
    q = [hs(q_all, h) for h in heads]
    k = [hs(k_all, h) for h in heads]
    v = [hs(v_all, h) for h in heads]
    q = [x * lax.rsqrt(jnp.sum(x * x, -1, keepdims=True) + RMS_EPS) * (GDN_D ** -0.5) for x in q]
    k = [x * lax.rsqrt(jnp.sum(x * x, -1, keepdims=True) + RMS_EPS) for x in k]
    beta = [beta_all[:, SM_BETA + h:SM_BETA + h + 1] for h in heads]
    gc = [gc_all[:, SM_DECAY + h:SM_DECAY + h + 1] for h in heads]
    g_last = [x[c_len - 1:c_len, :] for x in gc]
    diff = [gc[h] - gc_t[SM_DECAY + h:SM_DECAY + h + 1, :] for h in heads]
    decay = [jnp.where(causal, jnp.exp(d), 0.0) for d in diff]
    kk = [_bdot_nt(x, x) for x in k]
    m_pow = [-(jnp.where(strict, kk[h] * decay[h], 0.0) * beta[h]) for h in heads]
    t_inv = [eye + m for m in m_pow]
    for _ in range((c_len - 1).bit_length() - 1):
        m_pow = [_bdot(m, m) for m in m_pow]
        t_inv = [t + _bdot(t, m) for t, m in zip(t_inv, m_pow)]
    e_gc = [jnp.exp(x) for x in gc]
    u = [_bdot(t_inv[h], v[h] * beta[h]) for h in heads]
    w = [_bdot(t_inv[h], k[h] * (beta[h] * e_gc[h])) for h in heads]
    qk = [_bdot_nt(q[h], k[h]) * decay[h] for h in heads]
    q_dec = [q[h] * e_gc[h] for h in heads]
    k_dec = [k[h] * jnp.exp(g_last[h] - gc[h]) for h in heads]
    state = [state_ref[h] for h in heads]
    v_new = [u[h] - _bdot(w[h], state[h]) for h in heads]
    o = [_bdot(q_dec[h], state[h]) + _bdot(qk[h], v_new[h]) for h in heads]
    for h in heads:
        state_ref[h] = state[h] * jnp.exp(g_last[h]) + _bdot_tn(k_dec[h], v_new[h])
    nw = nw_ref[...]
    o = [x * lax.rsqrt(jnp.mean(x * x, -1, keepdims=True) + RMS_EPS) * nw for x in o]
    o_ref[...] = (jnp.concatenate(o, axis=1) * _silu(z_ref[...].astype(F32))).astype(o_ref.dtype)


def _gdn(proj3, small3, conv_w, a_log, dt_bias, norm_w, c_len=128):
    b, s, _ = proj3.shape
    width = GDN_HEADS * GDN_D
    col = lambda off: pl.BlockSpec((None, c_len, width), lambda i, j, off=off: (i, j, off // GDN_HEADS))
    cw = lambda k: pl.BlockSpec((GDN_CONV, width), lambda i, j, k=k: (0, k))
    full = lambda shape: pl.BlockSpec(shape, lambda i, j: (0, 0))
    pad = lambda v: jnp.pad(v, (SM_DECAY, LANE - SM_DECAY - GDN_HEADS)).reshape(1, LANE)
    return pl.pallas_call(
        _gdn_kernel,
        grid=(b, s // c_len),
        in_specs=[col(CB_Q), col(CB_K), col(CB_V), col(CB_Z),
                  pl.BlockSpec((None, c_len, LANE), lambda i, j: (i, j, 0)),
                  cw(0), cw(1), cw(2),
                  full((1, LANE)), full((1, LANE)), full((1, GDN_D))],
        out_specs=pl.BlockSpec((None, c_len, width), lambda i, j: (i, j, 0)),
        out_shape=jax.ShapeDtypeStruct((b, s, width), BF16),
        scratch_shapes=[pltpu.VMEM((GDN_HEADS, GDN_D, GDN_D), F32), pltpu.VMEM((3, 8, width), F32)],
        compiler_params=_cp(("parallel", "arbitrary")),
        name="gdn",
    )(proj3, proj3, proj3, proj3, small3, conv_w, conv_w, conv_w,
      pad(a_log), pad(dt_bias), norm_w.reshape(1, GDN_D))


def _rope_kernel(nq_ref, kv0_ref, kv1_ref, kv2_ref, cos_ref, sin_ref, q_out, ckv_out, kv_out, tmp_ref):
    cos2 = cos_ref[...]
    sin2 = sin_ref[...]

    def rope(x):
        x = x.astype(F32)
        return x * cos2 + pltpu.roll(x, NSA_DH // 2, 1) * sin2

    scale = NSA_DH ** -0.5
    q_out[...] = jnp.concatenate(
        [rope(nq_ref[:, hh * LANE:(hh + 1) * LANE]) * scale for hh in range(NSA_HEADS)],
        axis=1).astype(q_out.dtype)
    kv_refs = (kv0_ref, kv1_ref, kv2_ref)
    blk = lambda i: kv_refs[i // 4][:, (i % 4) * LANE:(i % 4 + 1) * LANE]
    n_out = tmp_ref.shape[0] // CMP_STRIDE
    for slot, val in enumerate((rope(blk(0)), rope(blk(1)), blk(2).astype(F32), blk(3).astype(F32))):
        tmp_ref[...] = val
        for l in range(CMP_STRIDE):
            ckv_out[slot, :, l * LANE:(l + 1) * LANE] = tmp_ref[pl.ds(l, n_out, stride=CMP_STRIDE), :].astype(
                ckv_out.dtype)
    kv_out[...] = jnp.concatenate(
        [rope(blk(4)), rope(blk(5)), blk(6), blk(7), rope(blk(8)), rope(blk(9)), blk(10), blk(11)],
        axis=1).astype(kv_out.dtype)


def _rope(proj3, cos2, sin2, ts=512):
    b, s, _ = proj3.shape
    kvspec = lambda k: pl.BlockSpec((None, ts, 4 * LANE), lambda i, j, k=k: (i, j, CB_KV // 4 + k))
    return pl.pallas_call(
        _rope_kernel,
        grid=(b, s // ts),
        in_specs=[pl.BlockSpec((None, ts, NSA_HEADS * LANE), lambda i, j: (i, j, CB_NQ // NSA_HEADS)),
                  kvspec(0), kvspec(1), kvspec(2),
                  pl.BlockSpec((None, ts, LANE), lambda i, j: (i, j, 0)),
                  pl.BlockSpec((None, ts, LANE), lambda i, j: (i, j, 0))],
        out_specs=[pl.BlockSpec((None, ts, NSA_HEADS * LANE), lambda i, j: (i, j, 0)),
                   pl.BlockSpec((None, 4, ts // CMP_STRIDE, CMP_STRIDE * LANE), lambda i, j: (i, 0, j, 0)),
                   pl.BlockSpec((None, ts, 8 * LANE), lambda i, j: (i, j, 0))],
        out_shape=[jax.ShapeDtypeStruct((b, s, NSA_HEADS * LANE), BF16),
                   jax.ShapeDtypeStruct((b, 4, s // CMP_STRIDE, CMP_STRIDE * LANE), BF16),
                   jax.ShapeDtypeStruct((b, s, 8 * LANE), BF16)],
        scratch_shapes=[pltpu.VMEM((ts, LANE), F32)],
        compiler_params=_cp(("parallel", "parallel")),
        name="rope",
    )(proj3, proj3, proj3, proj3, cos2, sin2)


def _compress_kernel(x_ref, pe_ref, w1_ref, b1_ref, w2_ref, o_ref):
    x = x_ref[...]
    half = CMP_STRIDE * NSA_DH
    first = jnp.dot(x, w1_ref[:half, :], preferred_element_type=F32)
    second = jnp.dot(x, w1_ref[half:, :], preferred_element_type=F32)
    n_rows = x.shape[0]
    second = pltpu.roll(second, n_rows - 1, 0)
    pe = jnp.broadcast_to(pe_ref[...], (8, pe_ref.shape[1]))
    bias = jnp.dot(pe, w1_ref[...], preferred_element_type=F32)[0:1] + b1_ref[...]
    hid = first + second + bias
    gelu = 0.5 * hid * (1.0 + jnp.tanh(0.7978845608028654 * (hid + 0.044715 * hid * hid * hid)))
    o_ref[...] = _bdot(gelu, w2_ref[...]).astype(o_ref.dtype)


def _compress(ckv_r, pe, w1, b1, w2):
    b, _, nr, wd = ckv_r.shape
    return pl.pallas_call(
        _compress_kernel,
        grid=(b, 4),
        in_specs=[pl.BlockSpec((None, None, nr, wd), lambda i, j: (i, j, 0, 0)),
                  pl.BlockSpec((None, 1, CMP_BLOCK * NSA_DH), lambda i, j: (j // 2, 0, 0)),
                  pl.BlockSpec((None, CMP_BLOCK * NSA_DH, CMP_HIDDEN), lambda i, j: (j // 2, 0, 0)),
                  pl.BlockSpec((None, 1, CMP_HIDDEN), lambda i, j: (j // 2, 0, 0)),
                  pl.BlockSpec((None, CMP_HIDDEN, NSA_DH), lambda i, j: (j // 2, 0, 0))],
        out_specs=pl.BlockSpec((None, None, nr, NSA_DH), lambda i, j: (i, j, 0, 0)),
        out_shape=jax.ShapeDtypeStruct((b, 4, nr, NSA_DH), BF16),
        compiler_params=_cp(("parallel", "arbitrary")),
        name="compress",
    )(ckv_r, pe, w1, b1, w2)


def _nsa_kernel(q_ref, kc_ref, vc_ref, ks_ref, vs_ref, kw_ref, vw_ref, eh_ref, sm_ref, ovt_ref, o_ref, *, tq):
    g = pl.program_id(1)
    qi = pl.program_id(2)
    t0 = qi * tq
    hpg = NSA_HPG
    rows4 = hpg * tq
    nb = eh_ref.shape[0] // SEL_BLOCK
    q = q_ref[...]
    qs = jnp.concatenate([q[:, hh * LANE:(hh + 1) * LANE] for hh in range(hpg)], axis=0)
    tile4 = lambda m: jnp.concatenate([m] * hpg, axis=0)

    def qpos(width):
        return t0 + (lax.broadcasted_iota(jnp.int32, (rows4, width), 0) & (tq - 1))

    lane4 = lax.broadcasted_iota(jnp.int32, (rows4, LANE), 1)
    cmask = lane4 * CMP_STRIDE + (CMP_BLOCK - 1) <= qpos(LANE)
    s_c = jnp.where(cmask, _bdot_nt(qs, kc_ref[...]), NEG)
    e_c = jnp.where(cmask, jnp.exp(s_c - jnp.max(s_c, -1, keepdims=True)), 0.0)
    p_c = e_c * (1.0 / jnp.maximum(jnp.sum(e_c, -1, keepdims=True), 1e-30))
    o_c = _bdot(p_c, vc_ref[...])

    p_sum = p_c[0:tq]
    for hh in range(1, hpg):
        p_sum = p_sum + p_c[hh * tq:(hh + 1) * tq]
    ovt = ovt_ref[...]
    imp_t = sum(lax.dot_general(ovt, part, (((1,), (1,)), ((), ())), preferred_element_type=F32)
                for part in _split3(p_sum))[:nb]
    blk = lax.broadcasted_iota(jnp.int32, (nb, tq), 0)
    t = t0 + lax.broadcasted_iota(jnp.int32, (nb, tq), 1)
    cur = t >> SEL_SHIFT
    forced = (blk == 0) | (blk == cur) | (blk == cur - 1)
    val = jnp.where(forced, jnp.inf, jnp.where(blk * SEL_BLOCK <= t, imp_t, -jnp.inf))
    rank = jnp.zeros((nb, tq), F32)
    for i in range(nb):
        vi = val[i:i + 1, :]
        rank = rank + jnp.where((vi > val) | ((vi == val) & (blk > i)), 1.0, 0.0)
    drop_t = jnp.where(rank < SEL_TOPK, 0.0, MASK_BIG)
    drop = jnp.concatenate([drop_t, jnp.zeros((LANE - nb, tq), F32)], axis=0).T
    q_aug = jnp.concatenate([qs, tile4((-drop).astype(BF16))], axis=1)

    ones_col = jnp.where(lax.broadcasted_iota(jnp.int32, (tq, LANE), 1) == 0, 1.0, 0.0).astype(BF16)
    kpos_d = lax.broadcasted_iota(jnp.int32, (rows4, tq), 1)
    row_d = lax.broadcasted_iota(jnp.int32, (rows4, tq), 0) & (tq - 1)

    prev0 = pl.multiple_of(jnp.maximum(qi - 1, 0) * tq, tq)
    own0 = pl.multiple_of(t0, tq)
    prev_lo = row_d + jnp.where(qi > 0, 0, tq)
    s_w = jnp.concatenate([jnp.where(kpos_d > prev_lo, _bdot_nt(qs, kw_ref[pl.ds(prev0, tq), :]), NEG),
                           jnp.where(kpos_d <= row_d, _bdot_nt(qs, kw_ref[pl.ds(own0, tq), :]), NEG)], axis=1)
    e_w = jnp.exp(s_w - jnp.max(s_w, -1, keepdims=True))
    v_w = jnp.concatenate([jnp.concatenate([vw_ref[pl.ds(prev0, tq), :], ones_col], axis=1),
                           jnp.concatenate([vw_ref[pl.ds(own0, tq), :], ones_col], axis=1)], axis=0)
    acc_w = _bdot(e_w, v_w)
    o_w = acc_w[:, :NSA_DH] * (1.0 / jnp.maximum(acc_w[:, NSA_DH:NSA_DH + 1], 1e-30))


    def sel_chunk(c, carry, diagonal):
        m_i, acc = carry
        k0 = pl.multiple_of(c * tq, tq)
        k_aug = jnp.concatenate([ks_ref[pl.ds(k0, tq), :], eh_ref[pl.ds(k0, tq), :]], axis=1)
        v_aug = jnp.concatenate([vs_ref[pl.ds(k0, tq), :], ones_col], axis=1)
        s = lax.dot_general(q_aug, k_aug, (((1,), (1,)), ((), ())), preferred_element_type=F32)
        if diagonal:
            s = jnp.where(kpos_d <= row_d, s, -MASK_BIG)
        m_new = jnp.maximum(m_i, jnp.max(s, -1, keepdims=True))
        p = jnp.exp(s - m_new)
        acc = jnp.exp(m_i - m_new) * acc + _bdot(p, v_aug)
        return m_new, acc

    carry = lax.fori_loop(0, qi, functools.partial(sel_chunk, diagonal=False),
                          (jnp.full((rows4, 1), -MASK_BIG, F32), jnp.zeros((rows4, 2 * NSA_DH), F32)))
    _, acc_s = sel_chunk(qi, carry, True)
    o_s = acc_s[:, :NSA_DH] * (1.0 / jnp.maximum(acc_s[:, NSA_DH:NSA_DH + 1], 1e-30))

    gates = _sigmoid(sm_ref[...])
    outs = []
    for hh in range(hpg):
        base = SM_NGATE + (g * hpg + hh) * 3
        r = slice(hh * tq, (hh + 1) * tq)
        outs.append(_lane_col(gates, base) * o_c[r] + _lane_col(gates, base + 1) * o_s[r]
                    + _lane_col(gates, base + 2) * o_w[r])
    o_ref[...] = jnp.concatenate(outs, axis=1).astype(o_ref.dtype)


def _nsa(q_r, kvc, kv_r, small3):
    b, s, _ = q_r.shape
    nr = kvc.shape[2]
    nb = s // SEL_BLOCK
    tq = WINDOW
    assert s % tq == 0 and nb <= LANE and nr <= LANE
    gw = NSA_HPG * LANE
    eh = (jnp.arange(s)[:, None] // SEL_BLOCK == jnp.arange(LANE)[None, :]).astype(BF16)
    kvspec = lambda off: pl.BlockSpec((None, s, LANE), lambda i, g, j, off=off: (i, 0, off + g))
    cspec = lambda off: pl.BlockSpec((None, None, nr, NSA_DH), lambda i, g, j, off=off: (i, off + g, 0, 0))
    return pl.pallas_call(
        functools.partial(_nsa_kernel, tq=tq),
        grid=(b, NSA_KV_HEADS, s // tq),
        in_specs=[pl.BlockSpec((None, tq, gw), lambda i, g, j: (i, j, g)),
                  cspec(0), cspec(2),
                  kvspec(0), kvspec(2), kvspec(4), kvspec(6),
                  pl.BlockSpec((s, LANE), lambda i, g, j: (0, 0)),
                  pl.BlockSpec((None, tq, LANE), lambda i, g, j: (i, j, 0)),
                  pl.BlockSpec((LANE, LANE), lambda i, g, j: (0, 0))],
        out_specs=pl.BlockSpec((None, tq, gw), lambda i, g, j: (i, j, g)),
        out_shape=jax.ShapeDtypeStruct((b, s, NSA_HEADS * NSA_DH), BF16),
        compiler_params=_cp(("parallel", "parallel", "arbitrary")),
        name="nsa",
    )(q_r, kvc, kvc, kv_r, kv_r, kv_r, kv_r, eh, small3, _overlap_matrix(s).T)


def _merge_xattn_kernel(ya_ref, yb_ref, mg_ref, x_ref, wout_ref, g1_ref, b1_ref,
                        kv_ref, wq_ref, wo_ref, g_ref, b_ref, o_ref):
    mg = mg_ref[...].astype(F32)
    merged = (_sigmoid(mg[:, :D_MODEL]) * ya_ref[...].astype(F32)
              + _sigmoid(mg[:, D_MODEL:]) * yb_ref[...].astype(F32))
    h = _layer_norm(DN_ALPHA * x_ref[...] + _bdot(merged, wout_ref[...]), g1_ref[...], b1_ref[...])
    q = _bdot(h, wq_ref[...]).astype(BF16)
    kv = kv_ref[...]
    outs = []
    for hh in range(XA_HEADS):
        sl = slice(hh * XA_DH, (hh + 1) * XA_DH)
        s = _bdot_nt(q[:, sl], kv[:, sl]) * (XA_DH ** -0.5)
        m = jnp.max(s, -1, keepdims=True)
        e = jnp.exp(s - m)
        p = e / jnp.sum(e, -1, keepdims=True)
        outs.append(_bdot(p, kv[:, D_MODEL + hh * XA_DH:D_MODEL + (hh + 1) * XA_DH]))
    o = jnp.concatenate(outs, axis=1)
    o_ref[...] = _layer_norm(DN_ALPHA * h + _bdot(o, wo_ref[...]), g_ref[...], b_ref[...])


def _merge_xattn(y_a, y_b, proj3, x3, w_out, g1, b1, kv3, wq, wo, g2, b2, tm=512):
    bsz, s, _ = x3.shape
    m = kv3.shape[1]
    full = lambda shape: pl.BlockSpec(shape, lambda i, j: (0, 0))
    row = pl.BlockSpec((None, tm, D_MODEL), lambda i, j: (i, j, 0))
    return pl.pallas_call(
        _merge_xattn_kernel,
        grid=(bsz, s // tm),
        in_specs=[row, row,
                  pl.BlockSpec((None, tm, 2 * D_MODEL), lambda i, j: (i, j, CB_MG * LANE // (2 * D_MODEL))),
                  row, full((D_MODEL, D_MODEL)), full((1, D_MODEL)), full((1, D_MODEL)),
                  pl.BlockSpec((None, m, 2 * D_MODEL), lambda i, j: (i, 0, 0)),
                  full((D_MODEL, D_MODEL)), full((D_MODEL, D_MODEL)), full((1, D_MODEL)), full((1, D_MODEL))],
        out_specs=row,
        out_shape=jax.ShapeDtypeStruct((bsz, s, D_MODEL), F32),
        compiler_params=_cp(("parallel", "parallel")),
        name="merge_xattn",
    )(y_a, y_b, proj3, x3, w_out, g1, b1, kv3, wq, wo, g2, b2)


def _route(x, wr_h, wr_l, br):
    xh = x.astype(BF16)
    xl = (x - xh.astype(F32)).astype(BF16)
    logits = (jnp.dot(xh, wr_h, preferred_element_type=F32) + jnp.dot(xh, wr_l, preferred_element_type=F32)
              + jnp.dot(xl, wr_h, preferred_element_type=F32)) + br
    lane = lax.broadcasted_iota(jnp.int32, logits.shape, 1)
    gmask = lane < N_GROUPS
    gl = jnp.where(gmask, logits, -jnp.inf)
    gmax = jnp.max(gl, -1, keepdims=True)
    g_idx = jnp.min(jnp.where(gl == gmax, lane, LANE), -1, keepdims=True)
    p_top = 1.0 / jnp.sum(jnp.where(gmask, jnp.exp(gl - gmax), 0.0), -1, keepdims=True)
    e_lane = lane - N_GROUPS
    emask = (e_lane >= 0) & (e_lane < N_EXPERTS) & ((e_lane >> 3) == g_idx)
    el = jnp.where(emask, logits, -jnp.inf)
    emax = jnp.max(el, -1, keepdims=True)
    ee = jnp.where(emask, jnp.exp(el - emax), 0.0)
    pe = ee / jnp.sum(ee, -1, keepdims=True)
    pv = jnp.where(emask, pe, -jnp.inf)
    m1 = jnp.max(pv, -1, keepdims=True)
    i1 = jnp.min(jnp.where(pv == m1, lane, LANE), -1, keepdims=True)
    pv2 = jnp.where(lane == i1, -jnp.inf, pv)
    m2 = jnp.max(pv2, -1, keepdims=True)
    i2 = jnp.min(jnp.where(pv2 == m2, lane, LANE), -1, keepdims=True)
    denom = m1 + m2
    return i1, i2, p_top * m1 / denom, p_top * m2 / denom


META_E1, META_E2, META_R1, META_R2, META_W1, META_W2 = 0, 1, 2, 3, 4, 5


def _router_kernel(h_ref, wrh_ref, wrl_ref, br_ref, meta_ref, cnt_ref, carry_ref):
    @pl.when(pl.program_id(0) == 0)
    def _():
        carry_ref[...] = jnp.zeros_like(carry_ref)

    tm = h_ref.shape[0]
    i1, i2, w1, w2 = _route(h_ref[...], wrh_ref[...], wrl_ref[...], br_ref[...])
    lane = lax.broadcasted_iota(jnp.int32, (tm, LANE), 1)
    onehot = jnp.where((lane == i1) | (lane == i2), 1.0, 0.0)
    earlier = jnp.where(lax.broadcasted_iota(jnp.int32, (tm, tm), 1) < lax.broadcasted_iota(jnp.int32, (tm, tm), 0),
                        1.0, 0.0).astype(BF16)
    before = jnp.dot(earlier, onehot.astype(BF16), preferred_element_type=F32) + carry_ref[...]
    r1 = jnp.sum(jnp.where(lane == i1, before, 0.0), -1, keepdims=True)
    r2 = jnp.sum(jnp.where(lane == i2, before, 0.0), -1, keepdims=True)
    carry_ref[...] += jnp.sum(onehot, axis=0, keepdims=True)
    cnt_ref[...] = carry_ref[...]
    cols = ((META_E1, (i1 - N_GROUPS).astype(F32)), (META_E2, (i2 - N_GROUPS).astype(F32)),
            (META_R1, r1), (META_R2, r2), (META_W1, w1), (META_W2, w2))
    meta = jnp.zeros((tm, LANE), F32)
    for c, v in cols:
        meta = jnp.where(lane == c, v, meta)
    meta_ref[...] = meta


def _router(h2, wr_h, wr_l, br, tm=1024):
    n = h2.shape[0]
    full = lambda shape: pl.BlockSpec(shape, lambda i: (0, 0))
    return pl.pallas_call(
        _router_kernel,
        grid=(n // tm,),
        in_specs=[pl.BlockSpec((tm, D_MODEL), lambda i: (i, 0)),
                  full((D_MODEL, LANE)), full((D_MODEL, LANE)), full((1, LANE))],
        out_specs=[pl.BlockSpec((tm, LANE), lambda i: (i, 0)), full((1, LANE))],
        out_shape=[jax.ShapeDtypeStruct((n, LANE), F32), jax.ShapeDtypeStruct((1, LANE), F32)],
        scratch_shapes=[pltpu.VMEM((1, LANE), F32)],
        compiler_params=_cp(("arbitrary",)),
        name="moe_router",
    )(h2, wr_h, wr_l, br)


def _positions_kernel(meta_ref, cnt_ref, pos_ref):
    tm = meta_ref.shape[0]
    before = jnp.where(lax.broadcasted_iota(jnp.int32, (LANE, LANE), 0) < lax.broadcasted_iota(jnp.int32, (LANE, LANE), 1),
                       1.0, 0.0).astype(BF16)
    cnt8 = jnp.broadcast_to(cnt_ref[...], (8, LANE))
    start = sum(jnp.dot(p, before, preferred_element_type=F32) for p in _split3(cnt8))[0:1]
    meta = meta_ref[...]
    lane = lax.broadcasted_iota(jnp.int32, (tm, LANE), 1)
    pos = []
    for c_e, c_r in ((META_E1, META_R1), (META_E2, META_R2)):
        e_lane = meta[:, c_e:c_e + 1].astype(jnp.int32) + N_GROUPS
        seg = jnp.sum(jnp.where(lane == e_lane, start, 0.0), -1, keepdims=True)
        pos.append((seg + meta[:, c_r:c_r + 1]).astype(jnp.int32))
    pos_ref[...] = jnp.where(lane == 0, pos[0], jnp.where(lane == 1, pos[1], 0))


def _positions(meta, counts, tm=2048):
    n = meta.shape[0]
    return pl.pallas_call(
        _positions_kernel,
        grid=(n // tm,),
        in_specs=[pl.BlockSpec((tm, LANE), lambda i: (i, 0)), pl.BlockSpec((1, LANE), lambda i: (0, 0))],
        out_specs=pl.BlockSpec((tm, LANE), lambda i: (i, 0)),
        out_shape=jax.ShapeDtypeStruct((n, LANE), jnp.int32),
        compiler_params=_cp(("parallel",)),
        name="moe_positions",
    )(meta, counts)


def _row_copies(n_rows, pos_refs, make_copy):
    def body(r, carry):
        for p in pos_refs:
            make_copy(r, p[r]).start()
        return carry

    lax.fori_loop(0, n_rows, body, 0, unroll=8)


def _dispatch_kernel(pos1_ref, pos2_ref, h_ref, xs_ref, sem):
    tm = h_ref.shape[0]
    _row_copies(tm, (pos1_ref, pos2_ref),
                lambda r, p: pltpu.make_async_copy(h_ref.at[pl.ds(r, 1)], xs_ref.at[pl.ds(p, 1)], sem))
    for _ in range(2):
        pltpu.make_async_copy(h_ref, xs_ref.at[pl.ds(0, tm)], sem).wait()


def _dispatch(h2, pos1, pos2, n_rows, tm=512):
    n = h2.shape[0]
    smem = pl.BlockSpec((tm,), lambda i: (i,), memory_space=pltpu.SMEM)
    return pl.pallas_call(
        _dispatch_kernel,
        grid=(n // tm,),
        in_specs=[smem, smem, pl.BlockSpec((tm, D_MODEL), lambda i: (i, 0))],
        out_specs=pl.BlockSpec(memory_space=pl.ANY),
        out_shape=jax.ShapeDtypeStruct((n_rows, D_MODEL), F32),
        scratch_shapes=[pltpu.SemaphoreType.DMA],
        compiler_params=_cp(("arbitrary",)),
        name="moe_dispatch",
    )(pos1, pos2, h2)


def _ffn_kernel(tile_ref, exp_ref, lo_ref, hi_ref, first_ref, xs_ref, wg_ref, wu_ref, wd_ref, ys_ref):
    i = pl.program_id(0)
    tm = xs_ref.shape[0]
    rows = lax.broadcasted_iota(jnp.int32, (tm, 1), 0)
    mine = (rows >= lo_ref[i]) & (rows < hi_ref[i])
    x = xs_ref[...].astype(BF16)
    gate = jnp.dot(x, wg_ref[...].astype(BF16), preferred_element_type=F32)
    up = jnp.dot(x, wu_ref[...].astype(BF16), preferred_element_type=F32)
    y = _bdot(_silu(gate) * up, wd_ref[...])

    @pl.when(first_ref[i] == 1)
    def _():
        ys_ref[...] = jnp.where(mine, y, 0.0)

    @pl.when(first_ref[i] == 0)
    def _():
        ys_ref[...] = jnp.where(mine, y, ys_ref[...])


def _ffn(xs, items, wg, wu, wd, tm):
    n_rows = xs.shape[0]
    wspec = lambda shape: pl.BlockSpec((None,) + shape, lambda i, tl, ex, lo, hi, fi: (ex[i], 0, 0))
    row_tile = pl.BlockSpec((tm, D_MODEL), lambda i, tl, ex, lo, hi, fi: (tl[i], 0))
    return pl.pallas_call(
        _ffn_kernel,
        grid_spec=pltpu.PrefetchScalarGridSpec(
            num_scalar_prefetch=5,
            grid=(items[0].shape[0],),
            in_specs=[row_tile, wspec((D_MODEL, D_FF)), wspec((D_MODEL, D_FF)), wspec((D_FF, D_MODEL))],
            out_specs=row_tile),
        out_shape=jax.ShapeDtypeStruct((n_rows, D_MODEL), F32),
        compiler_params=_cp(("arbitrary",)),
        name="moe_ffn",
    )(*items, xs, wg, wu, wd)


def _combine_kernel(pos1_ref, pos2_ref, nxt1_ref, nxt2_ref, h_ref, meta_ref, ys_ref, g_ref, b_ref, o_ref,
                    buf_ref, sem):
    i = pl.program_id(0)
    tm = h_ref.shape[0]
    slot = i % 2

    def start_gather(p1_ref, p2_ref, s):
        def copy(k, pos_ref):
            return lambda r, p: pltpu.make_async_copy(ys_ref.at[pl.ds(p, 1)], buf_ref.at[s, k, pl.ds(r, 1)],
                                                      sem.at[s])
        def body(r, carry):
            copy(0, p1_ref)(r, p1_ref[r]).start()
            copy(1, p2_ref)(r, p2_ref[r]).start()
            return carry
        lax.fori_loop(0, tm, body, 0, unroll=8)

    @pl.when(i == 0)
    def _():
        start_gather(pos1_ref, pos2_ref, 0)

    @pl.when(i + 1 < pl.num_programs(0))
    def _():
        start_gather(nxt1_ref, nxt2_ref, 1 - slot)

    for k in range(2):
        pltpu.make_async_copy(ys_ref.at[pl.ds(0, tm)], buf_ref.at[slot, k], sem.at[slot]).wait()
    meta = meta_ref[...]
    ffn = (meta[:, META_W1:META_W1 + 1] * buf_ref[slot, 0] + meta[:, META_W2:META_W2 + 1] * buf_ref[slot, 1])
    o_ref[...] = _layer_norm(DN_ALPHA * h_ref[...] + ffn, g_ref[...], b_ref[...])


def _combine(h2, meta, ys, pos1, pos2, g, b, tm=256):
    n = h2.shape[0]
    last = n // tm - 1
    smem = pl.BlockSpec((tm,), lambda i: (i,), memory_space=pltpu.SMEM)
    smem_next = pl.BlockSpec((tm,), lambda i: (jnp.minimum(i + 1, last),), memory_space=pltpu.SMEM)
    full = lambda shape: pl.BlockSpec(shape, lambda i: (0, 0))
    return pl.pallas_call(
        _combine_kernel,
        grid=(n // tm,),
        in_specs=[smem, smem, smem_next, smem_next, pl.BlockSpec((tm, D_MODEL), lambda i: (i, 0)),
                  pl.BlockSpec((tm, LANE), lambda i: (i, 0)),
                  pl.BlockSpec(memory_space=pl.ANY), full((1, D_MODEL)), full((1, D_MODEL))],
        out_specs=pl.BlockSpec((tm, D_MODEL), lambda i: (i, 0)),
        out_shape=jax.ShapeDtypeStruct((n, D_MODEL), F32),
        scratch_shapes=[pltpu.VMEM((2, 2, tm, D_MODEL), F32), pltpu.SemaphoreType.DMA((2,))],
        compiler_params=_cp(("arbitrary",)),
        name="moe_combine_ln3",
    )(pos1, pos2, pos1, pos2, h2, meta, ys, g, b)


def _moe(h2, wr_h, wr_l, br, wg, wu, wd, g, b, tile=512):
    n = h2.shape[0]
    i32 = jnp.int32
    meta, counts = _router(h2, wr_h, wr_l, br)
    cnt = counts[0, N_GROUPS:N_GROUPS + N_EXPERTS].astype(i32)
    seg_end = jnp.cumsum(cnt)
    seg_start = seg_end - cnt
    pos = _positions(meta, counts)
    pos1, pos2 = pos[:, 0], pos[:, 1]
    first_tile = seg_start // tile
    n_items_e = jnp.where(cnt > 0, (seg_end - 1) // tile - first_tile + 1, 0)
    item_end = jnp.cumsum(n_items_e)
    n_items = (2 * n) // tile + N_EXPERTS - 1
    idx = jnp.minimum(jnp.arange(n_items, dtype=i32), item_end[-1] - 1)
    exp = jnp.sum(idx[:, None] >= item_end[None, :], axis=1).astype(i32)
    til = first_tile[exp] + idx - (item_end - n_items_e)[exp]
    lo = jnp.maximum(seg_start[exp], til * tile) - til * tile
    hi = jnp.minimum(seg_end[exp], (til + 1) * tile) - til * tile
    repeat = jnp.arange(n_items, dtype=i32) >= item_end[-1]
    hi = jnp.where(repeat, lo, hi)
    first = jnp.concatenate([jnp.ones((1,), i32), (til[1:] != til[:-1]).astype(i32)])
    xs = _dispatch(h2, pos1, pos2, 2 * n)
    ys = _ffn(xs, (til.astype(i32), exp, lo.astype(i32), hi.astype(i32), first), wg, wu, wd, tile)
    return _combine(h2, meta, ys, pos1, pos2, g, b)


def _regroup_w_in(w):
    sizes = (1024, 1024, 1024, 1024, 8, 8, 1024, 256, 256, 256, 256, 256, 256, 24, 2048)
    offs = [0]
    for sz in sizes:
        offs.append(offs[-1] + sz)
    seg = lambda i: w[:, offs[i]:offs[i + 1]]
    small = jnp.concatenate([seg(4), seg(5), seg(13)], axis=1)
    small = jnp.pad(small, ((0, 0), (0, LANE - small.shape[1])))
    big = jnp.concatenate([seg(0), seg(1), seg(2), seg(3), seg(14), seg(6), seg(7), seg(8), seg(9), seg(10),
                           seg(11), seg(12)], axis=1)
    return big.astype(BF16), small.astype(BF16)


def _overlap_matrix(s):
    nb = s // SEL_BLOCK
    c0 = jnp.arange(LANE) * CMP_STRIDE
    s0 = jnp.arange(LANE) * SEL_BLOCK
    ov = jnp.minimum(c0[:, None] + CMP_BLOCK, s0[None, :] + SEL_BLOCK) - jnp.maximum(c0[:, None], s0[None, :])
    ov = jnp.maximum(ov, 0).astype(F32) / CMP_BLOCK
    nc = (s - CMP_BLOCK) // CMP_STRIDE + 1
    keep = (jnp.arange(LANE)[:, None] < nc) & (jnp.arange(LANE)[None, :] < nb)
    return jnp.where(keep, ov, 0.0).astype(BF16)


def _layer(h, mem, cos2, sin2, w_in, conv_w, a_log, dt_bias, norm_w, cmp_pe, cmp_w1, cmp_b1, cmp_w2, w_out,
           ln1_g, ln1_b, xa_wq, xa_wkv, xa_wo, ln2_g, ln2_b, w_group, b_group, w_expert, b_expert,
           w_gate, w_up, w_down, ln3_g, ln3_b):
    b, s, d = h.shape
    n = b * s
    x2 = h.reshape(n, d)
    proj, small = _inproj(x2, *_regroup_w_in(w_in))
    proj3 = proj.reshape(b, s, -1)
    small3 = small.reshape(b, s, LANE)

    y_a = _gdn(proj3, small3, conv_w, a_log, dt_bias, norm_w)

    q_r, ckv_r, kv_r = _rope(proj3, cos2, sin2)
    kvc = _compress(ckv_r, cmp_pe.reshape(2, 1, CMP_BLOCK * NSA_DH).astype(BF16), cmp_w1.astype(BF16),
                    cmp_b1.reshape(2, 1, CMP_HIDDEN), cmp_w2.astype(BF16))
    y_b = _nsa(q_r, kvc, kv_r, small3)

    row = lambda v: v.reshape(1, -1)
    m = mem.shape[1]
    kv = _matmul(mem.reshape(b * m, d).astype(BF16), xa_wkv.astype(BF16), BF16, tm=512, tn=512)
    h2 = _merge_xattn(y_a, y_b, proj3, h, w_out.astype(BF16), row(ln1_g), row(ln1_b),
                      kv.reshape(b, m, 2 * d), xa_wq.astype(BF16), xa_wo.astype(BF16), row(ln2_g), row(ln2_b))

    wr = jnp.pad(jnp.concatenate([w_group, w_expert], axis=1), ((0, 0), (0, LANE - N_GROUPS - N_EXPERTS)))
    wr_h = wr.astype(BF16)
    wr_l = (wr - wr_h.astype(F32)).astype(BF16)
    br = jnp.pad(jnp.concatenate([b_group, b_expert]), (0, LANE - N_GROUPS - N_EXPERTS)).reshape(1, LANE)
    h3 = _moe(h2.reshape(n, d), wr_h, wr_l, br, w_gate, w_up, w_down, row(ln3_g), row(ln3_b))
    return h3.reshape(b, s, d)


def kernel(x, mem, positions, w_in, gdn_conv_w, gdn_a_log, gdn_dt_bias, gdn_norm_w, cmp_pe, cmp_w1, cmp_b1, cmp_w2, w_out, ln1_g, ln1_b, xa_wq, xa_wkv, xa_wo, ln2_g, ln2_b, moe_w_group, moe_b_group, moe_w_expert, moe_b_expert, moe_w_gate, moe_w_up, moe_w_down, ln3_g, ln3_b):
    half = NSA_DH // 2
    inv_freq = ROPE_THETA ** (-jnp.arange(half, dtype=F32) / half)
    ang = positions.astype(F32)[..., None] * inv_freq
    cos, sin = jnp.cos(ang), jnp.sin(ang)
    cos2 = jnp.concatenate([cos, cos], -1)
    sin2 = jnp.concatenate([-sin, sin], -1)
    h = x
    for l in range(DEPTH):
        h = _layer(h, mem, cos2, sin2, w_in[l], gdn_conv_w[l], gdn_a_log[l], gdn_dt_bias[l], gdn_norm_w[l],
                   cmp_pe[l], cmp_w1[l], cmp_b1[l], cmp_w2[l], w_out[l], ln1_g[l], ln1_b[l],
                   xa_wq[l], xa_wkv[l], xa_wo[l], ln2_g[l], ln2_b[l], moe_w_group[l], moe_b_group[l],
                   moe_w_expert[l], moe_b_expert[l], moe_w_gate[l], moe_w_up[l], moe_w_down[l],
                   ln3_g[l], ln3_b[l])
    return h
```

```python
import functools

import jax
import jax.numpy as jnp
import numpy as np
from jax import lax
from jax.experimental import pallas as pl
from jax.experimental.pallas import tpu as pltpu

F32 = jnp.float32
BF16 = jnp.bfloat16

D_MODEL = 1024
LANE = 128
GDN_HEADS = 8
GDN_D = 128
GDN_CONV = 4
GDN_CHUNK = 64
NSA_HEADS = 8
NSA_KV_HEADS = 2
NSA_HPG = NSA_HEADS // NSA_KV_HEADS
NSA_DH = 128
CMP_BLOCK = 32
CMP_STRIDE = 16
CMP_HIDDEN = 256
SEL_BLOCK = 64
SEL_SHIFT = 6
SEL_TOPK = 8
WINDOW = 256
XA_HEADS = 4
XA_DH = 256
N_GROUPS = 4
EXPERTS_PER_GROUP = 8
N_EXPERTS = 32
D_FF = 256
DEPTH = 1
DN_ALPHA = (2.0 * DEPTH) ** 0.25
LN_EPS = 1e-5
RMS_EPS = 1e-6
ROPE_THETA = 10000.0
NEG = -1e30
MASK_BIG = 1e30

CB_Q, CB_K, CB_V, CB_Z = 0, 8, 16, 24
CB_MG = 32
CB_NQ = 48
CB_KV = 56
SM_BETA, SM_DECAY, SM_NGATE = 0, 8, 16

VMEM_LIMIT = 48 * 1024 * 1024


def _cp(sem, vmem=VMEM_LIMIT):
    return pltpu.CompilerParams(dimension_semantics=sem, vmem_limit_bytes=vmem)


def _bdot(a, b):
    return jnp.dot(a.astype(BF16), b.astype(BF16), preferred_element_type=F32)


def _bdot_nt(a, b):
    return lax.dot_general(a.astype(BF16), b.astype(BF16), (((1,), (1,)), ((), ())),
                           preferred_element_type=F32)


def _bdot_tn(a, b):
    return lax.dot_general(a.astype(BF16), b.astype(BF16), (((0,), (0,)), ((), ())),
                           preferred_element_type=F32)


def _split3(x):
    h = x.astype(BF16)
    r = x - h.astype(F32)
    m = r.astype(BF16)
    l = (r - m.astype(F32)).astype(BF16)
    return h, m, l


def _sigmoid(x):
    return 0.5 * jnp.tanh(0.5 * x) + 0.5


def _silu(x):
    return x * _sigmoid(x)


def _layer_norm(x, g, b):
    mu = jnp.mean(x, -1, keepdims=True)
    xc = x - mu
    var = jnp.mean(xc * xc, -1, keepdims=True)
    return xc * lax.rsqrt(var + LN_EPS) * g + b


def _lane_col(x, c):
    lane = lax.broadcasted_iota(jnp.int32, x.shape, 1)
    return jnp.sum(jnp.where(lane == c, x, 0.0), axis=-1, keepdims=True)


def _mm_kernel(x_ref, w_ref, o_ref):
    o_ref[...] = jnp.dot(x_ref[...], w_ref[...], preferred_element_type=F32).astype(o_ref.dtype)


def _matmul(x, w, out_dtype, tm, tn):
    m, k = x.shape
    n = w.shape[1]
    return pl.pallas_call(
        _mm_kernel,
        grid=(m // tm, n // tn),
        in_specs=[pl.BlockSpec((tm, k), lambda i, j: (i, 0)),
                  pl.BlockSpec((k, tn), lambda i, j: (0, j))],
        out_specs=pl.BlockSpec((tm, tn), lambda i, j: (i, j)),
        out_shape=jax.ShapeDtypeStruct((m, n), out_dtype),
        compiler_params=_cp(("parallel", "parallel")),
        name="matmul",
    )(x, w)


def _inproj_kernel(x_ref, w_ref, ws_ref, o_ref, os_ref, xb_ref):
    @pl.when(pl.program_id(1) == 0)
    def _():
        xb_ref[...] = x_ref[...].astype(BF16)
        os_ref[...] = jnp.dot(xb_ref[...], ws_ref[...], preferred_element_type=F32)

    o_ref[...] = jnp.dot(xb_ref[...], w_ref[...], preferred_element_type=F32).astype(o_ref.dtype)


def _inproj(x2, w_big, w_small, tm=2048, tn=512):
    m, k = x2.shape
    n = w_big.shape[1]
    return pl.pallas_call(
        _inproj_kernel,
        grid=(m // tm, n // tn),
        in_specs=[pl.BlockSpec((tm, k), lambda i, j: (i, 0)),
                  pl.BlockSpec((k, tn), lambda i, j: (0, j)),
                  pl.BlockSpec((k, LANE), lambda i, j: (0, 0))],
        out_specs=[pl.BlockSpec((tm, tn), lambda i, j: (i, j)),
                   pl.BlockSpec((tm, LANE), lambda i, j: (i, 0))],
        out_shape=[jax.ShapeDtypeStruct((m, n), BF16), jax.ShapeDtypeStruct((m, LANE), F32)],
        scratch_shapes=[pltpu.VMEM((tm, k), BF16)],
        compiler_params=_cp(("parallel", "arbitrary")),
        name="inproj",
    )(x2, w_big, w_small)


def _gdn_kernel(q_ref, k_ref, v_ref, z_ref, sm_ref, cwq_ref, cwk_ref, cwv_ref,
                alog_ref, dtb_ref, nw_ref, o_ref, state_ref, tail_ref):
    c_len = q_ref.shape[0]

    @pl.when(pl.program_id(1) == 0)
    def _():
        state_ref[...] = jnp.zeros_like(state_ref)
        tail_ref[...] = jnp.zeros_like(tail_ref)

    row = lax.broadcasted_iota(jnp.int32, (c_len, c_len), 0)
    col = lax.broadcasted_iota(jnp.int32, (c_len, c_len), 1)
    causal = row >= col
    strict = row > col
    eye = jnp.where(row == col, 1.0, 0.0)

    def conv_silu(idx, x_ref, w_ref):
        cur = x_ref[...].astype(F32)
        xc = jnp.concatenate([tail_ref[idx], cur], axis=0)
        tail_ref[idx] = cur[c_len - 8:, :]
        w = w_ref[...]
        acc = xc[8:, :] * w[GDN_CONV - 1:GDN_CONV, :]
        for j in range(GDN_CONV - 1):
            off = 8 - (GDN_CONV - 1) + j
            acc = acc + xc[off:off + c_len, :] * w[j:j + 1, :]
        return _silu(acc)

    q_all = conv_silu(0, q_ref, cwq_ref)
    k_all = conv_silu(1, k_ref, cwk_ref)
    v_all = conv_silu(2, v_ref, cwv_ref)

    sm = sm_ref[...]
    beta_all = _sigmoid(sm)
    a_in = sm + dtb_ref[...]
    softplus = jnp.maximum(a_in, 0.0) + jnp.log(1.0 + jnp.exp(-jnp.abs(a_in)))
    g_all = -jnp.exp(alog_ref[...]) * softplus
    gc_all = g_all
    shift = 1
    while shift < c_len:
        gc_all = gc_all + jnp.concatenate([jnp.zeros((shift, LANE), F32), gc_all[:c_len - shift, :]], axis=0)
        shift *= 2
    gc_t = gc_all.T

    heads = range(GDN_HEADS)
    hs = lambda x, h: x[:, h * GDN_D:(h + 1) * GDN_D]
    q = [hs(q_all, h) for h in heads]
    k = [hs(k_all, h) for h in heads]
    v = [hs(v_all, h) for h in heads]
    q = [x * (lax.rsqrt(jnp.sum(x * x, -1, keepdims=True) + RMS_EPS) * (GDN_D ** -0.5)) for x in q]
    k = [x * lax.rsqrt(jnp.sum(x * x, -1, keepdims=True) + RMS_EPS) for x in k]
    beta = [beta_all[:, SM_BETA + h:SM_BETA + h + 1] for h in heads]
    gc = [gc_all[:, SM_DECAY + h:SM_DECAY + h + 1] for h in heads]
    g_last = [x[c_len - 1:c_len, :] for x in gc]
    diff = [gc[h] - gc_t[SM_DECAY + h:SM_DECAY + h + 1, :] for h in heads]
    decay = [jnp.where(causal, jnp.exp(d), 0.0) for d in diff]
    kk = [_bdot_nt(x, x) for x in k]
    m_pow = [-(jnp.where(strict, kk[h] * decay[h], 0.0) * beta[h]) for h in heads]
    t_inv = [eye + m for m in m_pow]
    for _ in range((c_len - 1).bit_length() - 1):
        m_pow = [_bdot(m, m) for m in m_pow]
        t_inv = [t + _bdot(t, m) for t, m in zip(t_inv, m_pow)]
    e_gc = [jnp.exp(x) for x in gc]
    u = [_bdot(t_inv[h], v[h] * beta[h]) for h in heads]
    w = [_bdot(t_inv[h], k[h] * (beta[h] * e_gc[h])) for h in heads]
    qk = [_bdot_nt(q[h], k[h]) * decay[h] for h in heads]
    q_dec = [q[h] * e_gc[h] for h in heads]
    k_dec = [k[h] * jnp.exp(g_last[h] - gc[h]) for h in heads]
    state = [state_ref[h] for h in heads]
    v_new = [u[h] - _bdot(w[h], state[h]) for h in heads]
    o = [_bdot(q_dec[h], state[h]) + _bdot(qk[h], v_new[h]) for h in heads]
    for h in heads:
        state_ref[h] = state[h] * jnp.exp(g_last[h]) + _bdot_tn(k_dec[h], v_new[h])
    nw = nw_ref[...]
    o = [x * lax.rsqrt(jnp.mean(x * x, -1, keepdims=True) + RMS_EPS) * nw for x in o]
    o_ref[...] = (jnp.concatenate(o, axis=1) * _silu(z_ref[...].astype(F32))).astype(o_ref.dtype)


def _gdn(proj3, small3, conv_w, a_log, dt_bias, norm_w, c_len=128):
    b, s, _ = proj3.shape
    width = GDN_HEADS * GDN_D
    col = lambda off: pl.BlockSpec((None, c_len, width), lambda i, j, off=off: (i, j, off // GDN_HEADS))
    cw = lambda k: pl.BlockSpec((GDN_CONV, width), lambda i, j, k=k: (0, k))
    full = lambda shape: pl.BlockSpec(shape, lambda i, j: (0, 0))
    pad = lambda v: jnp.pad(v, (SM_DECAY, LANE - SM_DECAY - GDN_HEADS)).reshape(1, LANE)
    return pl.pallas_call(
        _gdn_kernel,
        grid=(b, s // c_len),
        in_specs=[col(CB_Q), col(CB_K), col(CB_V), col(CB_Z),
                  pl.BlockSpec((None, c_len, LANE), lambda i, j: (i, j, 0)),
                  cw(0), cw(1), cw(2),
                  full((1, LANE)), full((1, LANE)), full((1, GDN_D))],
        out_specs=pl.BlockSpec((None, c_len, width), lambda i, j: (i, j, 0)),
        out_shape=jax.ShapeDtypeStruct((b, s, width), BF16),
        scratch_shapes=[pltpu.VMEM((GDN_HEADS, GDN_D, GDN_D), F32), pltpu.VMEM((3, 8, width), F32)],
        compiler_params=_cp(("parallel", "arbitrary")),
        name="gdn",
    )(proj3, proj3, proj3, proj3, small3, conv_w, conv_w, conv_w,
      pad(a_log), pad(dt_bias), norm_w.reshape(1, GDN_D))


def _rope_kernel(nq_ref, kv0_ref, kv1_ref, kv2_ref, pos_ref, freq_ref, q_out, ckv_out, kv_out, tmp_ref):
    ang = pos_ref[...] * freq_ref[...]
    cos2 = jnp.cos(ang)
    sin = jnp.sin(ang)
    sin2 = jnp.where(lax.broadcasted_iota(jnp.int32, ang.shape, 1) < NSA_DH // 2, -sin, sin)

    def rope(x):
        x = x.astype(F32)
        return x * cos2 + pltpu.roll(x, NSA_DH // 2, 1) * sin2

    scale = NSA_DH ** -0.5
    q_out[...] = jnp.concatenate(
        [rope(nq_ref[:, hh * LANE:(hh + 1) * LANE]) * scale for hh in range(NSA_HEADS)],
        axis=1).astype(q_out.dtype)
    kv_refs = (kv0_ref, kv1_ref, kv2_ref)
    blk = lambda i: kv_refs[i // 4][:, (i % 4) * LANE:(i % 4 + 1) * LANE]
    n_out = tmp_ref.shape[0] // CMP_STRIDE
    for slot, val in enumerate((rope(blk(0)), rope(blk(1)), blk(2).astype(F32), blk(3).astype(F32))):
        tmp_ref[...] = val
        for l in range(CMP_STRIDE):
            ckv_out[slot, :, l * LANE:(l + 1) * LANE] = tmp_ref[pl.ds(l, n_out, stride=CMP_STRIDE), :].astype(
                ckv_out.dtype)
    kv_out[...] = jnp.concatenate(
        [rope(blk(4)), rope(blk(5)), blk(6), blk(7), rope(blk(8)), rope(blk(9)), blk(10), blk(11)],
        axis=1).astype(kv_out.dtype)


def _rope(proj3, pos3, freq2, ts=512):
    b, s, _ = proj3.shape
    kvspec = lambda k: pl.BlockSpec((None, ts, 4 * LANE), lambda i, j, k=k: (i, j, CB_KV // 4 + k))
    return pl.pallas_call(
        _rope_kernel,
        grid=(b, s // ts),
        in_specs=[pl.BlockSpec((None, ts, NSA_HEADS * LANE), lambda i, j: (i, j, CB_NQ // NSA_HEADS)),
                  kvspec(0), kvspec(1), kvspec(2),
                  pl.BlockSpec((None, ts, 1), lambda i, j: (i, j, 0)),
                  pl.BlockSpec((1, NSA_DH), lambda i, j: (0, 0))],
        out_specs=[pl.BlockSpec((None, ts, NSA_HEADS * LANE), lambda i, j: (i, j, 0)),
                   pl.BlockSpec((None, 4, ts // CMP_STRIDE, CMP_STRIDE * LANE), lambda i, j: (i, 0, j, 0)),
                   pl.BlockSpec((None, ts, 8 * LANE), lambda i, j: (i, j, 0))],
        out_shape=[jax.ShapeDtypeStruct((b, s, NSA_HEADS * LANE), BF16),
                   jax.ShapeDtypeStruct((b, 4, s // CMP_STRIDE, CMP_STRIDE * LANE), BF16),
                   jax.ShapeDtypeStruct((b, s, 8 * LANE), BF16)],
        scratch_shapes=[pltpu.VMEM((ts, LANE), F32)],
        compiler_params=_cp(("parallel", "parallel")),
        name="rope",
    )(proj3, proj3, proj3, proj3, pos3, freq2)


def _compress_kernel(x_ref, pe_ref, w1_ref, b1_ref, w2_ref, o_ref):
    x = x_ref[...]
    half = CMP_STRIDE * NSA_DH
    first = jnp.dot(x, w1_ref[:half, :], preferred_element_type=F32)
    second = jnp.dot(x, w1_ref[half:, :], preferred_element_type=F32)
    n_rows = x.shape[0]
    second = pltpu.roll(second, n_rows - 1, 0)
    pe = jnp.broadcast_to(pe_ref[...], (8, pe_ref.shape[1]))
    bias = jnp.dot(pe, w1_ref[...], preferred_element_type=F32)[0:1] + b1_ref[...]
    hid = first + second + bias
    gelu = 0.5 * hid * (1.0 + jnp.tanh(0.7978845608028654 * (hid + 0.044715 * hid * hid * hid)))
    o_ref[...] = _bdot(gelu, w2_ref[...]).astype(o_ref.dtype)


def _compress(ckv_r, pe, w1, b1, w2):
    b, _, nr, wd = ckv_r.shape
    return pl.pallas_call(
        _compress_kernel,
        grid=(b, 4),
        in_specs=[pl.BlockSpec((None, None, nr, wd), lambda i, j: (i, j, 0, 0)),
                  pl.BlockSpec((None, 1, CMP_BLOCK * NSA_DH), lambda i, j: (j // 2, 0, 0)),
                  pl.BlockSpec((None, CMP_BLOCK * NSA_DH, CMP_HIDDEN), lambda i, j: (j // 2, 0, 0)),
                  pl.BlockSpec((None, 1, CMP_HIDDEN), lambda i, j: (j // 2, 0, 0)),
                  pl.BlockSpec((None, CMP_HIDDEN, NSA_DH), lambda i, j: (j // 2, 0, 0))],
        out_specs=pl.BlockSpec((None, None, nr, NSA_DH), lambda i, j: (i, j, 0, 0)),
        out_shape=jax.ShapeDtypeStruct((b, 4, nr, NSA_DH), BF16),
        compiler_params=_cp(("parallel", "arbitrary")),
        name="compress",
    )(ckv_r, pe, w1, b1, w2)


def _nsa_kernel(q_ref, kc_ref, vc_ref, ks_ref, vs_ref, kw_ref, vw_ref, eh_ref, sm_ref, ovt_ref, o_ref, *, tq):
    g = pl.program_id(1)
    qi = pl.program_id(2)
    t0 = qi * tq
    heads = range(NSA_HPG)
    nb = eh_ref.shape[0] // SEL_BLOCK
    qs = [q_ref[:, hh * LANE:(hh + 1) * LANE] for hh in heads]

    cmask = (lax.broadcasted_iota(jnp.int32, (tq, LANE), 1) * CMP_STRIDE + (CMP_BLOCK - 1)
             <= t0 + lax.broadcasted_iota(jnp.int32, (tq, LANE), 0))
    kc = kc_ref[...]
    s_c = [jnp.where(cmask, _bdot_nt(x, kc), NEG) for x in qs]
    e_c = [jnp.where(cmask, jnp.exp(s - jnp.max(s, -1, keepdims=True)), 0.0) for s in s_c]
    p_c = [e * (1.0 / jnp.maximum(jnp.sum(e, -1, keepdims=True), 1e-30)) for e in e_c]
    vc = vc_ref[...]
    o_c = [_bdot(p, vc) for p in p_c]

    p_sum = p_c[0]
    for p in p_c[1:]:
        p_sum = p_sum + p
    ovt = ovt_ref[...]
    imp_t = sum(lax.dot_general(ovt, part, (((1,), (1,)), ((), ())), preferred_element_type=F32)
                for part in _split3(p_sum))[:nb]
    blk = lax.broadcasted_iota(jnp.int32, (nb, tq), 0)
    t = t0 + lax.broadcasted_iota(jnp.int32, (nb, tq), 1)
    cur = t >> SEL_SHIFT
    forced = (blk == 0) | (blk == cur) | (blk == cur - 1)
    val = jnp.where(forced, jnp.inf, jnp.where(blk * SEL_BLOCK <= t, imp_t, -jnp.inf))
    rank = jnp.zeros((nb, tq), F32)
    for i in range(nb):
        vi = val[i:i + 1, :]
        rank = rank + jnp.where((vi > val) | ((vi == val) & (blk > i)), 1.0, 0.0)
    drop_t = jnp.where(rank < SEL_TOPK, 0.0, MASK_BIG)
    drop = jnp.concatenate([drop_t, jnp.zeros((LANE - nb, tq), F32)], axis=0).T
    drop_b = (-drop).astype(BF16)
    q_aug = [jnp.concatenate([x, drop_b], axis=1) for x in qs]

    ones_col = jnp.where(lax.broadcasted_iota(jnp.int32, (tq, LANE), 1) == 0, 1.0, 0.0).astype(BF16)
    kpos_d = lax.broadcasted_iota(jnp.int32, (tq, tq), 1)
    row_d = lax.broadcasted_iota(jnp.int32, (tq, tq), 0)
    normalise = lambda acc: acc[:, :NSA_DH] * (1.0 / jnp.maximum(acc[:, NSA_DH:NSA_DH + 1], 1e-30))

    prev0 = pl.multiple_of(jnp.maximum(qi - 1, 0) * tq, tq)
    own0 = pl.multiple_of(t0, tq)
    in_prev = kpos_d > row_d + jnp.where(qi > 0, 0, tq)
    in_own = kpos_d <= row_d
    kw_prev, kw_own = kw_ref[pl.ds(prev0, tq), :], kw_ref[pl.ds(own0, tq), :]
    v_w = jnp.concatenate([jnp.concatenate([vw_ref[pl.ds(prev0, tq), :], ones_col], axis=1),
                           jnp.concatenate([vw_ref[pl.ds(own0, tq), :], ones_col], axis=1)], axis=0)
    s_w = [jnp.concatenate([jnp.where(in_prev, _bdot_nt(x, kw_prev), NEG),
                            jnp.where(in_own, _bdot_nt(x, kw_own), NEG)], axis=1) for x in qs]
    e_w = [jnp.exp(s - jnp.max(s, -1, keepdims=True)) for s in s_w]
    o_w = [normalise(_bdot(e, v_w)) for e in e_w]

    def sel_chunk(c, carry, diagonal):
        m_i, acc = carry
        k0 = pl.multiple_of(c * tq, tq)
        k_aug = jnp.concatenate([ks_ref[pl.ds(k0, tq), :], eh_ref[pl.ds(k0, tq), :]], axis=1)
        v_aug = jnp.concatenate([vs_ref[pl.ds(k0, tq), :], ones_col], axis=1)
        s = [lax.dot_general(x, k_aug, (((1,), (1,)), ((), ())), preferred_element_type=F32) for x in q_aug]
        if diagonal:
            s = [jnp.where(in_own, x, -MASK_BIG) for x in s]
        m_new = [jnp.maximum(m, jnp.max(x, -1, keepdims=True)) for m, x in zip(m_i, s)]
        p = [jnp.exp(x - m) for x, m in zip(s, m_new)]
        acc = [jnp.exp(m - mn) * a + _bdot(x, v_aug) for m, mn, a, x in zip(m_i, m_new, acc, p)]
        return tuple(m_new), tuple(acc)

    init = (tuple(jnp.full((tq, 1), -MASK_BIG, F32) for _ in heads),
            tuple(jnp.zeros((tq, 2 * NSA_DH), F32) for _ in heads))
    carry = lax.fori_loop(0, qi, functools.partial(sel_chunk, diagonal=False), init)
    _, acc_s = sel_chunk(qi, carry, True)
    o_s = [normalise(a) for a in acc_s]

    gates = _sigmoid(sm_ref[...])
    outs = []
    for hh in heads:
        base = SM_NGATE + (g * NSA_HPG + hh) * 3
        outs.append(_lane_col(gates, base) * o_c[hh] + _lane_col(gates, base + 1) * o_s[hh]
                    + _lane_col(gates, base + 2) * o_w[hh])
    o_ref[...] = jnp.concatenate(outs, axis=1).astype(o_ref.dtype)


def _nsa(q_r, kvc, kv_r, small3):
    b, s, _ = q_r.shape
    nr = kvc.shape[2]
    nb = s // SEL_BLOCK
    tq = WINDOW
    assert s % tq == 0 and nb <= LANE and nr <= LANE
    gw = NSA_HPG * LANE
    eh = jnp.asarray(np.arange(s)[:, None] // SEL_BLOCK == np.arange(LANE)[None, :], BF16)
    ovt = jnp.asarray(_overlap_matrix(s).T, BF16)
    kvspec = lambda off: pl.BlockSpec((None, s, LANE), lambda i, g, j, off=off: (i, 0, off + g))
    cspec = lambda off: pl.BlockSpec((None, None, nr, NSA_DH), lambda i, g, j, off=off: (i, off + g, 0, 0))
    return pl.pallas_call(
        functools.partial(_nsa_kernel, tq=tq),
        grid=(b, NSA_KV_HEADS, s // tq),
        in_specs=[pl.BlockSpec((None, tq, gw), lambda i, g, j: (i, j, g)),
                  cspec(0), cspec(2),
                  kvspec(0), kvspec(2), kvspec(4), kvspec(6),
                  pl.BlockSpec((s, LANE), lambda i, g, j: (0, 0)),
                  pl.BlockSpec((None, tq, LANE), lambda i, g, j: (i, j, 0)),
                  pl.BlockSpec((LANE, LANE), lambda i, g, j: (0, 0))],
        out_specs=pl.BlockSpec((None, tq, gw), lambda i, g, j: (i, j, g)),
        out_shape=jax.ShapeDtypeStruct((b, s, NSA_HEADS * NSA_DH), BF16),
        compiler_params=_cp(("parallel", "parallel", "arbitrary")),
        name="nsa",
    )(q_r, kvc, kvc, kv_r, kv_r, kv_r, kv_r, eh, small3, ovt)


def _merge_xattn_kernel(ya_ref, yb_ref, mg_ref, x_ref, wout_ref, g1_ref, b1_ref,
                        kv_ref, wq_ref, wo_ref, g_ref, b_ref, o_ref):
    mg = mg_ref[...].astype(F32)
    merged = (_sigmoid(mg[:, :D_MODEL]) * ya_ref[...].astype(F32)
              + _sigmoid(mg[:, D_MODEL:]) * yb_ref[...].astype(F32))
    h = _layer_norm(DN_ALPHA * x_ref[...] + _bdot(merged, wout_ref[...]), g1_ref[...], b1_ref[...])
    q = _bdot(h, wq_ref[...]).astype(BF16)
    kv = kv_ref[...]
    outs = []
    for hh in range(XA_HEADS):
        sl = slice(hh * XA_DH, (hh + 1) * XA_DH)
        s = _bdot_nt(q[:, sl], kv[:, sl]) * (XA_DH ** -0.5)
        m = jnp.max(s, -1, keepdims=True)
        e = jnp.exp(s - m)
        p = e / jnp.sum(e, -1, keepdims=True)
        outs.append(_bdot(p, kv[:, D_MODEL + hh * XA_DH:D_MODEL + (hh + 1) * XA_DH]))
    o = jnp.concatenate(outs, axis=1)
    o_ref[...] = _layer_norm(DN_ALPHA * h + _bdot(o, wo_ref[...]), g_ref[...], b_ref[...])


def _merge_xattn(y_a, y_b, proj3, x3, w_out, g1, b1, kv3, wq, wo, g2, b2, tm=512):
    bsz, s, _ = x3.shape
    m = kv3.shape[1]
    full = lambda shape: pl.BlockSpec(shape, lambda i, j: (0, 0))
    row = pl.BlockSpec((None, tm, D_MODEL), lambda i, j: (i, j, 0))
    return pl.pallas_call(
        _merge_xattn_kernel,
        grid=(bsz, s // tm),
        in_specs=[row, row,
                  pl.BlockSpec((None, tm, 2 * D_MODEL), lambda i, j: (i, j, CB_MG * LANE // (2 * D_MODEL))),
                  row, full((D_MODEL, D_MODEL)), full((1, D_MODEL)), full((1, D_MODEL)),
                  pl.BlockSpec((None, m, 2 * D_MODEL), lambda i, j: (i, 0, 0)),
                  full((D_MODEL, D_MODEL)), full((D_MODEL, D_MODEL)), full((1, D_MODEL)), full((1, D_MODEL))],
        out_specs=row,
        out_shape=jax.ShapeDtypeStruct((bsz, s, D_MODEL), F32),
        compiler_params=_cp(("parallel", "parallel")),
        name="merge_xattn",
    )(y_a, y_b, proj3, x3, w_out, g1, b1, kv3, wq, wo, g2, b2)


def _route(x, wr_h, wr_l, br):
    xh = x.astype(BF16)
    xl = (x - xh.astype(F32)).astype(BF16)
    logits = (jnp.dot(xh, wr_h, preferred_element_type=F32) + jnp.dot(xh, wr_l, preferred_element_type=F32)
              + jnp.dot(xl, wr_h, preferred_element_type=F32)) + br
    lane = lax.broadcasted_iota(jnp.int32, logits.shape, 1)
    gmask = lane < N_GROUPS
    gl = jnp.where(gmask, logits, -jnp.inf)
    gmax = jnp.max(gl, -1, keepdims=True)
    g_idx = jnp.min(jnp.where(gl == gmax, lane, LANE), -1, keepdims=True)
    p_top = 1.0 / jnp.sum(jnp.where(gmask, jnp.exp(gl - gmax), 0.0), -1, keepdims=True)
    e_lane = lane - N_GROUPS
    emask = (e_lane >= 0) & (e_lane < N_EXPERTS) & ((e_lane >> 3) == g_idx)
    el = jnp.where(emask, logits, -jnp.inf)
    emax = jnp.max(el, -1, keepdims=True)
    ee = jnp.where(emask, jnp.exp(el - emax), 0.0)
    pe = ee / jnp.sum(ee, -1, keepdims=True)
    pv = jnp.where(emask, pe, -jnp.inf)
    m1 = jnp.max(pv, -1, keepdims=True)
    i1 = jnp.min(jnp.where(pv == m1, lane, LANE), -1, keepdims=True)
    pv2 = jnp.where(lane == i1, -jnp.inf, pv)
    m2 = jnp.max(pv2, -1, keepdims=True)
    i2 = jnp.min(jnp.where(pv2 == m2, lane, LANE), -1, keepdims=True)
    denom = m1 + m2
    return i1, i2, p_top * m1 / denom, p_top * m2 / denom


META_E1, META_E2, META_R1, META_R2, META_W1, META_W2 = 0, 1, 2, 3, 4, 5


def _router_kernel(h_ref, wrh_ref, wrl_ref, br_ref, meta_ref, cnt_ref, carry_ref):
    @pl.when(pl.program_id(0) == 0)
    def _():
        carry_ref[...] = jnp.zeros_like(carry_ref)

    tm = h_ref.shape[0]
    i1, i2, w1, w2 = _route(h_ref[...], wrh_ref[...], wrl_ref[...], br_ref[...])
    lane = lax.broadcasted_iota(jnp.int32, (tm, LANE), 1)
    onehot = jnp.where((lane == i1) | (lane == i2), 1.0, 0.0)
    earlier = jnp.where(lax.broadcasted_iota(jnp.int32, (tm, tm), 1) < lax.broadcasted_iota(jnp.int32, (tm, tm), 0),
                        1.0, 0.0).astype(BF16)
    before = jnp.dot(earlier, onehot.astype(BF16), preferred_element_type=F32) + carry_ref[...]
    r1 = jnp.sum(jnp.where(lane == i1, before, 0.0), -1, keepdims=True)
    r2 = jnp.sum(jnp.where(lane == i2, before, 0.0), -1, keepdims=True)
    carry_ref[...] += jnp.sum(onehot, axis=0, keepdims=True)
    cnt_ref[...] = carry_ref[...]
    cols = ((META_E1, (i1 - N_GROUPS).astype(F32)), (META_E2, (i2 - N_GROUPS).astype(F32)),
            (META_R1, r1), (META_R2, r2), (META_W1, w1), (META_W2, w2))
    meta = jnp.zeros((tm, LANE), F32)
    for c, v in cols:
        meta = jnp.where(lane == c, v, meta)
    meta_ref[...] = meta


def _router(h2, wr_h, wr_l, br, tm=1024):
    n = h2.shape[0]
    full = lambda shape: pl.BlockSpec(shape, lambda i: (0, 0))
    return pl.pallas_call(
        _router_kernel,
        grid=(n // tm,),
        in_specs=[pl.BlockSpec((tm, D_MODEL), lambda i: (i, 0)),
                  full((D_MODEL, LANE)), full((D_MODEL, LANE)), full((1, LANE))],
        out_specs=[pl.BlockSpec((tm, LANE), lambda i: (i, 0)), full((1, LANE))],
        out_shape=[jax.ShapeDtypeStruct((n, LANE), F32), jax.ShapeDtypeStruct((1, LANE), F32)],
        scratch_shapes=[pltpu.VMEM((1, LANE), F32)],
        compiler_params=_cp(("arbitrary",)),
        name="moe_router",
    )(h2, wr_h, wr_l, br)


def _positions_kernel(meta_ref, cnt_ref, pos_ref):
    tm = meta_ref.shape[0]
    before = jnp.where(lax.broadcasted_iota(jnp.int32, (LANE, LANE), 0) < lax.broadcasted_iota(jnp.int32, (LANE, LANE), 1),
                       1.0, 0.0).astype(BF16)
    cnt8 = jnp.broadcast_to(cnt_ref[...], (8, LANE))
    start = sum(jnp.dot(p, before, preferred_element_type=F32) for p in _split3(cnt8))[0:1]
    meta = meta_ref[...]
    lane = lax.broadcasted_iota(jnp.int32, (tm, LANE), 1)
    pos = []
    for c_e, c_r in ((META_E1, META_R1), (META_E2, META_R2)):
        e_lane = meta[:, c_e:c_e + 1].astype(jnp.int32) + N_GROUPS
        seg = jnp.sum(jnp.where(lane == e_lane, start, 0.0), -1, keepdims=True)
        pos.append((seg + meta[:, c_r:c_r + 1]).astype(jnp.int32))
    both = jnp.where(lane == 0, pos[0], jnp.where(lane == 1, pos[1], 0)).astype(F32)
    pos_ref[...] = both.T[:8, :].astype(jnp.int32)


def _positions(meta, counts, tm=2048):
    n = meta.shape[0]
    return pl.pallas_call(
        _positions_kernel,
        grid=(n // tm,),
        in_specs=[pl.BlockSpec((tm, LANE), lambda i: (i, 0)), pl.BlockSpec((1, LANE), lambda i: (0, 0))],
        out_specs=pl.BlockSpec((8, tm), lambda i: (0, i)),
        out_shape=jax.ShapeDtypeStruct((8, n), jnp.int32),
        compiler_params=_cp(("parallel",)),
        name="moe_positions",
    )(meta, counts)


def _row_copies(n_rows, pos_refs, make_copy):
    def body(r, carry):
        for p in pos_refs:
            make_copy(r, p[r]).start()
        return carry

    lax.fori_loop(0, n_rows, body, 0, unroll=8)


def _dispatch_kernel(pos1_ref, pos2_ref, h_ref, xs_ref, sem):
    tm = h_ref.shape[0]
    _row_copies(tm, (pos1_ref, pos2_ref),
                lambda r, p: pltpu.make_async_copy(h_ref.at[pl.ds(r, 1)], xs_ref.at[pl.ds(p, 1)], sem))
    for _ in range(2):
        pltpu.make_async_copy(h_ref, xs_ref.at[pl.ds(0, tm)], sem).wait()


def _dispatch(h2, pos1, pos2, n_rows, tm=512):
    n = h2.shape[0]
    smem = pl.BlockSpec((tm,), lambda i: (i,), memory_space=pltpu.SMEM)
    return pl.pallas_call(
        _dispatch_kernel,
        grid=(n // tm,),
        in_specs=[smem, smem, pl.BlockSpec((tm, D_MODEL), lambda i: (i, 0))],
        out_specs=pl.BlockSpec(memory_space=pl.ANY),
        out_shape=jax.ShapeDtypeStruct((n_rows, D_MODEL), F32),
        scratch_shapes=[pltpu.SemaphoreType.DMA],
        compiler_params=_cp(("arbitrary",)),
        name="moe_dispatch",
    )(pos1, pos2, h2)


def _ffn_kernel(tile_ref, exp_ref, lo_ref, hi_ref, first_ref, xs_ref, wg_ref, wu_ref, wd_ref, ys_ref):
    i = pl.program_id(0)
    tm = xs_ref.shape[0]
    rows = lax.broadcasted_iota(jnp.int32, (tm, 1), 0)
    mine = (rows >= lo_ref[i]) & (rows < hi_ref[i])
    x = xs_ref[...].astype(BF16)
    gate = jnp.dot(x, wg_ref[...].astype(BF16), preferred_element_type=F32)
    up = jnp.dot(x, wu_ref[...].astype(BF16), preferred_element_type=F32)
    y = _bdot(_silu(gate) * up, wd_ref[...])

    @pl.when(first_ref[i] == 1)
    def _():
        ys_ref[...] = jnp.where(mine, y, 0.0)

    @pl.when(first_ref[i] == 0)
    def _():
        ys_ref[...] = jnp.where(mine, y, ys_ref[...])


def _ffn(xs, items, wg, wu, wd, tm):
    n_rows = xs.shape[0]
    wspec = lambda shape: pl.BlockSpec((None,) + shape, lambda i, tl, ex, lo, hi, fi: (ex[i], 0, 0))
    row_tile = pl.BlockSpec((tm, D_MODEL), lambda i, tl, ex, lo, hi, fi: (tl[i], 0))
    return pl.pallas_call(
        _ffn_kernel,
        grid_spec=pltpu.PrefetchScalarGridSpec(
            num_scalar_prefetch=5,
            grid=(items[0].shape[0],),
            in_specs=[row_tile, wspec((D_MODEL, D_FF)), wspec((D_MODEL, D_FF)), wspec((D_FF, D_MODEL))],
            out_specs=row_tile),
        out_shape=jax.ShapeDtypeStruct((n_rows, D_MODEL), F32),
        compiler_params=_cp(("arbitrary",)),
        name="moe_ffn",
    )(*items, xs, wg, wu, wd)


def _combine_kernel(pos1_ref, pos2_ref, nxt1_ref, nxt2_ref, h_ref, meta_ref, ys_ref, g_ref, b_ref, o_ref,
                    buf_ref, sem):
    i = pl.program_id(0)
    tm = h_ref.shape[0]
    slot = i % 2

    def start_gather(p1_ref, p2_ref, s):
        def copy(k, pos_ref):
            return lambda r, p: pltpu.make_async_copy(ys_ref.at[pl.ds(p, 1)], buf_ref.at[s, k, pl.ds(r, 1)],
                                                      sem.at[s])
        def body(r, carry):
            copy(0, p1_ref)(r, p1_ref[r]).start()
            copy(1, p2_ref)(r, p2_ref[r]).start()
            return carry
        lax.fori_loop(0, tm, body, 0, unroll=8)

    @pl.when(i == 0)
    def _():
        start_gather(pos1_ref, pos2_ref, 0)

    @pl.when(i + 1 < pl.num_programs(0))
    def _():
        start_gather(nxt1_ref, nxt2_ref, 1 - slot)

    for k in range(2):
        pltpu.make_async_copy(ys_ref.at[pl.ds(0, tm)], buf_ref.at[slot, k], sem.at[slot]).wait()
    meta = meta_ref[...]
    ffn = (meta[:, META_W1:META_W1 + 1] * buf_ref[slot, 0] + meta[:, META_W2:META_W2 + 1] * buf_ref[slot, 1])
    o_ref[...] = _layer_norm(DN_ALPHA * h_ref[...] + ffn, g_ref[...], b_ref[...])


def _combine(h2, meta, ys, pos1, pos2, g, b, tm=256):
    n = h2.shape[0]
    last = n // tm - 1
    smem = pl.BlockSpec((tm,), lambda i: (i,), memory_space=pltpu.SMEM)
    smem_next = pl.BlockSpec((tm,), lambda i: (jnp.minimum(i + 1, last),), memory_space=pltpu.SMEM)
    full = lambda shape: pl.BlockSpec(shape, lambda i: (0, 0))
    return pl.pallas_call(
        _combine_kernel,
        grid=(n // tm,),
        in_specs=[smem, smem, smem_next, smem_next, pl.BlockSpec((tm, D_MODEL), lambda i: (i, 0)),
                  pl.BlockSpec((tm, LANE), lambda i: (i, 0)),
                  pl.BlockSpec(memory_space=pl.ANY), full((1, D_MODEL)), full((1, D_MODEL))],
        out_specs=pl.BlockSpec((tm, D_MODEL), lambda i: (i, 0)),
        out_shape=jax.ShapeDtypeStruct((n, D_MODEL), F32),
        scratch_shapes=[pltpu.VMEM((2, 2, tm, D_MODEL), F32), pltpu.SemaphoreType.DMA((2,))],
        compiler_params=_cp(("arbitrary",)),
        name="moe_combine_ln3",
    )(pos1, pos2, pos1, pos2, h2, meta, ys, g, b)


def _moe(h2, wr_h, wr_l, br, wg, wu, wd, g, b, tile=512):
    n = h2.shape[0]
    i32 = jnp.int32
    meta, counts = _router(h2, wr_h, wr_l, br)
    cnt = counts[0, N_GROUPS:N_GROUPS + N_EXPERTS].astype(i32)
    seg_end = jnp.cumsum(cnt)
    seg_start = seg_end - cnt
    pos = _positions(meta, counts)
    pos1, pos2 = pos[0], pos[1]
    first_tile = seg_start // tile
    n_items_e = jnp.where(cnt > 0, (seg_end - 1) // tile - first_tile + 1, 0)
    item_end = jnp.cumsum(n_items_e)
    n_items = (2 * n) // tile + N_EXPERTS - 1
    idx = jnp.minimum(jnp.arange(n_items, dtype=i32), item_end[-1] - 1)
    exp = jnp.sum(idx[:, None] >= item_end[None, :], axis=1).astype(i32)
    til = first_tile[exp] + idx - (item_end - n_items_e)[exp]
    lo = jnp.maximum(seg_start[exp], til * tile) - til * tile
    hi = jnp.minimum(seg_end[exp], (til + 1) * tile) - til * tile
    repeat = jnp.arange(n_items, dtype=i32) >= item_end[-1]
    hi = jnp.where(repeat, lo, hi)
    first = jnp.concatenate([jnp.ones((1,), i32), (til[1:] != til[:-1]).astype(i32)])
    xs = _dispatch(h2, pos1, pos2, 2 * n)
    ys = _ffn(xs, (til.astype(i32), exp, lo.astype(i32), hi.astype(i32), first), wg, wu, wd, tile)
    return _combine(h2, meta, ys, pos1, pos2, g, b)


def _regroup_w_in(w):
    sizes = (1024, 1024, 1024, 1024, 8, 8, 1024, 256, 256, 256, 256, 256, 256, 24, 2048)
    offs = [0]
    for sz in sizes:
        offs.append(offs[-1] + sz)
    seg = lambda i: w[:, offs[i]:offs[i + 1]]
    small = jnp.concatenate([seg(4), seg(5), seg(13)], axis=1)
    small = jnp.pad(small, ((0, 0), (0, LANE - small.shape[1])))
    big = jnp.concatenate([seg(0), seg(1), seg(2), seg(3), seg(14), seg(6), seg(7), seg(8), seg(9), seg(10),
                           seg(11), seg(12)], axis=1)
    return big.astype(BF16), small.astype(BF16)


def _overlap_matrix(s):
    nb = s // SEL_BLOCK
    c0 = np.arange(LANE) * CMP_STRIDE
    s0 = np.arange(LANE) * SEL_BLOCK
    ov = np.minimum(c0[:, None] + CMP_BLOCK, s0[None, :] + SEL_BLOCK) - np.maximum(c0[:, None], s0[None, :])
    ov = np.maximum(ov, 0).astype(np.float32) / CMP_BLOCK
    nc = (s - CMP_BLOCK) // CMP_STRIDE + 1
    keep = (np.arange(LANE)[:, None] < nc) & (np.arange(LANE)[None, :] < nb)
    return np.where(keep, ov, 0.0).astype(np.float32)


def _layer(h, mem, pos3, freq2, w_in, conv_w, a_log, dt_bias, norm_w, cmp_pe, cmp_w1, cmp_b1, cmp_w2, w_out,
           ln1_g, ln1_b, xa_wq, xa_wkv, xa_wo, ln2_g, ln2_b, w_group, b_group, w_expert, b_expert,
           w_gate, w_up, w_down, ln3_g, ln3_b):
    b, s, d = h.shape
    n = b * s
    x2 = h.reshape(n, d)
    proj, small = _inproj(x2, *_regroup_w_in(w_in))
    proj3 = proj.reshape(b, s, -1)
    small3 = small.reshape(b, s, LANE)

    y_a = _gdn(proj3, small3, conv_w, a_log, dt_bias, norm_w)

    q_r, ckv_r, kv_r = _rope(proj3, pos3, freq2)
    kvc = _compress(ckv_r, cmp_pe.reshape(2, 1, CMP_BLOCK * NSA_DH).astype(BF16), cmp_w1.astype(BF16),
                    cmp_b1.reshape(2, 1, CMP_HIDDEN), cmp_w2.astype(BF16))
    y_b = _nsa(q_r, kvc, kv_r, small3)

    row = lambda v: v.reshape(1, -1)
    m = mem.shape[1]
    kv = _matmul(mem.reshape(b * m, d).astype(BF16), xa_wkv.astype(BF16), BF16, tm=512, tn=512)
    h2 = _merge_xattn(y_a, y_b, proj3, h, w_out.astype(BF16), row(ln1_g), row(ln1_b),
                      kv.reshape(b, m, 2 * d), xa_wq.astype(BF16), xa_wo.astype(BF16), row(ln2_g), row(ln2_b))

    wr = jnp.pad(jnp.concatenate([w_group, w_expert], axis=1), ((0, 0), (0, LANE - N_GROUPS - N_EXPERTS)))
    wr_h = wr.astype(BF16)
    wr_l = (wr - wr_h.astype(F32)).astype(BF16)
    br = jnp.pad(jnp.concatenate([b_group, b_expert]), (0, LANE - N_GROUPS - N_EXPERTS)).reshape(1, LANE)
    h3 = _moe(h2.reshape(n, d), wr_h, wr_l, br, w_gate, w_up, w_down, row(ln3_g), row(ln3_b))
    return h3.reshape(b, s, d)


def kernel(x, mem, positions, w_in, gdn_conv_w, gdn_a_log, gdn_dt_bias, gdn_norm_w, cmp_pe, cmp_w1, cmp_b1, cmp_w2, w_out, ln1_g, ln1_b, xa_wq, xa_wkv, xa_wo, ln2_g, ln2_b, moe_w_group, moe_b_group, moe_w_expert, moe_b_expert, moe_w_gate, moe_w_up, moe_w_down, ln3_g, ln3_b):
    half = NSA_DH // 2
    inv_freq = ROPE_THETA ** (-jnp.arange(half, dtype=F32) / half)
    freq2 = jnp.concatenate([inv_freq, inv_freq]).reshape(1, NSA_DH)
    pos3 = positions.astype(F32)[..., None]
    h = x
    for l in range(DEPTH):
        h = _layer(h, mem, pos3, freq2, w_in[l], gdn_conv_w[l], gdn_a_log[l], gdn_dt_bias[l], gdn_norm_w[l],
                   cmp_pe[l], cmp_w1[l], cmp_b1[l], cmp_w2[l], w_out[l], ln1_g[l], ln1_b[l],
                   xa_wq[l], xa_wkv[l], xa_wo[l], ln2_g[l], ln2_b[l], moe_w_group[l], moe_b_group[l],
                   moe_w_expert[l], moe_b_expert[l], moe_w_gate[l], moe_w_up[l], moe_w_down[l],
                   ln3_g[l], ln3_b[l])
    return h
```

```python
import functools

import jax
import jax.numpy as jnp
import numpy as np
from jax import lax
from jax.experimental import pallas as pl
from jax.experimental.pallas import tpu as pltpu

F32 = jnp.float32
BF16 = jnp.bfloat16

D_MODEL = 1024
LANE = 128
GDN_HEADS = 8
GDN_D = 128
GDN_CONV = 4
GDN_CHUNK = 64
NSA_HEADS = 8
NSA_KV_HEADS = 2
NSA_HPG = NSA_HEADS // NSA_KV_HEADS
NSA_DH = 128
CMP_BLOCK = 32
CMP_STRIDE = 16
CMP_HIDDEN = 256
SEL_BLOCK = 64
SEL_SHIFT = 6
SEL_TOPK = 8
WINDOW = 256
XA_HEADS = 4
XA_DH = 256
N_GROUPS = 4
EXPERTS_PER_GROUP = 8
N_EXPERTS = 32
D_FF = 256
DEPTH = 1
DN_ALPHA = (2.0 * DEPTH) ** 0.25
LN_EPS = 1e-5
RMS_EPS = 1e-6
ROPE_THETA = 10000.0
NEG = -1e30
MASK_BIG = 1e30

CB_Q, CB_K, CB_V, CB_Z = 0, 8, 16, 24
CB_MG = 32
CB_NQ = 48
CB_KV = 56
SM_BETA, SM_DECAY, SM_NGATE = 0, 8, 16

VMEM_LIMIT = 48 * 1024 * 1024


def _cp(sem, vmem=VMEM_LIMIT):
    return pltpu.CompilerParams(dimension_semantics=sem, vmem_limit_bytes=vmem)


def _bdot(a, b):
    return jnp.dot(a.astype(BF16), b.astype(BF16), preferred_element_type=F32)


def _bdot_nt(a, b):
    return lax.dot_general(a.astype(BF16), b.astype(BF16), (((1,), (1,)), ((), ())),
                           preferred_element_type=F32)


def _bdot_tn(a, b):
    return lax.dot_general(a.astype(BF16), b.astype(BF16), (((0,), (0,)), ((), ())),
                           preferred_element_type=F32)


def _split3(x):
    h = x.astype(BF16)
    r = x - h.astype(F32)
    m = r.astype(BF16)
    l = (r - m.astype(F32)).astype(BF16)
    return h, m, l


def _sigmoid(x):
    return 1.0 / (1.0 + jnp.exp(-x))


def _silu(x):
    return x * _sigmoid(x)


def _layer_norm(x, g, b):
    mu = jnp.mean(x, -1, keepdims=True)
    xc = x - mu
    var = jnp.mean(xc * xc, -1, keepdims=True)
    return xc * lax.rsqrt(var + LN_EPS) * g + b


def _mm_kernel(x_ref, w_ref, o_ref):
    o_ref[...] = jnp.dot(x_ref[...], w_ref[...], preferred_element_type=F32).astype(o_ref.dtype)


def _matmul(x, w, out_dtype, tm, tn):
    m, k = x.shape
    n = w.shape[1]
    return pl.pallas_call(
        _mm_kernel,
        grid=(m // tm, n // tn),
        in_specs=[pl.BlockSpec((tm, k), lambda i, j: (i, 0)),
                  pl.BlockSpec((k, tn), lambda i, j: (0, j))],
        out_specs=pl.BlockSpec((tm, tn), lambda i, j: (i, j)),
        out_shape=jax.ShapeDtypeStruct((m, n), out_dtype),
        compiler_params=_cp(("parallel", "parallel")),
        name="matmul",
    )(x, w)


def _inproj_kernel(x_ref, w_ref, ws_ref, o_ref, os_ref, xb_ref):
    @pl.when(pl.program_id(1) == 0)
    def _():
        xb_ref[...] = x_ref[...].astype(BF16)
        os_ref[...] = jnp.dot(xb_ref[...], ws_ref[...], preferred_element_type=F32)

    o_ref[...] = jnp.dot(xb_ref[...], w_ref[...], preferred_element_type=F32).astype(o_ref.dtype)


def _inproj(x2, w_big, w_small, tm=2048, tn=512):
    m, k = x2.shape
    n = w_big.shape[1]
    return pl.pallas_call(
        _inproj_kernel,
        grid=(m // tm, n // tn),
        in_specs=[pl.BlockSpec((tm, k), lambda i, j: (i, 0)),
                  pl.BlockSpec((k, tn), lambda i, j: (0, j)),
                  pl.BlockSpec((k, LANE), lambda i, j: (0, 0))],
        out_specs=[pl.BlockSpec((tm, tn), lambda i, j: (i, j)),
                   pl.BlockSpec((tm, LANE), lambda i, j: (i, 0))],
        out_shape=[jax.ShapeDtypeStruct((m, n), BF16), jax.ShapeDtypeStruct((m, LANE), F32)],
        scratch_shapes=[pltpu.VMEM((tm, k), BF16)],
        compiler_params=_cp(("parallel", "arbitrary")),
        name="inproj",
    )(x2, w_big, w_small)


def _gdn_kernel(q_ref, k_ref, v_ref, z_ref, sm_ref, cwq_ref, cwk_ref, cwv_ref,
                alog_ref, dtb_ref, nw_ref, o_ref, state_ref, tail_ref):
    c_len = q_ref.shape[0]

    @pl.when(pl.program_id(1) == 0)
    def _():
        state_ref[...] = jnp.zeros_like(state_ref)
        tail_ref[...] = jnp.zeros_like(tail_ref)

    row = lax.broadcasted_iota(jnp.int32, (c_len, c_len), 0)
    col = lax.broadcasted_iota(jnp.int32, (c_len, c_len), 1)
    causal = row >= col
    strict = row > col
    eye = jnp.where(row == col, 1.0, 0.0)
    tril_incl = jnp.where(causal, 1.0, 0.0).astype(BF16)

    def conv_silu(idx, x_ref, w_ref):
        cur = x_ref[...].astype(F32)
        xc = jnp.concatenate([tail_ref[idx], cur], axis=0)
        tail_ref[idx] = cur[c_len - 8:, :]
        w = w_ref[...]
        acc = xc[8:, :] * w[GDN_CONV - 1:GDN_CONV, :]
        for j in range(GDN_CONV - 1):
            off = 8 - (GDN_CONV - 1) + j
            acc = acc + xc[off:off + c_len, :] * w[j:j + 1, :]
        return _silu(acc)

    q_all = conv_silu(0, q_ref, cwq_ref)
    k_all = conv_silu(1, k_ref, cwk_ref)
    v_all = conv_silu(2, v_ref, cwv_ref)

    sm = sm_ref[...]
    beta_all = _sigmoid(sm)
    a_in = sm + dtb_ref[...]
    softplus = jnp.maximum(a_in, 0.0) + jnp.log(1.0 + jnp.exp(-jnp.abs(a_in)))
    g_all = -jnp.exp(alog_ref[...]) * softplus
    gh, gm, gl = _split3(g_all)
    gc_all = (jnp.dot(tril_incl, gh, preferred_element_type=F32)
              + jnp.dot(tril_incl, gm, preferred_element_type=F32)
              + jnp.dot(tril_incl, gl, preferred_element_type=F32))
    gc_t = gc_all.T

    heads = range(GDN_HEADS)
    hs = lambda x, h: x[:, h * GDN_D:(h + 1) * GDN_D]
    q = [hs(q_all, h) for h in heads]
    k = [hs(k_all, h) for h in heads]
    v = [hs(v_all, h) for h in heads]
    q = [x * (lax.rsqrt(jnp.sum(x * x, -1, keepdims=True) + RMS_EPS) * (GDN_D ** -0.5)) for x in q]
    k = [x * lax.rsqrt(jnp.sum(x * x, -1, keepdims=True) + RMS_EPS) for x in k]
    beta = [beta_all[:, SM_BETA + h:SM_BETA + h + 1] for h in heads]
    gc = [gc_all[:, SM_DECAY + h:SM_DECAY + h + 1] for h in heads]
    g_last = [x[c_len - 1:c_len, :] for x in gc]
    diff = [gc[h] - gc_t[SM_DECAY + h:SM_DECAY + h + 1, :] for h in heads]
    decay = [jnp.where(causal, jnp.exp(d), 0.0) for d in diff]
    kk = [_bdot_nt(x, x) for x in k]
    m_pow = [-(jnp.where(strict, kk[h] * decay[h], 0.0) * beta[h]) for h in heads]
    t_inv = [eye + m for m in m_pow]
    for _ in range((c_len - 1).bit_length() - 1):
        m_pow = [_bdot(m, m) for m in m_pow]
        t_inv = [t + _bdot(t, m) for t, m in zip(t_inv, m_pow)]
    e_gc = [jnp.exp(x) for x in gc]
    u = [_bdot(t_inv[h], v[h] * beta[h]) for h in heads]
    w = [_bdot(t_inv[h], k[h] * (beta[h] * e_gc[h])) for h in heads]
    qk = [_bdot_nt(q[h], k[h]) * decay[h] for h in heads]
    q_dec = [q[h] * e_gc[h] for h in heads]
    k_dec = [k[h] * jnp.exp(g_last[h] - gc[h]) for h in heads]
    state = [state_ref[h] for h in heads]
    v_new = [u[h] - _bdot(w[h], state[h]) for h in heads]
    o = [_bdot(q_dec[h], state[h]) + _bdot(qk[h], v_new[h]) for h in heads]
    for h in heads:
        state_ref[h] = state[h] * jnp.exp(g_last[h]) + _bdot_tn(k_dec[h], v_new[h])
    nw = nw_ref[...]
    o = [x * lax.rsqrt(jnp.mean(x * x, -1, keepdims=True) + RMS_EPS) * nw for x in o]
    o_ref[...] = (jnp.concatenate(o, axis=1) * _silu(z_ref[...].astype(F32))).astype(o_ref.dtype)


def _gdn(proj3, small3, conv_w, a_log, dt_bias, norm_w, c_len=128):
    b, s, _ = proj3.shape
    width = GDN_HEADS * GDN_D
    col = lambda off: pl.BlockSpec((None, c_len, width), lambda i, j, off=off: (i, j, off // GDN_HEADS))
    cw = lambda k: pl.BlockSpec((GDN_CONV, width), lambda i, j, k=k: (0, k))
    full = lambda shape: pl.BlockSpec(shape, lambda i, j: (0, 0))
    pad = lambda v: jnp.pad(v, (SM_DECAY, LANE - SM_DECAY - GDN_HEADS)).reshape(1, LANE)
    return pl.pallas_call(
        _gdn_kernel,
        grid=(b, s // c_len),
        in_specs=[col(CB_Q), col(CB_K), col(CB_V), col(CB_Z),
                  pl.BlockSpec((None, c_len, LANE), lambda i, j: (i, j, 0)),
                  cw(0), cw(1), cw(2),
                  full((1, LANE)), full((1, LANE)), full((1, GDN_D))],
        out_specs=pl.BlockSpec((None, c_len, width), lambda i, j: (i, j, 0)),
        out_shape=jax.ShapeDtypeStruct((b, s, width), BF16),
        scratch_shapes=[pltpu.VMEM((GDN_HEADS, GDN_D, GDN_D), F32), pltpu.VMEM((3, 8, width), F32)],
        compiler_params=_cp(("parallel", "arbitrary")),
        name="gdn",
    )(proj3, proj3, proj3, proj3, small3, conv_w, conv_w, conv_w,
      pad(a_log), pad(dt_bias), norm_w.reshape(1, GDN_D))


def _rope_kernel(nq_ref, kv0_ref, kv1_ref, kv2_ref, pos_ref, freq_ref, q_out, ckv_out, kv_out, tmp_ref):
    ang = pos_ref[...] * freq_ref[...]
    cos2 = jnp.cos(ang)
    sin = jnp.sin(ang)
    sin2 = jnp.where(lax.broadcasted_iota(jnp.int32, ang.shape, 1) < NSA_DH // 2, -sin, sin)

    def rope(x):
        x = x.astype(F32)
        return x * cos2 + pltpu.roll(x, NSA_DH // 2, 1) * sin2

    scale = NSA_DH ** -0.5
    q_out[...] = jnp.concatenate(
        [rope(nq_ref[:, hh * LANE:(hh + 1) * LANE]) * scale for hh in range(NSA_HEADS)],
        axis=1).astype(q_out.dtype)
    kv_refs = (kv0_ref, kv1_ref, kv2_ref)
    blk = lambda i: kv_refs[i // 4][:, (i % 4) * LANE:(i % 4 + 1) * LANE]
    n_out = tmp_ref.shape[0] // CMP_STRIDE
    for slot, val in enumerate((rope(blk(0)), rope(blk(1)), blk(2).astype(F32), blk(3).astype(F32))):
        tmp_ref[...] = val
        for l in range(CMP_STRIDE):
            ckv_out[slot, :, l * LANE:(l + 1) * LANE] = tmp_ref[pl.ds(l, n_out, stride=CMP_STRIDE), :].astype(
                ckv_out.dtype)
    kv_out[...] = jnp.concatenate(
        [rope(blk(4)), rope(blk(5)), blk(6), blk(7), rope(blk(8)), rope(blk(9)), blk(10), blk(11)],
        axis=1).astype(kv_out.dtype)


def _rope(proj3, pos3, freq2, ts=512):
    b, s, _ = proj3.shape
    kvspec = lambda k: pl.BlockSpec((None, ts, 4 * LANE), lambda i, j, k=k: (i, j, CB_KV // 4 + k))
    return pl.pallas_call(
        _rope_kernel,
        grid=(b, s // ts),
        in_specs=[pl.BlockSpec((None, ts, NSA_HEADS * LANE), lambda i, j: (i, j, CB_NQ // NSA_HEADS)),
                  kvspec(0), kvspec(1), kvspec(2),
                  pl.BlockSpec((None, ts, 1), lambda i, j: (i, j, 0)),
                  pl.BlockSpec((1, NSA_DH), lambda i, j: (0, 0))],
        out_specs=[pl.BlockSpec((None, ts, NSA_HEADS * LANE), lambda i, j: (i, j, 0)),
                   pl.BlockSpec((None, 4, ts // CMP_STRIDE, CMP_STRIDE * LANE), lambda i, j: (i, 0, j, 0)),
                   pl.BlockSpec((None, ts, 8 * LANE), lambda i, j: (i, j, 0))],
        out_shape=[jax.ShapeDtypeStruct((b, s, NSA_HEADS * LANE), BF16),
                   jax.ShapeDtypeStruct((b, 4, s // CMP_STRIDE, CMP_STRIDE * LANE), BF16),
                   jax.ShapeDtypeStruct((b, s, 8 * LANE), BF16)],
        scratch_shapes=[pltpu.VMEM((ts, LANE), F32)],
        compiler_params=_cp(("parallel", "parallel")),
        name="rope",
    )(proj3, proj3, proj3, proj3, pos3, freq2)


def _compress_kernel(x_ref, pe_ref, w1_ref, b1_ref, w2_ref, o_ref):
    x = x_ref[...]
    half = CMP_STRIDE * NSA_DH
    first = jnp.dot(x, w1_ref[:half, :], preferred_element_type=F32)
    second = jnp.dot(x, w1_ref[half:, :], preferred_element_type=F32)
    n_rows = x.shape[0]
    second = pltpu.roll(second, n_rows - 1, 0)
    pe = jnp.broadcast_to(pe_ref[...], (8, pe_ref.shape[1]))
    bias = jnp.dot(pe, w1_ref[...], preferred_element_type=F32)[0:1] + b1_ref[...]
    hid = first + second + bias
    gelu = 0.5 * hid * (1.0 + jnp.tanh(0.7978845608028654 * (hid + 0.044715 * hid * hid * hid)))
    o_ref[...] = _bdot(gelu, w2_ref[...]).astype(o_ref.dtype)


def _compress(ckv_r, pe, w1, b1, w2):
    b, _, nr, wd = ckv_r.shape
    return pl.pallas_call(
        _compress_kernel,
        grid=(b, 4),
        in_specs=[pl.BlockSpec((None, None, nr, wd), lambda i, j: (i, j, 0, 0)),
                  pl.BlockSpec((None, 1, CMP_BLOCK * NSA_DH), lambda i, j: (j // 2, 0, 0)),
                  pl.BlockSpec((None, CMP_BLOCK * NSA_DH, CMP_HIDDEN), lambda i, j: (j // 2, 0, 0)),
                  pl.BlockSpec((None, 1, CMP_HIDDEN), lambda i, j: (j // 2, 0, 0)),
                  pl.BlockSpec((None, CMP_HIDDEN, NSA_DH), lambda i, j: (j // 2, 0, 0))],
        out_specs=pl.BlockSpec((None, None, nr, NSA_DH), lambda i, j: (i, j, 0, 0)),
        out_shape=jax.ShapeDtypeStruct((b, 4, nr, NSA_DH), BF16),
        compiler_params=_cp(("parallel", "arbitrary")),
        name="compress",
    )(ckv_r, pe, w1, b1, w2)


def _nsa_kernel(q_ref, kc_ref, vc_ref, ks_ref, vs_ref, kw_ref, vw_ref, eh_ref, sm_ref, ovt_ref, o_ref, *, tq):
    g = pl.program_id(1)
    qi = pl.program_id(2)
    t0 = qi * tq
    heads = range(NSA_HPG)
    nb = eh_ref.shape[0] // SEL_BLOCK
    qs = [q_ref[:, hh * LANE:(hh + 1) * LANE] for hh in heads]
    transpose_bf16 = lambda x: x.astype(F32).T.astype(BF16)
    col_max = lambda s: jnp.max(s, 0, keepdims=True)

    cmask = (lax.broadcasted_iota(jnp.int32, (LANE, tq), 0) * CMP_STRIDE + (CMP_BLOCK - 1)
             <= t0 + lax.broadcasted_iota(jnp.int32, (LANE, tq), 1))
    kc = kc_ref[...]
    s_c = [jnp.where(cmask, _bdot_nt(kc, x), NEG) for x in qs]
    e_c = [jnp.where(cmask, jnp.exp(s - col_max(s)), 0.0) for s in s_c]
    p_c = [e * (1.0 / jnp.maximum(jnp.sum(e, 0, keepdims=True), 1e-30)) for e in e_c]
    vc_t = transpose_bf16(vc_ref[...])
    o_c = [_bdot(vc_t, p) for p in p_c]

    p_sum = p_c[0]
    for p in p_c[1:]:
        p_sum = p_sum + p
    ovt = ovt_ref[...]
    imp_t = sum(jnp.dot(ovt, part, preferred_element_type=F32) for part in _split3(p_sum))[:nb]
    blk = lax.broadcasted_iota(jnp.int32, (nb, tq), 0)
    t = t0 + lax.broadcasted_iota(jnp.int32, (nb, tq), 1)
    cur = t >> SEL_SHIFT
    forced = (blk == 0) | (blk == cur) | (blk == cur - 1)
    val = jnp.where(forced, jnp.inf, jnp.where(blk * SEL_BLOCK <= t, imp_t, -jnp.inf))
    rank = jnp.zeros((nb, tq), F32)
    for i in range(nb):
        vi = val[i:i + 1, :]
        rank = rank + jnp.where((vi > val) | ((vi == val) & (blk > i)), 1.0, 0.0)
    drop_t = jnp.where(rank < SEL_TOPK, 0.0, MASK_BIG)
    drop = jnp.concatenate([drop_t, jnp.zeros((LANE - nb, tq), F32)], axis=0).T
    drop_b = (-drop).astype(BF16)
    q_aug = [jnp.concatenate([x, drop_b], axis=1) for x in qs]

    ones_col = jnp.where(lax.broadcasted_iota(jnp.int32, (tq, LANE), 1) == 0, 1.0, 0.0).astype(BF16)
    key_i = lax.broadcasted_iota(jnp.int32, (tq, tq), 0)
    qry_i = lax.broadcasted_iota(jnp.int32, (tq, tq), 1)
    key_le_query = key_i <= qry_i
    normalise = lambda acc: acc[:NSA_DH, :] * (1.0 / jnp.maximum(acc[NSA_DH:NSA_DH + 1, :], 1e-30))

    prev0 = pl.multiple_of(jnp.maximum(qi - 1, 0) * tq, tq)
    own0 = pl.multiple_of(t0, tq)
    in_prev = key_i > qry_i + jnp.where(qi > 0, 0, tq)
    kw_prev, kw_own = kw_ref[pl.ds(prev0, tq), :], kw_ref[pl.ds(own0, tq), :]
    v_w_t = transpose_bf16(jnp.concatenate(
        [jnp.concatenate([vw_ref[pl.ds(prev0, tq), :], ones_col], axis=1),
         jnp.concatenate([vw_ref[pl.ds(own0, tq), :], ones_col], axis=1)], axis=0))
    s_w = [jnp.concatenate([jnp.where(in_prev, _bdot_nt(kw_prev, x), NEG),
                            jnp.where(key_le_query, _bdot_nt(kw_own, x), NEG)], axis=0) for x in qs]
    e_w = [jnp.exp(s - col_max(s)) for s in s_w]
    o_w = [normalise(_bdot(v_w_t, e)) for e in e_w]

    def sel_chunk(c, carry, diagonal):
        m_i, acc = carry
        k0 = pl.multiple_of(c * tq, tq)
        k_aug = jnp.concatenate([ks_ref[pl.ds(k0, tq), :], eh_ref[pl.ds(k0, tq), :]], axis=1)
        v_t = transpose_bf16(jnp.concatenate([vs_ref[pl.ds(k0, tq), :], ones_col], axis=1))
        s = [lax.dot_general(k_aug, x, (((1,), (1,)), ((), ())), preferred_element_type=F32) for x in q_aug]
        if diagonal:
            s = [jnp.where(key_le_query, x, -MASK_BIG) for x in s]
        m_new = [jnp.maximum(m, col_max(x)) for m, x in zip(m_i, s)]
        p = [jnp.exp(x - m).astype(BF16) for x, m in zip(s, m_new)]
        acc = [jnp.exp(m - mn) * a + jnp.dot(v_t, x, preferred_element_type=F32)
               for m, mn, a, x in zip(m_i, m_new, acc, p)]
        return tuple(m_new), tuple(acc)

    init = (tuple(jnp.full((1, tq), -MASK_BIG, F32) for _ in heads),
            tuple(jnp.zeros((2 * NSA_DH, tq), F32) for _ in heads))
    carry = lax.fori_loop(0, qi, functools.partial(sel_chunk, diagonal=False), init)
    _, acc_s = sel_chunk(qi, carry, True)
    o_s = [normalise(a) for a in acc_s]

    gates_t = _sigmoid(sm_ref[...]).T
    outs = []
    for hh in heads:
        def gate(branch):
            rows = [gates_t[SM_NGATE + (grp * NSA_HPG + hh) * 3 + branch][None, :] for grp in range(NSA_KV_HEADS)]
            return jnp.where(g == 0, rows[0], rows[1])
        assert NSA_KV_HEADS == 2
        outs.append((gate(0) * o_c[hh] + gate(1) * o_s[hh] + gate(2) * o_w[hh]).T)
    o_ref[...] = jnp.concatenate(outs, axis=1).astype(o_ref.dtype)


def _nsa(q_r, kvc, kv_r, small3):
    b, s, _ = q_r.shape
    nr = kvc.shape[2]
    nb = s // SEL_BLOCK
    tq = WINDOW
    assert s % tq == 0 and nb <= LANE and nr <= LANE
    gw = NSA_HPG * LANE
    eh = jnp.asarray(np.arange(s)[:, None] // SEL_BLOCK == np.arange(LANE)[None, :], BF16)
    ovt = jnp.asarray(_overlap_matrix(s).T, BF16)
    kvspec = lambda off: pl.BlockSpec((None, s, LANE), lambda i, g, j, off=off: (i, 0, off + g))
    cspec = lambda off: pl.BlockSpec((None, None, nr, NSA_DH), lambda i, g, j, off=off: (i, off + g, 0, 0))
    return pl.pallas_call(
        functools.partial(_nsa_kernel, tq=tq),
        grid=(b, NSA_KV_HEADS, s // tq),
        in_specs=[pl.BlockSpec((None, tq, gw), lambda i, g, j: (i, j, g)),
                  cspec(0), cspec(2),
                  kvspec(0), kvspec(2), kvspec(4), kvspec(6),
                  pl.BlockSpec((s, LANE), lambda i, g, j: (0, 0)),
                  pl.BlockSpec((None, tq, LANE), lambda i, g, j: (i, j, 0)),
                  pl.BlockSpec((LANE, LANE), lambda i, g, j: (0, 0))],
        out_specs=pl.BlockSpec((None, tq, gw), lambda i, g, j: (i, j, g)),
        out_shape=jax.ShapeDtypeStruct((b, s, NSA_HEADS * NSA_DH), BF16),
        compiler_params=_cp(("parallel", "parallel", "arbitrary")),
        name="nsa",
    )(q_r, kvc, kvc, kv_r, kv_r, kv_r, kv_r, eh, small3, ovt)


def _merge_xattn_kernel(ya_ref, yb_ref, mg_ref, x_ref, wout_ref, g1_ref, b1_ref,
                        kv_ref, wq_ref, wo_ref, g_ref, b_ref, o_ref):
    mg = mg_ref[...].astype(F32)
    merged = (_sigmoid(mg[:, :D_MODEL]) * ya_ref[...].astype(F32)
              + _sigmoid(mg[:, D_MODEL:]) * yb_ref[...].astype(F32))
    h = _layer_norm(DN_ALPHA * x_ref[...] + _bdot(merged, wout_ref[...]), g1_ref[...], b1_ref[...])
    q = _bdot(h, wq_ref[...]).astype(BF16)
    kv = kv_ref[...]
    outs = []
    for hh in range(XA_HEADS):
        sl = slice(hh * XA_DH, (hh + 1) * XA_DH)
        s = _bdot_nt(q[:, sl], kv[:, sl]) * (XA_DH ** -0.5)
        m = jnp.max(s, -1, keepdims=True)
        e = jnp.exp(s - m)
        p = e / jnp.sum(e, -1, keepdims=True)
        outs.append(_bdot(p, kv[:, D_MODEL + hh * XA_DH:D_MODEL + (hh + 1) * XA_DH]))
    o = jnp.concatenate(outs, axis=1)
    o_ref[...] = _layer_norm(DN_ALPHA * h + _bdot(o, wo_ref[...]), g_ref[...], b_ref[...])


def _merge_xattn(y_a, y_b, proj3, x3, w_out, g1, b1, kv3, wq, wo, g2, b2, tm=512):
    bsz, s, _ = x3.shape
    m = kv3.shape[1]
    full = lambda shape: pl.BlockSpec(shape, lambda i, j: (0, 0))
    row = pl.BlockSpec((None, tm, D_MODEL), lambda i, j: (i, j, 0))
    return pl.pallas_call(
        _merge_xattn_kernel,
        grid=(bsz, s // tm),
        in_specs=[row, row,
                  pl.BlockSpec((None, tm, 2 * D_MODEL), lambda i, j: (i, j, CB_MG * LANE // (2 * D_MODEL))),
                  row, full((D_MODEL, D_MODEL)), full((1, D_MODEL)), full((1, D_MODEL)),
                  pl.BlockSpec((None, m, 2 * D_MODEL), lambda i, j: (i, 0, 0)),
                  full((D_MODEL, D_MODEL)), full((D_MODEL, D_MODEL)), full((1, D_MODEL)), full((1, D_MODEL))],
        out_specs=row,
        out_shape=jax.ShapeDtypeStruct((bsz, s, D_MODEL), F32),
        compiler_params=_cp(("parallel", "parallel")),
        name="merge_xattn",
    )(y_a, y_b, proj3, x3, w_out, g1, b1, kv3, wq, wo, g2, b2)


def _route(x, wr_h, wr_l, br):
    xh = x.astype(BF16)
    xl = (x - xh.astype(F32)).astype(BF16)
    logits = (jnp.dot(xh, wr_h, preferred_element_type=F32) + jnp.dot(xh, wr_l, preferred_element_type=F32)
              + jnp.dot(xl, wr_h, preferred_element_type=F32)) + br
    lane = lax.broadcasted_iota(jnp.int32, logits.shape, 1)
    gmask = lane < N_GROUPS
    gl = jnp.where(gmask, logits, -jnp.inf)
    gmax = jnp.max(gl, -1, keepdims=True)
    g_idx = jnp.min(jnp.where(gl == gmax, lane, LANE), -1, keepdims=True)
    p_top = 1.0 / jnp.sum(jnp.where(gmask, jnp.exp(gl - gmax), 0.0), -1, keepdims=True)
    e_lane = lane - N_GROUPS
    emask = (e_lane >= 0) & (e_lane < N_EXPERTS) & ((e_lane >> 3) == g_idx)
    el = jnp.where(emask, logits, -jnp.inf)
    emax = jnp.max(el, -1, keepdims=True)
    ee = jnp.where(emask, jnp.exp(el - emax), 0.0)
    pe = ee / jnp.sum(ee, -1, keepdims=True)
    pv = jnp.where(emask, pe, -jnp.inf)
    m1 = jnp.max(pv, -1, keepdims=True)
    i1 = jnp.min(jnp.where(pv == m1, lane, LANE), -1, keepdims=True)
    pv2 = jnp.where(lane == i1, -jnp.inf, pv)
    m2 = jnp.max(pv2, -1, keepdims=True)
    i2 = jnp.min(jnp.where(pv2 == m2, lane, LANE), -1, keepdims=True)
    denom = m1 + m2
    return i1, i2, p_top * m1 / denom, p_top * m2 / denom


META_E1, META_E2, META_R1, META_R2, META_W1, META_W2 = 0, 1, 2, 3, 4, 5


def _router_kernel(h_ref, wrh_ref, wrl_ref, br_ref, meta_ref, cnt_ref, carry_ref):
    @pl.when(pl.program_id(0) == 0)
    def _():
        carry_ref[...] = jnp.zeros_like(carry_ref)

    tm = h_ref.shape[0]
    i1, i2, w1, w2 = _route(h_ref[...], wrh_ref[...], wrl_ref[...], br_ref[...])
    lane = lax.broadcasted_iota(jnp.int32, (tm, LANE), 1)
    onehot = jnp.where((lane == i1) | (lane == i2), 1.0, 0.0)
    earlier = jnp.where(lax.broadcasted_iota(jnp.int32, (tm, tm), 1) < lax.broadcasted_iota(jnp.int32, (tm, tm), 0),
                        1.0, 0.0).astype(BF16)
    before = jnp.dot(earlier, onehot.astype(BF16), preferred_element_type=F32) + carry_ref[...]
    r1 = jnp.sum(jnp.where(lane == i1, before, 0.0), -1, keepdims=True)
    r2 = jnp.sum(jnp.where(lane == i2, before, 0.0), -1, keepdims=True)
    carry_ref[...] += jnp.sum(onehot, axis=0, keepdims=True)
    cnt_ref[...] = carry_ref[...]
    cols = ((META_E1, (i1 - N_GROUPS).astype(F32)), (META_E2, (i2 - N_GROUPS).astype(F32)),
            (META_R1, r1), (META_R2, r2), (META_W1, w1), (META_W2, w2))
    meta = jnp.zeros((tm, LANE), F32)
    for c, v in cols:
        meta = jnp.where(lane == c, v, meta)
    meta_ref[...] = meta


def _router(h2, wr_h, wr_l, br, tm=1024):
    n = h2.shape[0]
    full = lambda shape: pl.BlockSpec(shape, lambda i: (0, 0))
    return pl.pallas_call(
        _router_kernel,
        grid=(n // tm,),
        in_specs=[pl.BlockSpec((tm, D_MODEL), lambda i: (i, 0)),
                  full((D_MODEL, LANE)), full((D_MODEL, LANE)), full((1, LANE))],
        out_specs=[pl.BlockSpec((tm, LANE), lambda i: (i, 0)), full((1, LANE))],
        out_shape=[jax.ShapeDtypeStruct((n, LANE), F32), jax.ShapeDtypeStruct((1, LANE), F32)],
        scratch_shapes=[pltpu.VMEM((1, LANE), F32)],
        compiler_params=_cp(("arbitrary",)),
        name="moe_router",
    )(h2, wr_h, wr_l, br)


def _positions_kernel(meta_ref, cnt_ref, pos_ref):
    tm = meta_ref.shape[0]
    before = jnp.where(lax.broadcasted_iota(jnp.int32, (LANE, LANE), 0) < lax.broadcasted_iota(jnp.int32, (LANE, LANE), 1),
                       1.0, 0.0).astype(BF16)
    cnt8 = jnp.broadcast_to(cnt_ref[...], (8, LANE))
    start = sum(jnp.dot(p, before, preferred_element_type=F32) for p in _split3(cnt8))[0:1]
    meta = meta_ref[...]
    lane = lax.broadcasted_iota(jnp.int32, (tm, LANE), 1)
    pos = []
    for c_e, c_r in ((META_E1, META_R1), (META_E2, META_R2)):
        e_lane = meta[:, c_e:c_e + 1].astype(jnp.int32) + N_GROUPS
        seg = jnp.sum(jnp.where(lane == e_lane, start, 0.0), -1, keepdims=True)
        pos.append((seg + meta[:, c_r:c_r + 1]).astype(jnp.int32))
    both = jnp.where(lane == 0, pos[0], jnp.where(lane == 1, pos[1], 0)).astype(F32)
    pos_ref[...] = both.T[:8, :].astype(jnp.int32)


def _positions(meta, counts, tm=2048):
    n = meta.shape[0]
    return pl.pallas_call(
        _positions_kernel,
        grid=(n // tm,),
        in_specs=[pl.BlockSpec((tm, LANE), lambda i: (i, 0)), pl.BlockSpec((1, LANE), lambda i: (0, 0))],
        out_specs=pl.BlockSpec((8, tm), lambda i: (0, i)),
        out_shape=jax.ShapeDtypeStruct((8, n), jnp.int32),
        compiler_params=_cp(("parallel",)),
        name="moe_positions",
    )(meta, counts)


def _row_copies(n_rows, pos_refs, make_copy):
    def body(r, carry):
        for p in pos_refs:
            make_copy(r, p[r]).start()
        return carry

    lax.fori_loop(0, n_rows, body, 0, unroll=8)


def _dispatch_kernel(pos1_ref, pos2_ref, h_ref, xs_ref, sem):
    tm = h_ref.shape[0]
    _row_copies(tm, (pos1_ref, pos2_ref),
                lambda r, p: pltpu.make_async_copy(h_ref.at[pl.ds(r, 1)], xs_ref.at[pl.ds(p, 1)], sem))
    for _ in range(2):
        pltpu.make_async_copy(h_ref, xs_ref.at[pl.ds(0, tm)], sem).wait()


def _dispatch(h2, pos1, pos2, n_rows, tm=512):
    n = h2.shape[0]
    smem = pl.BlockSpec((tm,), lambda i: (i,), memory_space=pltpu.SMEM)
    return pl.pallas_call(
        _dispatch_kernel,
        grid=(n // tm,),
        in_specs=[smem, smem, pl.BlockSpec((tm, D_MODEL), lambda i: (i, 0))],
        out_specs=pl.BlockSpec(memory_space=pl.ANY),
        out_shape=jax.ShapeDtypeStruct((n_rows, D_MODEL), F32),
        scratch_shapes=[pltpu.SemaphoreType.DMA],
        compiler_params=_cp(("arbitrary",)),
        name="moe_dispatch",
    )(pos1, pos2, h2)


def _ffn_kernel(tile_ref, exp_ref, lo_ref, hi_ref, first_ref, xs_ref, wg_ref, wu_ref, wd_ref, ys_ref):
    i = pl.program_id(0)
    tm = xs_ref.shape[0]
    rows = lax.broadcasted_iota(jnp.int32, (tm, 1), 0)
    mine = (rows >= lo_ref[i]) & (rows < hi_ref[i])
    x = xs_ref[...].astype(BF16)
    gate = jnp.dot(x, wg_ref[...].astype(BF16), preferred_element_type=F32)
    up = jnp.dot(x, wu_ref[...].astype(BF16), preferred_element_type=F32)
    y = _bdot(_silu(gate) * up, wd_ref[...])

    @pl.when(first_ref[i] == 1)
    def _():
        ys_ref[...] = jnp.where(mine, y, 0.0)

    @pl.when(first_ref[i] == 0)
    def _():
        ys_ref[...] = jnp.where(mine, y, ys_ref[...])


def _ffn(xs, items, wg, wu, wd, tm):
    n_rows = xs.shape[0]
    wspec = lambda shape: pl.BlockSpec((None,) + shape, lambda i, tl, ex, lo, hi, fi: (ex[i], 0, 0))
    row_tile = pl.BlockSpec((tm, D_MODEL), lambda i, tl, ex, lo, hi, fi: (tl[i], 0))
    return pl.pallas_call(
        _ffn_kernel,
        grid_spec=pltpu.PrefetchScalarGridSpec(
            num_scalar_prefetch=5,
            grid=(items[0].shape[0],),
            in_specs=[row_tile, wspec((D_MODEL, D_FF)), wspec((D_MODEL, D_FF)), wspec((D_FF, D_MODEL))],
            out_specs=row_tile),
        out_shape=jax.ShapeDtypeStruct((n_rows, D_MODEL), F32),
        compiler_params=_cp(("arbitrary",)),
        name="moe_ffn",
    )(*items, xs, wg, wu, wd)


def _combine_kernel(pos1_ref, pos2_ref, nxt1_ref, nxt2_ref, h_ref, meta_ref, ys_ref, g_ref, b_ref, o_ref,
                    buf_ref, sem):
    i = pl.program_id(0)
    tm = h_ref.shape[0]
    slot = i % 2

    def start_gather(p1_ref, p2_ref, s):
        def copy(k, pos_ref):
            return lambda r, p: pltpu.make_async_copy(ys_ref.at[pl.ds(p, 1)], buf_ref.at[s, k, pl.ds(r, 1)],
                                                      sem.at[s])
        def body(r, carry):
            copy(0, p1_ref)(r, p1_ref[r]).start()
            copy(1, p2_ref)(r, p2_ref[r]).start()
            return carry
        lax.fori_loop(0, tm, body, 0, unroll=8)

    @pl.when(i == 0)
    def _():
        start_gather(pos1_ref, pos2_ref, 0)

    @pl.when(i + 1 < pl.num_programs(0))
    def _():
        start_gather(nxt1_ref, nxt2_ref, 1 - slot)

    for k in range(2):
        pltpu.make_async_copy(ys_ref.at[pl.ds(0, tm)], buf_ref.at[slot, k], sem.at[slot]).wait()
    meta = meta_ref[...]
    ffn = (meta[:, META_W1:META_W1 + 1] * buf_ref[slot, 0] + meta[:, META_W2:META_W2 + 1] * buf_ref[slot, 1])
    o_ref[...] = _layer_norm(DN_ALPHA * h_ref[...] + ffn, g_ref[...], b_ref[...])


def _combine(h2, meta, ys, pos1, pos2, g, b, tm=256):
    n = h2.shape[0]
    last = n // tm - 1
    smem = pl.BlockSpec((tm,), lambda i: (i,), memory_space=pltpu.SMEM)
    smem_next = pl.BlockSpec((tm,), lambda i: (jnp.minimum(i + 1, last),), memory_space=pltpu.SMEM)
    full = lambda shape: pl.BlockSpec(shape, lambda i: (0, 0))
    return pl.pallas_call(
        _combine_kernel,
        grid=(n // tm,),
        in_specs=[smem, smem, smem_next, smem_next, pl.BlockSpec((tm, D_MODEL), lambda i: (i, 0)),
                  pl.BlockSpec((tm, LANE), lambda i: (i, 0)),
                  pl.BlockSpec(memory_space=pl.ANY), full((1, D_MODEL)), full((1, D_MODEL))],
        out_specs=pl.BlockSpec((tm, D_MODEL), lambda i: (i, 0)),
        out_shape=jax.ShapeDtypeStruct((n, D_MODEL), F32),
        scratch_shapes=[pltpu.VMEM((2, 2, tm, D_MODEL), F32), pltpu.SemaphoreType.DMA((2,))],
        compiler_params=_cp(("arbitrary",)),
        name="moe_combine_ln3",
    )(pos1, pos2, pos1, pos2, h2, meta, ys, g, b)


def _moe(h2, wr_h, wr_l, br, wg, wu, wd, g, b, tile=512):
    n = h2.shape[0]
    i32 = jnp.int32
    meta, counts = _router(h2, wr_h, wr_l, br)
    cnt = counts[0, N_GROUPS:N_GROUPS + N_EXPERTS].astype(i32)
    seg_end = jnp.cumsum(cnt)
    seg_start = seg_end - cnt
    pos = _positions(meta, counts)
    pos1, pos2 = pos[0], pos[1]
    first_tile = seg_start // tile
    n_items_e = jnp.where(cnt > 0, (seg_end - 1) // tile - first_tile + 1, 0)
    item_end = jnp.cumsum(n_items_e)
    n_items = (2 * n) // tile + N_EXPERTS - 1
    idx = jnp.minimum(jnp.arange(n_items, dtype=i32), item_end[-1] - 1)
    exp = jnp.sum(idx[:, None] >= item_end[None, :], axis=1).astype(i32)
    til = first_tile[exp] + idx - (item_end - n_items_e)[exp]
    lo = jnp.maximum(seg_start[exp], til * tile) - til * tile
    hi = jnp.minimum(seg_end[exp], (til + 1) * tile) - til * tile
    repeat = jnp.arange(n_items, dtype=i32) >= item_end[-1]
    hi = jnp.where(repeat, lo, hi)
    first = jnp.concatenate([jnp.ones((1,), i32), (til[1:] != til[:-1]).astype(i32)])
    xs = _dispatch(h2, pos1, pos2, 2 * n)
    ys = _ffn(xs, (til.astype(i32), exp, lo.astype(i32), hi.astype(i32), first), wg, wu, wd, tile)
    return _combine(h2, meta, ys, pos1, pos2, g, b)


def _regroup_w_in(w):
    sizes = (1024, 1024, 1024, 1024, 8, 8, 1024, 256, 256, 256, 256, 256, 256, 24, 2048)
    offs = [0]
    for sz in sizes:
        offs.append(offs[-1] + sz)
    seg = lambda i: w[:, offs[i]:offs[i + 1]]
    small = jnp.concatenate([seg(4), seg(5), seg(13)], axis=1)
    small = jnp.pad(small, ((0, 0), (0, LANE - small.shape[1])))
    big = jnp.concatenate([seg(0), seg(1), seg(2), seg(3), seg(14), seg(6), seg(7), seg(8), seg(9), seg(10),
                           seg(11), seg(12)], axis=1)
    return big.astype(BF16), small.astype(BF16)


def _overlap_matrix(s):
    nb = s // SEL_BLOCK
    c0 = np.arange(LANE) * CMP_STRIDE
    s0 = np.arange(LANE) * SEL_BLOCK
    ov = np.minimum(c0[:, None] + CMP_BLOCK, s0[None, :] + SEL_BLOCK) - np.maximum(c0[:, None], s0[None, :])
    ov = np.maximum(ov, 0).astype(np.float32) / CMP_BLOCK
    nc = (s - CMP_BLOCK) // CMP_STRIDE + 1
    keep = (np.arange(LANE)[:, None] < nc) & (np.arange(LANE)[None, :] < nb)
    return np.where(keep, ov, 0.0).astype(np.float32)


def _layer(h, mem, pos3, freq2, w_in, conv_w, a_log, dt_bias, norm_w, cmp_pe, cmp_w1, cmp_b1, cmp_w2, w_out,
           ln1_g, ln1_b, xa_wq, xa_wkv, xa_wo, ln2_g, ln2_b, w_group, b_group, w_expert, b_expert,
           w_gate, w_up, w_down, ln3_g, ln3_b):
    b, s, d = h.shape
    n = b * s
    x2 = h.reshape(n, d)
    proj, small = _inproj(x2, *_regroup_w_in(w_in))
    proj3 = proj.reshape(b, s, -1)
    small3 = small.reshape(b, s, LANE)

    y_a = _gdn(proj3, small3, conv_w, a_log, dt_bias, norm_w)

    q_r, ckv_r, kv_r = _rope(proj3, pos3, freq2)
    kvc = _compress(ckv_r, cmp_pe.reshape(2, 1, CMP_BLOCK * NSA_DH).astype(BF16), cmp_w1.astype(BF16),
                    cmp_b1.reshape(2, 1, CMP_HIDDEN), cmp_w2.astype(BF16))
    y_b = _nsa(q_r, kvc, kv_r, small3)

    row = lambda v: v.reshape(1, -1)
    m = mem.shape[1]
    kv = _matmul(mem.reshape(b * m, d).astype(BF16), xa_wkv.astype(BF16), BF16, tm=512, tn=512)
    h2 = _merge_xattn(y_a, y_b, proj3, h, w_out.astype(BF16), row(ln1_g), row(ln1_b),
                      kv.reshape(b, m, 2 * d), xa_wq.astype(BF16), xa_wo.astype(BF16), row(ln2_g), row(ln2_b))

    wr = jnp.pad(jnp.concatenate([w_group, w_expert], axis=1), ((0, 0), (0, LANE - N_GROUPS - N_EXPERTS)))
    wr_h = wr.astype(BF16)
    wr_l = (wr - wr_h.astype(F32)).astype(BF16)
    br = jnp.pad(jnp.concatenate([b_group, b_expert]), (0, LANE - N_GROUPS - N_EXPERTS)).reshape(1, LANE)
    h3 = _moe(h2.reshape(n, d), wr_h, wr_l, br, w_gate, w_up, w_down, row(ln3_g), row(ln3_b))
    return h3.reshape(b, s, d)


def kernel(x, mem, positions, w_in, gdn_conv_w, gdn_a_log, gdn_dt_bias, gdn_norm_w, cmp_pe, cmp_w1, cmp_b1, cmp_w2, w_out, ln1_g, ln1_b, xa_wq, xa_wkv, xa_wo, ln2_g, ln2_b, moe_w_group, moe_b_group, moe_w_expert, moe_b_expert, moe_w_gate, moe_w_up, moe_w_down, ln3_g, ln3_b):
    half = NSA_DH // 2
    inv_freq = ROPE_THETA ** (-jnp.arange(half, dtype=F32) / half)
    freq2 = jnp.concatenate([inv_freq, inv_freq]).reshape(1, NSA_DH)
    pos3 = positions.astype(F32)[..., None]
    h = x
    for l in range(DEPTH):
        h = _layer(h, mem, pos3, freq2, w_in[l], gdn_conv_w[l], gdn_a_log[l], gdn_dt_bias[l], gdn_norm_w[l],
                   cmp_pe[l], cmp_w1[l], cmp_b1[l], cmp_w2[l], w_out[l], ln1_g[l], ln1_b[l],
                   xa_wq[l], xa_wkv[l], xa_wo[l], ln2_g[l], ln2_b[l], moe_w_group[l], moe_b_group[l],
                   moe_w_expert[l], moe_b_expert[l], moe_w_gate[l], moe_w_up[l], moe_w_down[l],
                   ln3_g[l], ln3_b[l])
    return h
```

```python
import functools

import jax
import jax.numpy as jnp
import numpy as np
from jax import lax
from jax.experimental import pallas as pl
from jax.experimental.pallas import tpu as pltpu

F32 = jnp.float32
BF16 = jnp.bfloat16

D_MODEL = 1024
LANE = 128
GDN_HEADS = 8
GDN_D = 128
GDN_CONV = 4
GDN_CHUNK = 64
NSA_HEADS = 8
NSA_KV_HEADS = 2
NSA_HPG = NSA_HEADS // NSA_KV_HEADS
NSA_DH = 128
CMP_BLOCK = 32
CMP_STRIDE = 16
CMP_HIDDEN = 256
SEL_BLOCK = 64
SEL_SHIFT = 6
SEL_TOPK = 8
WINDOW = 256
XA_HEADS = 4
XA_DH = 256
N_GROUPS = 4
EXPERTS_PER_GROUP = 8
N_EXPERTS = 32
D_FF = 256
DEPTH = 1
DN_ALPHA = (2.0 * DEPTH) ** 0.25
LN_EPS = 1e-5
RMS_EPS = 1e-6
ROPE_THETA = 10000.0
NEG = -1e30
MASK_BIG = 1e30

CB_Q, CB_K, CB_V, CB_Z = 0, 8, 16, 24
CB_MG = 32
CB_NQ = 48
CB_KV = 56
SM_BETA, SM_DECAY, SM_NGATE = 0, 8, 16

VMEM_LIMIT = 48 * 1024 * 1024


def _cp(sem, vmem=VMEM_LIMIT):
    return pltpu.CompilerParams(dimension_semantics=sem, vmem_limit_bytes=vmem)


def _bdot(a, b):
    return jnp.dot(a.astype(BF16), b.astype(BF16), preferred_element_type=F32)


def _bdot_nt(a, b):
    return lax.dot_general(a.astype(BF16), b.astype(BF16), (((1,), (1,)), ((), ())),
                           preferred_element_type=F32)


def _bdot_tn(a, b):
    return lax.dot_general(a.astype(BF16), b.astype(BF16), (((0,), (0,)), ((), ())),
                           preferred_element_type=F32)


def _split3(x):
    h = x.astype(BF16)
    r = x - h.astype(F32)
    m = r.astype(BF16)
    l = (r - m.astype(F32)).astype(BF16)
    return h, m, l


def _sigmoid(x):
    return 1.0 / (1.0 + jnp.exp(-x))


def _silu(x):
    return x * _sigmoid(x)


def _layer_norm(x, g, b):
    mu = jnp.mean(x, -1, keepdims=True)
    xc = x - mu
    var = jnp.mean(xc * xc, -1, keepdims=True)
    return xc * lax.rsqrt(var + LN_EPS) * g + b


def _mm_kernel(x_ref, w_ref, o_ref):
    o_ref[...] = jnp.dot(x_ref[...], w_ref[...], preferred_element_type=F32).astype(o_ref.dtype)


def _matmul(x, w, out_dtype, tm, tn):
    m, k = x.shape
    n = w.shape[1]
    return pl.pallas_call(
        _mm_kernel,
        grid=(m // tm, n // tn),
        in_specs=[pl.BlockSpec((tm, k), lambda i, j: (i, 0)),
                  pl.BlockSpec((k, tn), lambda i, j: (0, j))],
        out_specs=pl.BlockSpec((tm, tn), lambda i, j: (i, j)),
        out_shape=jax.ShapeDtypeStruct((m, n), out_dtype),
        compiler_params=_cp(("parallel", "parallel")),
        name="matmul",
    )(x, w)


def _inproj_kernel(x_ref, w_ref, ws_ref, o_ref, os_ref, xb_ref):
    @pl.when(pl.program_id(1) == 0)
    def _():
        xb_ref[...] = x_ref[...].astype(BF16)
        os_ref[...] = jnp.dot(xb_ref[...], ws_ref[...], preferred_element_type=F32)

    o_ref[...] = jnp.dot(xb_ref[...], w_ref[...], preferred_element_type=F32).astype(o_ref.dtype)


def _inproj(x2, w_big, w_small, tm=2048, tn=512):
    m, k = x2.shape
    n = w_big.shape[1]
    return pl.pallas_call(
        _inproj_kernel,
        grid=(m // tm, n // tn),
        in_specs=[pl.BlockSpec((tm, k), lambda i, j: (i, 0)),
                  pl.BlockSpec((k, tn), lambda i, j: (0, j)),
                  pl.BlockSpec((k, LANE), lambda i, j: (0, 0))],
        out_specs=[pl.BlockSpec((tm, tn), lambda i, j: (i, j)),
                   pl.BlockSpec((tm, LANE), lambda i, j: (i, 0))],
        out_shape=[jax.ShapeDtypeStruct((m, n), BF16), jax.ShapeDtypeStruct((m, LANE), F32)],
        scratch_shapes=[pltpu.VMEM((tm, k), BF16)],
        compiler_params=_cp(("parallel", "arbitrary")),
        name="inproj",
    )(x2, w_big, w_small)


def _gdn_kernel(q_ref, k_ref, v_ref, z_ref, sm_ref, cwq_ref, cwk_ref, cwv_ref,
                alog_ref, dtb_ref, nw_ref, o_ref, state_ref, tail_ref):
    c_len = q_ref.shape[0]

    @pl.when(pl.program_id(1) == 0)
    def _():
        state_ref[...] = jnp.zeros_like(state_ref)
        tail_ref[...] = jnp.zeros_like(tail_ref)

    row = lax.broadcasted_iota(jnp.int32, (c_len, c_len), 0)
    col = lax.broadcasted_iota(jnp.int32, (c_len, c_len), 1)
    causal = row >= col
    strict = row > col
    eye = jnp.where(row == col, 1.0, 0.0)
    tril_incl = jnp.where(causal, 1.0, 0.0).astype(BF16)

    def conv_silu(idx, x_ref, w_ref):
        cur = x_ref[...].astype(F32)
        xc = jnp.concatenate([tail_ref[idx], cur], axis=0)
        tail_ref[idx] = cur[c_len - 8:, :]
        w = w_ref[...]
        acc = xc[8:, :] * w[GDN_CONV - 1:GDN_CONV, :]
        for j in range(GDN_CONV - 1):
            off = 8 - (GDN_CONV - 1) + j
            acc = acc + xc[off:off + c_len, :] * w[j:j + 1, :]
        return _silu(acc)

    q_all = conv_silu(0, q_ref, cwq_ref)
    k_all = conv_silu(1, k_ref, cwk_ref)
    v_all = conv_silu(2, v_ref, cwv_ref)

    sm = sm_ref[...]
    beta_all = _sigmoid(sm)
    a_in = sm + dtb_ref[...]
    softplus = jnp.maximum(a_in, 0.0) + jnp.log(1.0 + jnp.exp(-jnp.abs(a_in)))
    g_all = -jnp.exp(alog_ref[...]) * softplus
    gh, gm, gl = _split3(g_all)
    gc_all = (jnp.dot(tril_incl, gh, preferred_element_type=F32)
              + jnp.dot(tril_incl, gm, preferred_element_type=F32)
              + jnp.dot(tril_incl, gl, preferred_element_type=F32))
    gc_t = gc_all.T

    heads = range(GDN_HEADS)
    hs = lambda x, h: x[:, h * GDN_D:(h + 1) * GDN_D]
    q = [hs(q_all, h) for h in heads]
    k = [hs(k_all, h) for h in heads]
    v = [hs(v_all, h) for h in heads]
    q = [x * (lax.rsqrt(jnp.sum(x * x, -1, keepdims=True) + RMS_EPS) * (GDN_D ** -0.5)) for x in q]
    k = [x * lax.rsqrt(jnp.sum(x * x, -1, keepdims=True) + RMS_EPS) for x in k]
    beta = [beta_all[:, SM_BETA + h:SM_BETA + h + 1] for h in heads]
    gc = [gc_all[:, SM_DECAY + h:SM_DECAY + h + 1] for h in heads]
    g_last = [x[c_len - 1:c_len, :] for x in gc]
    diff = [gc[h] - gc_t[SM_DECAY + h:SM_DECAY + h + 1, :] for h in heads]
    decay = [jnp.where(causal, jnp.exp(d), 0.0) for d in diff]
    kk = [_bdot_nt(x, x) for x in k]
    m_pow = [-(jnp.where(strict, kk[h] * decay[h], 0.0) * beta[h]) for h in heads]
    t_inv = [eye + m for m in m_pow]
    for _ in range((c_len - 1).bit_length() - 1):
        m_pow = [_bdot(m, m) for m in m_pow]
        t_inv = [t + _bdot(t, m) for t, m in zip(t_inv, m_pow)]
    e_gc = [jnp.exp(x) for x in gc]
    u = [_bdot(t_inv[h], v[h] * beta[h]) for h in heads]
    w = [_bdot(t_inv[h], k[h] * (beta[h] * e_gc[h])) for h in heads]
    qk = [_bdot_nt(q[h], k[h]) * decay[h] for h in heads]
    q_dec = [q[h] * e_gc[h] for h in heads]
    k_dec = [k[h] * jnp.exp(g_last[h] - gc[h]) for h in heads]
    state = [state_ref[h] for h in heads]
    v_new = [u[h] - _bdot(w[h], state[h]) for h in heads]
    o = [_bdot(q_dec[h], state[h]) + _bdot(qk[h], v_new[h]) for h in heads]
    for h in heads:
        state_ref[h] = state[h] * jnp.exp(g_last[h]) + _bdot_tn(k_dec[h], v_new[h])
    nw = nw_ref[...]
    o = [x * lax.rsqrt(jnp.mean(x * x, -1, keepdims=True) + RMS_EPS) * nw for x in o]
    o_ref[...] = (jnp.concatenate(o, axis=1) * _silu(z_ref[...].astype(F32))).astype(o_ref.dtype)


def _gdn(proj3, small3, conv_w, a_log, dt_bias, norm_w, c_len=128):
    b, s, _ = proj3.shape
    width = GDN_HEADS * GDN_D
    col = lambda off: pl.BlockSpec((None, c_len, width), lambda i, j, off=off: (i, j, off // GDN_HEADS))
    cw = lambda k: pl.BlockSpec((GDN_CONV, width), lambda i, j, k=k: (0, k))
    full = lambda shape: pl.BlockSpec(shape, lambda i, j: (0, 0))
    pad = lambda v: jnp.pad(v, (SM_DECAY, LANE - SM_DECAY - GDN_HEADS)).reshape(1, LANE)
    return pl.pallas_call(
        _gdn_kernel,
        grid=(b, s // c_len),
        in_specs=[col(CB_Q), col(CB_K), col(CB_V), col(CB_Z),
                  pl.BlockSpec((None, c_len, LANE), lambda i, j: (i, j, 0)),
                  cw(0), cw(1), cw(2),
                  full((1, LANE)), full((1, LANE)), full((1, GDN_D))],
        out_specs=pl.BlockSpec((None, c_len, width), lambda i, j: (i, j, 0)),
        out_shape=jax.ShapeDtypeStruct((b, s, width), BF16),
        scratch_shapes=[pltpu.VMEM((GDN_HEADS, GDN_D, GDN_D), F32), pltpu.VMEM((3, 8, width), F32)],
        compiler_params=_cp(("parallel", "arbitrary")),
        name="gdn",
    )(proj3, proj3, proj3, proj3, small3, conv_w, conv_w, conv_w,
      pad(a_log), pad(dt_bias), norm_w.reshape(1, GDN_D))


def _rope_kernel(nq_ref, kv0_ref, kv1_ref, kv2_ref, pos_ref, freq_ref, q_out, ckv_out, kv_out, tmp_ref):
    ang = pos_ref[...] * freq_ref[...]
    cos2 = jnp.cos(ang)
    sin = jnp.sin(ang)
    sin2 = jnp.where(lax.broadcasted_iota(jnp.int32, ang.shape, 1) < NSA_DH // 2, -sin, sin)

    def rope(x):
        x = x.astype(F32)
        return x * cos2 + pltpu.roll(x, NSA_DH // 2, 1) * sin2

    scale = NSA_DH ** -0.5
    q_out[...] = jnp.concatenate(
        [rope(nq_ref[:, hh * LANE:(hh + 1) * LANE]) * scale for hh in range(NSA_HEADS)],
        axis=1).astype(q_out.dtype)
    kv_refs = (kv0_ref, kv1_ref, kv2_ref)
    blk = lambda i: kv_refs[i // 4][:, (i % 4) * LANE:(i % 4 + 1) * LANE]
    n_out = tmp_ref.shape[0] // CMP_STRIDE
    for slot, val in enumerate((rope(blk(0)), rope(blk(1)), blk(2).astype(F32), blk(3).astype(F32))):
        tmp_ref[...] = val
        for l in range(CMP_STRIDE):
            ckv_out[slot, :, l * LANE:(l + 1) * LANE] = tmp_ref[pl.ds(l, n_out, stride=CMP_STRIDE), :].astype(
                ckv_out.dtype)
    kv_out[...] = jnp.concatenate(
        [rope(blk(4)), rope(blk(5)), blk(6), blk(7), rope(blk(8)), rope(blk(9)), blk(10), blk(11)],
        axis=1).astype(kv_out.dtype)


def _rope(proj3, pos3, freq2, ts=512):
    b, s, _ = proj3.shape
    kvspec = lambda k: pl.BlockSpec((None, ts, 4 * LANE), lambda i, j, k=k: (i, j, CB_KV // 4 + k))
    return pl.pallas_call(
        _rope_kernel,
        grid=(b, s // ts),
        in_specs=[pl.BlockSpec((None, ts, NSA_HEADS * LANE), lambda i, j: (i, j, CB_NQ // NSA_HEADS)),
                  kvspec(0), kvspec(1), kvspec(2),
                  pl.BlockSpec((None, ts, 1), lambda i, j: (i, j, 0)),
                  pl.BlockSpec((1, NSA_DH), lambda i, j: (0, 0))],
        out_specs=[pl.BlockSpec((None, ts, NSA_HEADS * LANE), lambda i, j: (i, j, 0)),
                   pl.BlockSpec((None, 4, ts // CMP_STRIDE, CMP_STRIDE * LANE), lambda i, j: (i, 0, j, 0)),
                   pl.BlockSpec((None, ts, 8 * LANE), lambda i, j: (i, j, 0))],
        out_shape=[jax.ShapeDtypeStruct((b, s, NSA_HEADS * LANE), BF16),
                   jax.ShapeDtypeStruct((b, 4, s // CMP_STRIDE, CMP_STRIDE * LANE), BF16),
                   jax.ShapeDtypeStruct((b, s, 8 * LANE), BF16)],
        scratch_shapes=[pltpu.VMEM((ts, LANE), F32)],
        compiler_params=_cp(("parallel", "parallel")),
        name="rope",
    )(proj3, proj3, proj3, proj3, pos3, freq2)


def _compress_kernel(x_ref, pe_ref, w1_ref, b1_ref, w2_ref, o_ref):
    x = x_ref[...]
    half = CMP_STRIDE * NSA_DH
    first = jnp.dot(x, w1_ref[:half, :], preferred_element_type=F32)
    second = jnp.dot(x, w1_ref[half:, :], preferred_element_type=F32)
    n_rows = x.shape[0]
    second = pltpu.roll(second, n_rows - 1, 0)
    pe = jnp.broadcast_to(pe_ref[...], (8, pe_ref.shape[1]))
    bias = jnp.dot(pe, w1_ref[...], preferred_element_type=F32)[0:1] + b1_ref[...]
    hid = first + second + bias
    gelu = 0.5 * hid * (1.0 + jnp.tanh(0.7978845608028654 * (hid + 0.044715 * hid * hid * hid)))
    o_ref[...] = _bdot(gelu, w2_ref[...]).astype(o_ref.dtype)


def _compress(ckv_r, pe, w1, b1, w2):
    b, _, nr, wd = ckv_r.shape
    return pl.pallas_call(
        _compress_kernel,
        grid=(b, 4),
        in_specs=[pl.BlockSpec((None, None, nr, wd), lambda i, j: (i, j, 0, 0)),
                  pl.BlockSpec((None, 1, CMP_BLOCK * NSA_DH), lambda i, j: (j // 2, 0, 0)),
                  pl.BlockSpec((None, CMP_BLOCK * NSA_DH, CMP_HIDDEN), lambda i, j: (j // 2, 0, 0)),
                  pl.BlockSpec((None, 1, CMP_HIDDEN), lambda i, j: (j // 2, 0, 0)),
                  pl.BlockSpec((None, CMP_HIDDEN, NSA_DH), lambda i, j: (j // 2, 0, 0))],
        out_specs=pl.BlockSpec((None, None, nr, NSA_DH), lambda i, j: (i, j, 0, 0)),
        out_shape=jax.ShapeDtypeStruct((b, 4, nr, NSA_DH), BF16),
        compiler_params=_cp(("parallel", "arbitrary")),
        name="compress",
    )(ckv_r, pe, w1, b1, w2)


def _nsa_kernel(q_ref, kc_ref, vc_ref, ks_ref, vs_ref, kw_ref, vw_ref, eh_ref, sm_ref, ovt_ref, o_ref, *, tq):
    g = pl.program_id(1)
    qi = pl.program_id(2)
    t0 = qi * tq
    heads = range(NSA_HPG)
    nb = eh_ref.shape[0] // SEL_BLOCK
    qs = [q_ref[:, hh * LANE:(hh + 1) * LANE] for hh in heads]
    transpose_bf16 = lambda x: x.astype(F32).T.astype(BF16)
    col_max = lambda s: jnp.max(s, 0, keepdims=True)

    cmask = (lax.broadcasted_iota(jnp.int32, (LANE, tq), 0) * CMP_STRIDE + (CMP_BLOCK - 1)
             <= t0 + lax.broadcasted_iota(jnp.int32, (LANE, tq), 1))
    kc = kc_ref[...]
    s_c = [jnp.where(cmask, _bdot_nt(kc, x), NEG) for x in qs]
    e_c = [jnp.where(cmask, jnp.exp(s - col_max(s)), 0.0) for s in s_c]
    p_c = [e * (1.0 / jnp.maximum(jnp.sum(e, 0, keepdims=True), 1e-30)) for e in e_c]
    vc_t = transpose_bf16(vc_ref[...])
    o_c = [_bdot(vc_t, p) for p in p_c]

    p_sum = p_c[0]
    for p in p_c[1:]:
        p_sum = p_sum + p
    ovt = ovt_ref[...]
    imp_t = sum(jnp.dot(ovt, part, preferred_element_type=F32) for part in _split3(p_sum))[:nb]
    blk = lax.broadcasted_iota(jnp.int32, (nb, tq), 0)
    t = t0 + lax.broadcasted_iota(jnp.int32, (nb, tq), 1)
    cur = t >> SEL_SHIFT
    forced = (blk == 0) | (blk == cur) | (blk == cur - 1)
    val = jnp.where(forced, jnp.inf, jnp.where(blk * SEL_BLOCK <= t, imp_t, -jnp.inf))
    rank = jnp.zeros((nb, tq), F32)
    for i in range(nb):
        vi = val[i:i + 1, :]
        rank = rank + jnp.where((vi > val) | ((vi == val) & (blk > i)), 1.0, 0.0)
    drop_t = jnp.where(rank < SEL_TOPK, 0.0, MASK_BIG)
    drop = jnp.concatenate([drop_t, jnp.zeros((LANE - nb, tq), F32)], axis=0).T
    drop_b = (-drop).astype(BF16)
    q_aug = [jnp.concatenate([x, drop_b], axis=1) for x in qs]

    ones_col = jnp.where(lax.broadcasted_iota(jnp.int32, (tq, LANE), 1) == 0, 1.0, 0.0).astype(BF16)
    key_i = lax.broadcasted_iota(jnp.int32, (tq, tq), 0)
    qry_i = lax.broadcasted_iota(jnp.int32, (tq, tq), 1)
    key_le_query = key_i <= qry_i
    normalise = lambda acc: acc[:NSA_DH, :] * (1.0 / jnp.maximum(acc[NSA_DH:NSA_DH + 1, :], 1e-30))

    prev0 = pl.multiple_of(jnp.maximum(qi - 1, 0) * tq, tq)
    own0 = pl.multiple_of(t0, tq)
    in_prev = key_i > qry_i + jnp.where(qi > 0, 0, tq)
    kw_prev, kw_own = kw_ref[pl.ds(prev0, tq), :], kw_ref[pl.ds(own0, tq), :]
    v_w_t = transpose_bf16(jnp.concatenate(
        [jnp.concatenate([vw_ref[pl.ds(prev0, tq), :], ones_col], axis=1),
         jnp.concatenate([vw_ref[pl.ds(own0, tq), :], ones_col], axis=1)], axis=0))
    s_w = [jnp.concatenate([jnp.where(in_prev, _bdot_nt(kw_prev, x), NEG),
                            jnp.where(key_le_query, _bdot_nt(kw_own, x), NEG)], axis=0) for x in qs]
    e_w = [jnp.exp(s - col_max(s)) for s in s_w]
    o_w = [normalise(_bdot(v_w_t, e)) for e in e_w]

    def sel_chunk(c, carry, diagonal):
        m_i, acc = carry
        k0 = pl.multiple_of(c * tq, tq)
        k_aug = jnp.concatenate([ks_ref[pl.ds(k0, tq), :], eh_ref[pl.ds(k0, tq), :]], axis=1)
        v_t = transpose_bf16(jnp.concatenate([vs_ref[pl.ds(k0, tq), :], ones_col], axis=1))
        s = [lax.dot_general(k_aug, x, (((1,), (1,)), ((), ())), preferred_element_type=F32) for x in q_aug]
        if diagonal:
            s = [jnp.where(key_le_query, x, -MASK_BIG) for x in s]
        m_new = [jnp.maximum(m, col_max(x)) for m, x in zip(m_i, s)]
        p = [jnp.exp(x - m).astype(BF16) for x, m in zip(s, m_new)]
        acc = [jnp.exp(m - mn) * a + jnp.dot(v_t, x, preferred_element_type=F32)
               for m, mn, a, x in zip(m_i, m_new, acc, p)]
        return tuple(m_new), tuple(acc)

    init = (tuple(jnp.full((1, tq), -MASK_BIG, F32) for _ in heads),
            tuple(jnp.zeros((2 * NSA_DH, tq), F32) for _ in heads))
    carry = lax.fori_loop(0, qi, functools.partial(sel_chunk, diagonal=False), init)
    _, acc_s = sel_chunk(qi, carry, True)
    o_s = [normalise(a) for a in acc_s]

    gates_t = _sigmoid(sm_ref[...]).T
    outs = []
    for hh in heads:
        def gate(branch):
            rows = [gates_t[SM_NGATE + (grp * NSA_HPG + hh) * 3 + branch][None, :] for grp in range(NSA_KV_HEADS)]
            return jnp.where(g == 0, rows[0], rows[1])
        assert NSA_KV_HEADS == 2
        outs.append((gate(0) * o_c[hh] + gate(1) * o_s[hh] + gate(2) * o_w[hh]).T)
    o_ref[...] = jnp.concatenate(outs, axis=1).astype(o_ref.dtype)


def _nsa(q_r, kvc, kv_r, small3):
    b, s, _ = q_r.shape
    nr = kvc.shape[2]
    nb = s // SEL_BLOCK
    tq = WINDOW
    assert s % tq == 0 and nb <= LANE and nr <= LANE
    gw = NSA_HPG * LANE
    eh = jnp.asarray(np.arange(s)[:, None] // SEL_BLOCK == np.arange(LANE)[None, :], BF16)
    ovt = jnp.asarray(_overlap_matrix(s).T, BF16)
    kvspec = lambda off: pl.BlockSpec((None, s, LANE), lambda i, g, j, off=off: (i, 0, off + g))
    cspec = lambda off: pl.BlockSpec((None, None, nr, NSA_DH), lambda i, g, j, off=off: (i, off + g, 0, 0))
    return pl.pallas_call(
        functools.partial(_nsa_kernel, tq=tq),
        grid=(b, NSA_KV_HEADS, s // tq),
        in_specs=[pl.BlockSpec((None, tq, gw), lambda i, g, j: (i, j, g)),
                  cspec(0), cspec(2),
                  kvspec(0), kvspec(2), kvspec(4), kvspec(6),
                  pl.BlockSpec((s, LANE), lambda i, g, j: (0, 0)),
                  pl.BlockSpec((None, tq, LANE), lambda i, g, j: (i, j, 0)),
                  pl.BlockSpec((LANE, LANE), lambda i, g, j: (0, 0))],
        out_specs=pl.BlockSpec((None, tq, gw), lambda i, g, j: (i, j, g)),
        out_shape=jax.ShapeDtypeStruct((b, s, NSA_HEADS * NSA_DH), BF16),
        compiler_params=_cp(("parallel", "parallel", "arbitrary")),
        name="nsa",
    )(q_r, kvc, kvc, kv_r, kv_r, kv_r, kv_r, eh, small3, ovt)


def _merge_xattn_kernel(ya_ref, yb_ref, mg_ref, x_ref, wout_ref, g1_ref, b1_ref,
                        kv_ref, wq_ref, wo_ref, g_ref, b_ref, o_ref):
    mg = mg_ref[...].astype(F32)
    merged = (_sigmoid(mg[:, :D_MODEL]) * ya_ref[...].astype(F32)
              + _sigmoid(mg[:, D_MODEL:]) * yb_ref[...].astype(F32))
    h = _layer_norm(DN_ALPHA * x_ref[...] + _bdot(merged, wout_ref[...]), g1_ref[...], b1_ref[...])
    q = _bdot(h, wq_ref[...]).astype(BF16)
    kv = kv_ref[...]
    outs = []
    for hh in range(XA_HEADS):
        sl = slice(hh * XA_DH, (hh + 1) * XA_DH)
        s = _bdot_nt(q[:, sl], kv[:, sl]) * (XA_DH ** -0.5)
        m = jnp.max(s, -1, keepdims=True)
        e = jnp.exp(s - m)
        pv = _bdot(e, kv[:, D_MODEL + hh * XA_DH:D_MODEL + (hh + 1) * XA_DH])
        outs.append(pv * (1.0 / jnp.sum(e, -1, keepdims=True)))
    o = jnp.concatenate(outs, axis=1)
    o_ref[...] = _layer_norm(DN_ALPHA * h + _bdot(o, wo_ref[...]), g_ref[...], b_ref[...])


def _merge_xattn(y_a, y_b, proj3, x3, w_out, g1, b1, kv3, wq, wo, g2, b2, tm=512):
    bsz, s, _ = x3.shape
    m = kv3.shape[1]
    full = lambda shape: pl.BlockSpec(shape, lambda i, j: (0, 0))
    row = pl.BlockSpec((None, tm, D_MODEL), lambda i, j: (i, j, 0))
    return pl.pallas_call(
        _merge_xattn_kernel,
        grid=(bsz, s // tm),
        in_specs=[row, row,
                  pl.BlockSpec((None, tm, 2 * D_MODEL), lambda i, j: (i, j, CB_MG * LANE // (2 * D_MODEL))),
                  row, full((D_MODEL, D_MODEL)), full((1, D_MODEL)), full((1, D_MODEL)),
                  pl.BlockSpec((None, m, 2 * D_MODEL), lambda i, j: (i, 0, 0)),
                  full((D_MODEL, D_MODEL)), full((D_MODEL, D_MODEL)), full((1, D_MODEL)), full((1, D_MODEL))],
        out_specs=row,
        out_shape=jax.ShapeDtypeStruct((bsz, s, D_MODEL), F32),
        compiler_params=_cp(("parallel", "parallel")),
        name="merge_xattn",
    )(y_a, y_b, proj3, x3, w_out, g1, b1, kv3, wq, wo, g2, b2)


def _route(x, wrt_h, wrt_l, br):
    xh = x.astype(BF16)
    xl = (x - xh.astype(F32)).astype(BF16)
    logits = _bdot_nt(wrt_h, xh) + _bdot_nt(wrt_l, xh) + _bdot_nt(wrt_h, xl) + br
    row = lax.broadcasted_iota(jnp.int32, logits.shape, 0)
    first_row = lambda hit: jnp.min(jnp.where(hit, row, LANE), 0, keepdims=True)
    gmask = row < N_GROUPS
    gl = jnp.where(gmask, logits, -jnp.inf)
    gmax = jnp.max(gl, 0, keepdims=True)
    g_idx = first_row(gl == gmax)
    p_top = 1.0 / jnp.sum(jnp.where(gmask, jnp.exp(gl - gmax), 0.0), 0, keepdims=True)
    e_row = row - N_GROUPS
    emask = (e_row >= 0) & (e_row < N_EXPERTS) & ((e_row >> 3) == g_idx)
    el = jnp.where(emask, logits, -jnp.inf)
    ee = jnp.where(emask, jnp.exp(el - jnp.max(el, 0, keepdims=True)), 0.0)
    pe = ee / jnp.sum(ee, 0, keepdims=True)
    pv = jnp.where(emask, pe, -jnp.inf)
    m1 = jnp.max(pv, 0, keepdims=True)
    i1 = first_row(pv == m1)
    pv2 = jnp.where(row == i1, -jnp.inf, pv)
    m2 = jnp.max(pv2, 0, keepdims=True)
    i2 = first_row(pv2 == m2)
    denom = m1 + m2
    return i1, i2, p_top * m1 / denom, p_top * m2 / denom


META_E1, META_E2, META_R1, META_R2, META_W1, META_W2 = 0, 1, 2, 3, 4, 5
META_ROWS = 8


def _router_kernel(h_ref, wrh_ref, wrl_ref, br_ref, meta_ref, cnt_ref, carry_ref):
    @pl.when(pl.program_id(0) == 0)
    def _():
        carry_ref[...] = jnp.zeros_like(carry_ref)

    tm = h_ref.shape[0]
    i1, i2, w1, w2 = _route(h_ref[...], wrh_ref[...], wrl_ref[...], br_ref[...])
    row = lax.broadcasted_iota(jnp.int32, (LANE, tm), 0)
    onehot = jnp.where((row == i1) | (row == i2), 1.0, 0.0)
    earlier = jnp.where(lax.broadcasted_iota(jnp.int32, (tm, tm), 0) < lax.broadcasted_iota(jnp.int32, (tm, tm), 1),
                        1.0, 0.0).astype(BF16)
    before = jnp.dot(onehot.astype(BF16), earlier, preferred_element_type=F32) + carry_ref[...]
    r1 = jnp.sum(jnp.where(row == i1, before, 0.0), 0, keepdims=True)
    r2 = jnp.sum(jnp.where(row == i2, before, 0.0), 0, keepdims=True)
    carry_ref[...] += jnp.sum(onehot, axis=1, keepdims=True)
    cnt_ref[...] = carry_ref[...]
    rows = {META_E1: (i1 - N_GROUPS).astype(F32), META_E2: (i2 - N_GROUPS).astype(F32),
            META_R1: r1, META_R2: r2, META_W1: w1, META_W2: w2}
    zero = jnp.zeros((1, tm), F32)
    meta_ref[...] = jnp.concatenate([rows.get(r, zero) for r in range(META_ROWS)], axis=0)


def _router(h2, wrt_h, wrt_l, br, tm=1024):
    n = h2.shape[0]
    full = lambda shape: pl.BlockSpec(shape, lambda i: (0, 0))
    return pl.pallas_call(
        _router_kernel,
        grid=(n // tm,),
        in_specs=[pl.BlockSpec((tm, D_MODEL), lambda i: (i, 0)),
                  full((LANE, D_MODEL)), full((LANE, D_MODEL)), full((LANE, 1))],
        out_specs=[pl.BlockSpec((META_ROWS, tm), lambda i: (0, i)), full((LANE, 1))],
        out_shape=[jax.ShapeDtypeStruct((META_ROWS, n), F32), jax.ShapeDtypeStruct((LANE, 1), F32)],
        scratch_shapes=[pltpu.VMEM((LANE, 1), F32)],
        compiler_params=_cp(("arbitrary",)),
        name="moe_router",
    )(h2, wrt_h, wrt_l, br)


def _positions_kernel(meta_ref, cnt_ref, pos_ref):
    tm = meta_ref.shape[1]
    below = jnp.where(lax.broadcasted_iota(jnp.int32, (LANE, LANE), 1) < lax.broadcasted_iota(jnp.int32, (LANE, LANE), 0),
                      1.0, 0.0).astype(BF16)
    cnt = jnp.broadcast_to(cnt_ref[...], (LANE, LANE))
    start = sum(jnp.dot(below, p, preferred_element_type=F32) for p in _split3(cnt))
    start = jnp.concatenate([start] * (tm // LANE), axis=1)
    meta = meta_ref[...]
    row = lax.broadcasted_iota(jnp.int32, (LANE, tm), 0)
    pos = []
    for r_e, r_r in ((META_E1, META_R1), (META_E2, META_R2)):
        e_row = meta[r_e:r_e + 1, :].astype(jnp.int32) + N_GROUPS
        seg = jnp.sum(jnp.where(row == e_row, start, 0.0), 0, keepdims=True)
        pos.append((seg + meta[r_r:r_r + 1, :]).astype(jnp.int32))
    pos_ref[...] = jnp.concatenate(pos + [jnp.zeros((META_ROWS - 2, tm), jnp.int32)], axis=0)


def _positions(meta, counts, tm=2048):
    n = meta.shape[1]
    return pl.pallas_call(
        _positions_kernel,
        grid=(n // tm,),
        in_specs=[pl.BlockSpec((META_ROWS, tm), lambda i: (0, i)), pl.BlockSpec((LANE, 1), lambda i: (0, 0))],
        out_specs=pl.BlockSpec((META_ROWS, tm), lambda i: (0, i)),
        out_shape=jax.ShapeDtypeStruct((META_ROWS, n), jnp.int32),
        compiler_params=_cp(("parallel",)),
        name="moe_positions",
    )(meta, counts)


def _row_copies(n_rows, pos_refs, make_copy):
    def body(r, carry):
        for p in pos_refs:
            make_copy(r, p[r]).start()
        return carry

    lax.fori_loop(0, n_rows, body, 0, unroll=8)


def _dispatch_kernel(pos1_ref, pos2_ref, h_ref, xs_ref, sem):
    tm = h_ref.shape[0]
    _row_copies(tm, (pos1_ref, pos2_ref),
                lambda r, p: pltpu.make_async_copy(h_ref.at[pl.ds(r, 1)], xs_ref.at[pl.ds(p, 1)], sem))
    for _ in range(2):
        pltpu.make_async_copy(h_ref, xs_ref.at[pl.ds(0, tm)], sem).wait()


def _dispatch(h2, pos1, pos2, n_rows, tm=512):
    n = h2.shape[0]
    smem = pl.BlockSpec((tm,), lambda i: (i,), memory_space=pltpu.SMEM)
    return pl.pallas_call(
        _dispatch_kernel,
        grid=(n // tm,),
        in_specs=[smem, smem, pl.BlockSpec((tm, D_MODEL), lambda i: (i, 0))],
        out_specs=pl.BlockSpec(memory_space=pl.ANY),
        out_shape=jax.ShapeDtypeStruct((n_rows, D_MODEL), F32),
        scratch_shapes=[pltpu.SemaphoreType.DMA],
        compiler_params=_cp(("arbitrary",)),
        name="moe_dispatch",
    )(pos1, pos2, h2)


def _ffn_kernel(tile_ref, exp_ref, lo_ref, hi_ref, first_ref, xs_ref, wg_ref, wu_ref, wd_ref, ys_ref):
    i = pl.program_id(0)
    tm = xs_ref.shape[0]
    rows = lax.broadcasted_iota(jnp.int32, (tm, 1), 0)
    mine = (rows >= lo_ref[i]) & (rows < hi_ref[i])
    x = xs_ref[...].astype(BF16)
    gate = jnp.dot(x, wg_ref[...].astype(BF16), preferred_element_type=F32)
    up = jnp.dot(x, wu_ref[...].astype(BF16), preferred_element_type=F32)
    y = _bdot(_silu(gate) * up, wd_ref[...])

    @pl.when(first_ref[i] == 1)
    def _():
        ys_ref[...] = jnp.where(mine, y, 0.0)

    @pl.when(first_ref[i] == 0)
    def _():
        ys_ref[...] = jnp.where(mine, y, ys_ref[...])


def _ffn(xs, items, wg, wu, wd, tm):
    n_rows = xs.shape[0]
    wspec = lambda shape: pl.BlockSpec((None,) + shape, lambda i, tl, ex, lo, hi, fi: (ex[i], 0, 0))
    row_tile = pl.BlockSpec((tm, D_MODEL), lambda i, tl, ex, lo, hi, fi: (tl[i], 0))
    return pl.pallas_call(
        _ffn_kernel,
        grid_spec=pltpu.PrefetchScalarGridSpec(
            num_scalar_prefetch=5,
            grid=(items[0].shape[0],),
            in_specs=[row_tile, wspec((D_MODEL, D_FF)), wspec((D_MODEL, D_FF)), wspec((D_FF, D_MODEL))],
            out_specs=row_tile),
        out_shape=jax.ShapeDtypeStruct((n_rows, D_MODEL), F32),
        compiler_params=_cp(("arbitrary",)),
        name="moe_ffn",
    )(*items, xs, wg, wu, wd)


def _combine_kernel(pos1_ref, pos2_ref, nxt1_ref, nxt2_ref, h_ref, meta_ref, ys_ref, g_ref, b_ref, o_ref,
                    buf_ref, sem):
    i = pl.program_id(0)
    tm = h_ref.shape[0]
    slot = i % 2

    def start_gather(p1_ref, p2_ref, s):
        def copy(k, pos_ref):
            return lambda r, p: pltpu.make_async_copy(ys_ref.at[pl.ds(p, 1)], buf_ref.at[s, k, pl.ds(r, 1)],
                                                      sem.at[s])
        def body(r, carry):
            copy(0, p1_ref)(r, p1_ref[r]).start()
            copy(1, p2_ref)(r, p2_ref[r]).start()
            return carry
        lax.fori_loop(0, tm, body, 0, unroll=8)

    @pl.when(i == 0)
    def _():
        start_gather(pos1_ref, pos2_ref, 0)

    @pl.when(i + 1 < pl.num_programs(0))
    def _():
        start_gather(nxt1_ref, nxt2_ref, 1 - slot)

    for k in range(2):
        pltpu.make_async_copy(ys_ref.at[pl.ds(0, tm)], buf_ref.at[slot, k], sem.at[slot]).wait()
    meta = jnp.concatenate([meta_ref[...], jnp.zeros((LANE - META_ROWS, tm), F32)], axis=0).T
    ffn = (meta[:, META_W1:META_W1 + 1] * buf_ref[slot, 0] + meta[:, META_W2:META_W2 + 1] * buf_ref[slot, 1])
    o_ref[...] = _layer_norm(DN_ALPHA * h_ref[...] + ffn, g_ref[...], b_ref[...])


def _combine(h2, meta, ys, pos1, pos2, g, b, tm=256):
    n = h2.shape[0]
    last = n // tm - 1
    smem = pl.BlockSpec((tm,), lambda i: (i,), memory_space=pltpu.SMEM)
    smem_next = pl.BlockSpec((tm,), lambda i: (jnp.minimum(i + 1, last),), memory_space=pltpu.SMEM)
    full = lambda shape: pl.BlockSpec(shape, lambda i: (0, 0))
    return pl.pallas_call(
        _combine_kernel,
        grid=(n // tm,),
        in_specs=[smem, smem, smem_next, smem_next, pl.BlockSpec((tm, D_MODEL), lambda i: (i, 0)),
                  pl.BlockSpec((META_ROWS, tm), lambda i: (0, i)),
                  pl.BlockSpec(memory_space=pl.ANY), full((1, D_MODEL)), full((1, D_MODEL))],
        out_specs=pl.BlockSpec((tm, D_MODEL), lambda i: (i, 0)),
        out_shape=jax.ShapeDtypeStruct((n, D_MODEL), F32),
        scratch_shapes=[pltpu.VMEM((2, 2, tm, D_MODEL), F32), pltpu.SemaphoreType.DMA((2,))],
        compiler_params=_cp(("arbitrary",)),
        name="moe_combine_ln3",
    )(pos1, pos2, pos1, pos2, h2, meta, ys, g, b)


def _moe(h2, wr_h, wr_l, br, wg, wu, wd, g, b, tile=512):
    n = h2.shape[0]
    i32 = jnp.int32
    meta, counts = _router(h2, wr_h, wr_l, br)
    cnt = counts[N_GROUPS:N_GROUPS + N_EXPERTS, 0].astype(i32)
    seg_end = jnp.cumsum(cnt)
    seg_start = seg_end - cnt
    pos = _positions(meta, counts)
    pos1, pos2 = pos[0], pos[1]
    first_tile = seg_start // tile
    n_items_e = jnp.where(cnt > 0, (seg_end - 1) // tile - first_tile + 1, 0)
    item_end = jnp.cumsum(n_items_e)
    n_items = (2 * n) // tile + N_EXPERTS - 1
    idx = jnp.minimum(jnp.arange(n_items, dtype=i32), item_end[-1] - 1)
    exp = jnp.sum(idx[:, None] >= item_end[None, :], axis=1).astype(i32)
    til = first_tile[exp] + idx - (item_end - n_items_e)[exp]
    lo = jnp.maximum(seg_start[exp], til * tile) - til * tile
    hi = jnp.minimum(seg_end[exp], (til + 1) * tile) - til * tile
    repeat = jnp.arange(n_items, dtype=i32) >= item_end[-1]
    hi = jnp.where(repeat, lo, hi)
    first = jnp.concatenate([jnp.ones((1,), i32), (til[1:] != til[:-1]).astype(i32)])
    xs = _dispatch(h2, pos1, pos2, 2 * n)
    ys = _ffn(xs, (til.astype(i32), exp, lo.astype(i32), hi.astype(i32), first), wg, wu, wd, tile)
    return _combine(h2, meta, ys, pos1, pos2, g, b)


def _regroup_w_in(w):
    sizes = (1024, 1024, 1024, 1024, 8, 8, 1024, 256, 256, 256, 256, 256, 256, 24, 2048)
    offs = [0]
    for sz in sizes:
        offs.append(offs[-1] + sz)
    seg = lambda i: w[:, offs[i]:offs[i + 1]]
    small = jnp.concatenate([seg(4), seg(5), seg(13)], axis=1)
    small = jnp.pad(small, ((0, 0), (0, LANE - small.shape[1])))
    big = jnp.concatenate([seg(0), seg(1), seg(2), seg(3), seg(14), seg(6), seg(7), seg(8), seg(9), seg(10),
                           seg(11), seg(12)], axis=1)
    return big.astype(BF16), small.astype(BF16)


def _overlap_matrix(s):
    nb = s // SEL_BLOCK
    c0 = np.arange(LANE) * CMP_STRIDE
    s0 = np.arange(LANE) * SEL_BLOCK
    ov = np.minimum(c0[:, None] + CMP_BLOCK, s0[None, :] + SEL_BLOCK) - np.maximum(c0[:, None], s0[None, :])
    ov = np.maximum(ov, 0).astype(np.float32) / CMP_BLOCK
    nc = (s - CMP_BLOCK) // CMP_STRIDE + 1
    keep = (np.arange(LANE)[:, None] < nc) & (np.arange(LANE)[None, :] < nb)
    return np.where(keep, ov, 0.0).astype(np.float32)


def _layer(h, mem, pos3, freq2, w_in, conv_w, a_log, dt_bias, norm_w, cmp_pe, cmp_w1, cmp_b1, cmp_w2, w_out,
           ln1_g, ln1_b, xa_wq, xa_wkv, xa_wo, ln2_g, ln2_b, w_group, b_group, w_expert, b_expert,
           w_gate, w_up, w_down, ln3_g, ln3_b):
    b, s, d = h.shape
    n = b * s
    x2 = h.reshape(n, d)
    proj, small = _inproj(x2, *_regroup_w_in(w_in))
    proj3 = proj.reshape(b, s, -1)
    small3 = small.reshape(b, s, LANE)

    y_a = _gdn(proj3, small3, conv_w, a_log, dt_bias, norm_w)

    q_r, ckv_r, kv_r = _rope(proj3, pos3, freq2)
    kvc = _compress(ckv_r, cmp_pe.reshape(2, 1, CMP_BLOCK * NSA_DH).astype(BF16), cmp_w1.astype(BF16),
                    cmp_b1.reshape(2, 1, CMP_HIDDEN), cmp_w2.astype(BF16))
    y_b = _nsa(q_r, kvc, kv_r, small3)

    row = lambda v: v.reshape(1, -1)
    m = mem.shape[1]
    kv = _matmul(mem.reshape(b * m, d).astype(BF16), xa_wkv.astype(BF16), BF16, tm=512, tn=512)
    h2 = _merge_xattn(y_a, y_b, proj3, h, w_out.astype(BF16), row(ln1_g), row(ln1_b),
                      kv.reshape(b, m, 2 * d), xa_wq.astype(BF16), xa_wo.astype(BF16), row(ln2_g), row(ln2_b))

    wrt = jnp.pad(jnp.concatenate([w_group, w_expert], axis=1).T, ((0, LANE - N_GROUPS - N_EXPERTS), (0, 0)))
    wrt_h = wrt.astype(BF16)
    wrt_l = (wrt - wrt_h.astype(F32)).astype(BF16)
    br = jnp.pad(jnp.concatenate([b_group, b_expert]), (0, LANE - N_GROUPS - N_EXPERTS)).reshape(LANE, 1)
    h3 = _moe(h2.reshape(n, d), wrt_h, wrt_l, br, w_gate, w_up, w_down, row(ln3_g), row(ln3_b))
    return h3.reshape(b, s, d)


def kernel(x, mem, positions, w_in, gdn_conv_w, gdn_a_log, gdn_dt_bias, gdn_norm_w, cmp_pe, cmp_w1, cmp_b1, cmp_w2, w_out, ln1_g, ln1_b, xa_wq, xa_wkv, xa_wo, ln2_g, ln2_b, moe_w_group, moe_b_group, moe_w_expert, moe_b_expert, moe_w_gate, moe_w_up, moe_w_down, ln3_g, ln3_b):
    half = NSA_DH // 2
    inv_freq = ROPE_THETA ** (-jnp.arange(half, dtype=F32) / half)
    freq2 = jnp.concatenate([inv_freq, inv_freq]).reshape(1, NSA_DH)
    pos3 = positions.astype(F32)[..., None]
    h = x
    for l in range(DEPTH):
        h = _layer(h, mem, pos3, freq2, w_in[l], gdn_conv_w[l], gdn_a_log[l], gdn_dt_bias[l], gdn_norm_w[l],
                   cmp_pe[l], cmp_w1[l], cmp_b1[l], cmp_w2[l], w_out[l], ln1_g[l], ln1_b[l],
                   xa_wq[l], xa_wkv[l], xa_wo[l], ln2_g[l], ln2_b[l], moe_w_group[l], moe_b_group[l],
                   moe_w_expert[l], moe_b_expert[l], moe_w_gate[l], moe_w_up[l], moe_w_down[l],
                   ln3_g[l], ln3_b[l])
    return h
```

```python
import functools

import jax
import jax.numpy as jnp
import numpy as np
from jax import lax
from jax.experimental import pallas as pl
from jax.experimental.pallas import tpu as pltpu

F32 = jnp.float32
BF16 = jnp.bfloat16

D_MODEL = 1024
LANE = 128
GDN_HEADS = 8
GDN_D = 128
GDN_CONV = 4
GDN_CHUNK = 64
NSA_HEADS = 8
NSA_KV_HEADS = 2
NSA_HPG = NSA_HEADS // NSA_KV_HEADS
NSA_DH = 128
CMP_BLOCK = 32
CMP_STRIDE = 16
CMP_HIDDEN = 256
SEL_BLOCK = 64
SEL_SHIFT = 6
SEL_TOPK = 8
WINDOW = 256
XA_HEADS = 4
XA_DH = 256
N_GROUPS = 4
EXPERTS_PER_GROUP = 8
N_EXPERTS = 32
D_FF = 256
DEPTH = 1
DN_ALPHA = (2.0 * DEPTH) ** 0.25
LN_EPS = 1e-5
RMS_EPS = 1e-6
ROPE_THETA = 10000.0
NEG = -1e30
MASK_BIG = 1e30

CB_Q, CB_K, CB_V, CB_Z = 0, 8, 16, 24
CB_MG = 32
CB_NQ = 48
CB_KV = 56
SM_BETA, SM_DECAY, SM_NGATE = 0, 8, 16

VMEM_LIMIT = 48 * 1024 * 1024


def _cp(sem, vmem=VMEM_LIMIT):
    return pltpu.CompilerParams(dimension_semantics=sem, vmem_limit_bytes=vmem)


def _bdot(a, b):
    return jnp.dot(a.astype(BF16), b.astype(BF16), preferred_element_type=F32)


def _bdot_nt(a, b):
    return lax.dot_general(a.astype(BF16), b.astype(BF16), (((1,), (1,)), ((), ())),
                           preferred_element_type=F32)


def _bdot_tn(a, b):
    return lax.dot_general(a.astype(BF16), b.astype(BF16), (((0,), (0,)), ((), ())),
                           preferred_element_type=F32)


def _split3(x):
    h = x.astype(BF16)
    r = x - h.astype(F32)
    m = r.astype(BF16)
    l = (r - m.astype(F32)).astype(BF16)
    return h, m, l


def _sigmoid(x):
    return 1.0 / (1.0 + jnp.exp(-x))


def _silu(x):
    return x * _sigmoid(x)


def _layer_norm(x, g, b):
    mu = jnp.mean(x, -1, keepdims=True)
    xc = x - mu
    var = jnp.mean(xc * xc, -1, keepdims=True)
    return xc * lax.rsqrt(var + LN_EPS) * g + b


def _mm_kernel(x_ref, w_ref, o_ref):
    o_ref[...] = jnp.dot(x_ref[...], w_ref[...], preferred_element_type=F32).astype(o_ref.dtype)


def _matmul(x, w, out_dtype, tm, tn):
    m, k = x.shape
    n = w.shape[1]
    return pl.pallas_call(
        _mm_kernel,
        grid=(m // tm, n // tn),
        in_specs=[pl.BlockSpec((tm, k), lambda i, j: (i, 0)),
                  pl.BlockSpec((k, tn), lambda i, j: (0, j))],
        out_specs=pl.BlockSpec((tm, tn), lambda i, j: (i, j)),
        out_shape=jax.ShapeDtypeStruct((m, n), out_dtype),
        compiler_params=_cp(("parallel", "parallel")),
        name="matmul",
    )(x, w)


def _inproj_kernel(x_ref, w_ref, ws_ref, o_ref, os_ref, xb_ref):
    @pl.when(pl.program_id(1) == 0)
    def _():
        xb_ref[...] = x_ref[...].astype(BF16)
        os_ref[...] = jnp.dot(xb_ref[...], ws_ref[...], preferred_element_type=F32)

    o_ref[...] = jnp.dot(xb_ref[...], w_ref[...], preferred_element_type=F32).astype(o_ref.dtype)


def _inproj(x2, w_big, w_small, tm=2048, tn=512):
    m, k = x2.shape
    n = w_big.shape[1]
    return pl.pallas_call(
        _inproj_kernel,
        grid=(m // tm, n // tn),
        in_specs=[pl.BlockSpec((tm, k), lambda i, j: (i, 0)),
                  pl.BlockSpec((k, tn), lambda i, j: (0, j)),
                  pl.BlockSpec((k, LANE), lambda i, j: (0, 0))],
        out_specs=[pl.BlockSpec((tm, tn), lambda i, j: (i, j)),
                   pl.BlockSpec((tm, LANE), lambda i, j: (i, 0))],
        out_shape=[jax.ShapeDtypeStruct((m, n), BF16), jax.ShapeDtypeStruct((m, LANE), F32)],
        scratch_shapes=[pltpu.VMEM((tm, k), BF16)],
        compiler_params=_cp(("parallel", "arbitrary")),
        name="inproj",
    )(x2, w_big, w_small)


def _gdn_kernel(q_ref, k_ref, v_ref, z_ref, sm_ref, cwq_ref, cwk_ref, cwv_ref,
                alog_ref, dtb_ref, nw_ref, o_ref, state_ref, tail_ref, *, c_len):
    n_rows = q_ref.shape[0]
    chunks = range(n_rows // c_len)

    @pl.when(pl.program_id(1) == 0)
    def _():
        state_ref[...] = jnp.zeros_like(state_ref)
        tail_ref[...] = jnp.zeros_like(tail_ref)

    row = lax.broadcasted_iota(jnp.int32, (c_len, c_len), 0)
    col = lax.broadcasted_iota(jnp.int32, (c_len, c_len), 1)
    causal = row >= col
    strict = row > col
    eye = jnp.where(row == col, 1.0, 0.0)
    tril_incl = jnp.where(causal, 1.0, 0.0).astype(BF16)

    def conv_silu(idx, x_ref, w_ref):
        cur = x_ref[...].astype(F32)
        xc = jnp.concatenate([tail_ref[idx], cur], axis=0)
        tail_ref[idx] = cur[n_rows - 8:, :]
        w = w_ref[...]
        acc = xc[8:, :] * w[GDN_CONV - 1:GDN_CONV, :]
        for j in range(GDN_CONV - 1):
            off = 8 - (GDN_CONV - 1) + j
            acc = acc + xc[off:off + n_rows, :] * w[j:j + 1, :]
        return _silu(acc)

    q_all = conv_silu(0, q_ref, cwq_ref)
    k_all = conv_silu(1, k_ref, cwk_ref)
    v_all = conv_silu(2, v_ref, cwv_ref)

    sm = sm_ref[...]
    beta_all = _sigmoid(sm)
    a_in = sm + dtb_ref[...]
    softplus = jnp.maximum(a_in, 0.0) + jnp.log(1.0 + jnp.exp(-jnp.abs(a_in)))
    g_all = -jnp.exp(alog_ref[...]) * softplus
    gc_rows, gc_cols = [], []
    for c in chunks:
        gh, gm, gl = _split3(g_all[c * c_len:(c + 1) * c_len, :])
        gc_c = (jnp.dot(tril_incl, gh, preferred_element_type=F32)
                + jnp.dot(tril_incl, gm, preferred_element_type=F32)
                + jnp.dot(tril_incl, gl, preferred_element_type=F32))
        gc_rows.append(gc_c)
        gc_cols.append(gc_c.T)

    heads = range(GDN_HEADS)
    units = [(c, h) for c in chunks for h in heads]
    blk = lambda x, c, h: x[c * c_len:(c + 1) * c_len, h * GDN_D:(h + 1) * GDN_D]
    q = [blk(q_all, c, h) for c, h in units]
    k = [blk(k_all, c, h) for c, h in units]
    v = [blk(v_all, c, h) for c, h in units]
    q = [x * (lax.rsqrt(jnp.sum(x * x, -1, keepdims=True) + RMS_EPS) * (GDN_D ** -0.5)) for x in q]
    k = [x * lax.rsqrt(jnp.sum(x * x, -1, keepdims=True) + RMS_EPS) for x in k]
    beta = [beta_all[c * c_len:(c + 1) * c_len, SM_BETA + h:SM_BETA + h + 1] for c, h in units]
    gc = [gc_rows[c][:, SM_DECAY + h:SM_DECAY + h + 1] for c, h in units]
    g_last = [x[c_len - 1:c_len, :] for x in gc]
    diff = [gc[i] - gc_cols[c][SM_DECAY + h:SM_DECAY + h + 1, :] for i, (c, h) in enumerate(units)]
    decay = [jnp.where(causal, jnp.exp(d), 0.0) for d in diff]
    kk = [_bdot_nt(x, x) for x in k]
    m_pow = [-(jnp.where(strict, kk[i] * decay[i], 0.0) * beta[i]) for i in range(len(units))]
    t_inv = [eye + m for m in m_pow]
    for _ in range((c_len - 1).bit_length() - 1):
        m_pow = [_bdot(m, m) for m in m_pow]
        t_inv = [t + _bdot(t, m) for t, m in zip(t_inv, m_pow)]
    e_gc = [jnp.exp(x) for x in gc]
    u = [_bdot(t_inv[i], v[i] * beta[i]) for i in range(len(units))]
    w = [_bdot(t_inv[i], k[i] * (beta[i] * e_gc[i])) for i in range(len(units))]
    qk = [_bdot_nt(q[i], k[i]) * decay[i] for i in range(len(units))]
    q_dec = [q[i] * e_gc[i] for i in range(len(units))]
    k_dec = [k[i] * jnp.exp(g_last[i] - gc[i]) for i in range(len(units))]
    state = [state_ref[h] for h in heads]
    nw = nw_ref[...]
    out_rows = []
    for c in chunks:
        ids = [c * GDN_HEADS + h for h in heads]
        v_new = [u[i] - _bdot(w[i], state[h]) for h, i in zip(heads, ids)]
        o = [_bdot(q_dec[i], state[h]) + _bdot(qk[i], v_new[h]) for h, i in zip(heads, ids)]
        state = [state[h] * jnp.exp(g_last[i]) + _bdot_tn(k_dec[i], v_new[h]) for h, i in zip(heads, ids)]
        o = [x * lax.rsqrt(jnp.mean(x * x, -1, keepdims=True) + RMS_EPS) * nw for x in o]
        out_rows.append(jnp.concatenate(o, axis=1))
    for h in heads:
        state_ref[h] = state[h]
    o_ref[...] = (jnp.concatenate(out_rows, axis=0) * _silu(z_ref[...].astype(F32))).astype(o_ref.dtype)


def _gdn(proj3, small3, conv_w, a_log, dt_bias, norm_w, c_len=128, chunks_per_step=2):
    b, s, _ = proj3.shape
    width = GDN_HEADS * GDN_D
    rows = c_len * chunks_per_step
    col = lambda off: pl.BlockSpec((None, rows, width), lambda i, j, off=off: (i, j, off // GDN_HEADS))
    cw = lambda k: pl.BlockSpec((GDN_CONV, width), lambda i, j, k=k: (0, k))
    full = lambda shape: pl.BlockSpec(shape, lambda i, j: (0, 0))
    pad = lambda v: jnp.pad(v, (SM_DECAY, LANE - SM_DECAY - GDN_HEADS)).reshape(1, LANE)
    return pl.pallas_call(
        functools.partial(_gdn_kernel, c_len=c_len),
        grid=(b, s // rows),
        in_specs=[col(CB_Q), col(CB_K), col(CB_V), col(CB_Z),
                  pl.BlockSpec((None, rows, LANE), lambda i, j: (i, j, 0)),
                  cw(0), cw(1), cw(2),
                  full((1, LANE)), full((1, LANE)), full((1, GDN_D))],
        out_specs=pl.BlockSpec((None, rows, width), lambda i, j: (i, j, 0)),
        out_shape=jax.ShapeDtypeStruct((b, s, width), BF16),
        scratch_shapes=[pltpu.VMEM((GDN_HEADS, GDN_D, GDN_D), F32), pltpu.VMEM((3, 8, width), F32)],
        compiler_params=_cp(("parallel", "arbitrary")),
        name="gdn",
    )(proj3, proj3, proj3, proj3, small3, conv_w, conv_w, conv_w,
      pad(a_log), pad(dt_bias), norm_w.reshape(1, GDN_D))


def _rope_kernel(nq_ref, kv0_ref, kv1_ref, kv2_ref, pos_ref, freq_ref, q_out, ckv_out, kv_out, tmp_ref):
    ang = pos_ref[...] * freq_ref[...]
    cos2 = jnp.cos(ang)
    sin = jnp.sin(ang)
    sin2 = jnp.where(lax.broadcasted_iota(jnp.int32, ang.shape, 1) < NSA_DH // 2, -sin, sin)

    def rope(x):
        x = x.astype(F32)
        return x * cos2 + pltpu.roll(x, NSA_DH // 2, 1) * sin2

    scale = NSA_DH ** -0.5
    q_out[...] = jnp.concatenate(
        [rope(nq_ref[:, hh * LANE:(hh + 1) * LANE]) * scale for hh in range(NSA_HEADS)],
        axis=1).astype(q_out.dtype)
    kv_refs = (kv0_ref, kv1_ref, kv2_ref)
    blk = lambda i: kv_refs[i // 4][:, (i % 4) * LANE:(i % 4 + 1) * LANE]
    n_out = tmp_ref.shape[0] // CMP_STRIDE
    for slot, val in enumerate((rope(blk(0)), rope(blk(1)), blk(2).astype(F32), blk(3).astype(F32))):
        tmp_ref[...] = val
        for l in range(CMP_STRIDE):
            ckv_out[slot, :, l * LANE:(l + 1) * LANE] = tmp_ref[pl.ds(l, n_out, stride=CMP_STRIDE), :].astype(
                ckv_out.dtype)
    kv_out[...] = jnp.concatenate(
        [rope(blk(4)), rope(blk(5)), blk(6), blk(7), rope(blk(8)), rope(blk(9)), blk(10), blk(11)],
        axis=1).astype(kv_out.dtype)


def _rope(proj3, pos3, freq2, ts=512):
    b, s, _ = proj3.shape
    kvspec = lambda k: pl.BlockSpec((None, ts, 4 * LANE), lambda i, j, k=k: (i, j, CB_KV // 4 + k))
    return pl.pallas_call(
        _rope_kernel,
        grid=(b, s // ts),
        in_specs=[pl.BlockSpec((None, ts, NSA_HEADS * LANE), lambda i, j: (i, j, CB_NQ // NSA_HEADS)),
                  kvspec(0), kvspec(1), kvspec(2),
                  pl.BlockSpec((None, ts, 1), lambda i, j: (i, j, 0)),
                  pl.BlockSpec((1, NSA_DH), lambda i, j: (0, 0))],
        out_specs=[pl.BlockSpec((None, ts, NSA_HEADS * LANE), lambda i, j: (i, j, 0)),
                   pl.BlockSpec((None, 4, ts // CMP_STRIDE, CMP_STRIDE * LANE), lambda i, j: (i, 0, j, 0)),
                   pl.BlockSpec((None, ts, 8 * LANE), lambda i, j: (i, j, 0))],
        out_shape=[jax.ShapeDtypeStruct((b, s, NSA_HEADS * LANE), BF16),
                   jax.ShapeDtypeStruct((b, 4, s // CMP_STRIDE, CMP_STRIDE * LANE), BF16),
                   jax.ShapeDtypeStruct((b, s, 8 * LANE), BF16)],
        scratch_shapes=[pltpu.VMEM((ts, LANE), F32)],
        compiler_params=_cp(("parallel", "parallel")),
        name="rope",
    )(proj3, proj3, proj3, proj3, pos3, freq2)


def _compress_kernel(x_ref, pe_ref, w1_ref, b1_ref, w2_ref, o_ref):
    x = x_ref[...]
    half = CMP_STRIDE * NSA_DH
    first = jnp.dot(x, w1_ref[:half, :], preferred_element_type=F32)
    second = jnp.dot(x, w1_ref[half:, :], preferred_element_type=F32)
    n_rows = x.shape[0]
    second = pltpu.roll(second, n_rows - 1, 0)
    pe = jnp.broadcast_to(pe_ref[...], (8, pe_ref.shape[1]))
    bias = jnp.dot(pe, w1_ref[...], preferred_element_type=F32)[0:1] + b1_ref[...]
    hid = first + second + bias
    gelu = 0.5 * hid * (1.0 + jnp.tanh(0.7978845608028654 * (hid + 0.044715 * hid * hid * hid)))
    o_ref[...] = _bdot(gelu, w2_ref[...]).astype(o_ref.dtype)


def _compress(ckv_r, pe, w1, b1, w2):
    b, _, nr, wd = ckv_r.shape
    return pl.pallas_call(
        _compress_kernel,
        grid=(b, 4),
        in_specs=[pl.BlockSpec((None, None, nr, wd), lambda i, j: (i, j, 0, 0)),
                  pl.BlockSpec((None, 1, CMP_BLOCK * NSA_DH), lambda i, j: (j // 2, 0, 0)),
                  pl.BlockSpec((None, CMP_BLOCK * NSA_DH, CMP_HIDDEN), lambda i, j: (j // 2, 0, 0)),
                  pl.BlockSpec((None, 1, CMP_HIDDEN), lambda i, j: (j // 2, 0, 0)),
                  pl.BlockSpec((None, CMP_HIDDEN, NSA_DH), lambda i, j: (j // 2, 0, 0))],
        out_specs=pl.BlockSpec((None, None, nr, NSA_DH), lambda i, j: (i, j, 0, 0)),
        out_shape=jax.ShapeDtypeStruct((b, 4, nr, NSA_DH), BF16),
        compiler_params=_cp(("parallel", "arbitrary")),
        name="compress",
    )(ckv_r, pe, w1, b1, w2)


def _nsa_kernel(q_ref, kc_ref, vc_ref, ks_ref, vs_ref, kw_ref, vw_ref, eh_ref, sm_ref, ovt_ref, o_ref, *, tq):
    g = pl.program_id(1)
    qi = pl.program_id(2)
    t0 = qi * tq
    heads = range(NSA_HPG)
    nb = eh_ref.shape[0] // SEL_BLOCK
    qs = [q_ref[:, hh * LANE:(hh + 1) * LANE] for hh in heads]
    transpose_bf16 = lambda x: x.astype(F32).T.astype(BF16)
    col_max = lambda s: jnp.max(s, 0, keepdims=True)

    cmask = (lax.broadcasted_iota(jnp.int32, (LANE, tq), 0) * CMP_STRIDE + (CMP_BLOCK - 1)
             <= t0 + lax.broadcasted_iota(jnp.int32, (LANE, tq), 1))
    kc = kc_ref[...]
    s_c = [jnp.where(cmask, _bdot_nt(kc, x), NEG) for x in qs]
    e_c = [jnp.where(cmask, jnp.exp(s - col_max(s)), 0.0) for s in s_c]
    p_c = [e * (1.0 / jnp.maximum(jnp.sum(e, 0, keepdims=True), 1e-30)) for e in e_c]
    vc_t = transpose_bf16(vc_ref[...])
    o_c = [_bdot(vc_t, p) for p in p_c]

    p_sum = p_c[0]
    for p in p_c[1:]:
        p_sum = p_sum + p
    ovt = ovt_ref[...]
    imp_t = sum(jnp.dot(ovt, part, preferred_element_type=F32) for part in _split3(p_sum))[:nb]
    blk = lax.broadcasted_iota(jnp.int32, (nb, tq), 0)
    t = t0 + lax.broadcasted_iota(jnp.int32, (nb, tq), 1)
    cur = t >> SEL_SHIFT
    forced = (blk == 0) | (blk == cur) | (blk == cur - 1)
    val = jnp.where(forced, jnp.inf, jnp.where(blk * SEL_BLOCK <= t, imp_t, -jnp.inf))
    rank = jnp.zeros((nb, tq), F32)
    for i in range(nb):
        vi = val[i:i + 1, :]
        rank = rank + jnp.where((vi > val) | ((vi == val) & (blk > i)), 1.0, 0.0)
    drop_t = jnp.where(rank < SEL_TOPK, 0.0, MASK_BIG)
    drop = jnp.concatenate([drop_t, jnp.zeros((LANE - nb, tq), F32)], axis=0).T
    drop_b = (-drop).astype(BF16)
    q_aug = [jnp.concatenate([x, drop_b], axis=1) for x in qs]

    ones_col = jnp.where(lax.broadcasted_iota(jnp.int32, (tq, LANE), 1) == 0, 1.0, 0.0).astype(BF16)
    key_i = lax.broadcasted_iota(jnp.int32, (tq, tq), 0)
    qry_i = lax.broadcasted_iota(jnp.int32, (tq, tq), 1)
    key_le_query = key_i <= qry_i
    normalise = lambda acc: acc[:NSA_DH, :] * (1.0 / jnp.maximum(acc[NSA_DH:NSA_DH + 1, :], 1e-30))

    prev0 = pl.multiple_of(jnp.maximum(qi - 1, 0) * tq, tq)
    own0 = pl.multiple_of(t0, tq)
    in_prev = key_i > qry_i + jnp.where(qi > 0, 0, tq)
    kw_prev, kw_own = kw_ref[pl.ds(prev0, tq), :], kw_ref[pl.ds(own0, tq), :]
    v_w_t = transpose_bf16(jnp.concatenate(
        [jnp.concatenate([vw_ref[pl.ds(prev0, tq), :], ones_col], axis=1),
         jnp.concatenate([vw_ref[pl.ds(own0, tq), :], ones_col], axis=1)], axis=0))
    s_w = [jnp.concatenate([jnp.where(in_prev, _bdot_nt(kw_prev, x), NEG),
                            jnp.where(key_le_query, _bdot_nt(kw_own, x), NEG)], axis=0) for x in qs]
    e_w = [jnp.exp(s - col_max(s)) for s in s_w]
    o_w = [normalise(_bdot(v_w_t, e)) for e in e_w]

    def sel_chunk(c, carry, diagonal):
        m_i, acc = carry
        k0 = pl.multiple_of(c * tq, tq)
        k_aug = jnp.concatenate([ks_ref[pl.ds(k0, tq), :], eh_ref[pl.ds(k0, tq), :]], axis=1)
        v_t = transpose_bf16(jnp.concatenate([vs_ref[pl.ds(k0, tq), :], ones_col], axis=1))
        s = [lax.dot_general(k_aug, x, (((1,), (1,)), ((), ())), preferred_element_type=F32) for x in q_aug]
        if diagonal:
            s = [jnp.where(key_le_query, x, -MASK_BIG) for x in s]
        m_new = [jnp.maximum(m, col_max(x)) for m, x in zip(m_i, s)]
        p = [jnp.exp(x - m).astype(BF16) for x, m in zip(s, m_new)]
        acc = [jnp.exp(m - mn) * a + jnp.dot(v_t, x, preferred_element_type=F32)
               for m, mn, a, x in zip(m_i, m_new, acc, p)]
        return tuple(m_new), tuple(acc)

    init = (tuple(jnp.full((1, tq), -MASK_BIG, F32) for _ in heads),
            tuple(jnp.zeros((2 * NSA_DH, tq), F32) for _ in heads))
    carry = lax.fori_loop(0, qi, functools.partial(sel_chunk, diagonal=False), init)
    _, acc_s = sel_chunk(qi, carry, True)
    o_s = [normalise(a) for a in acc_s]

    gates_t = _sigmoid(sm_ref[...]).T
    outs = []
    for hh in heads:
        def gate(branch):
            rows = [gates_t[SM_NGATE + (grp * NSA_HPG + hh) * 3 + branch][None, :] for grp in range(NSA_KV_HEADS)]
            return jnp.where(g == 0, rows[0], rows[1])
        assert NSA_KV_HEADS == 2
        outs.append((gate(0) * o_c[hh] + gate(1) * o_s[hh] + gate(2) * o_w[hh]).T)
    o_ref[...] = jnp.concatenate(outs, axis=1).astype(o_ref.dtype)


def _nsa(q_r, kvc, kv_r, small3):
    b, s, _ = q_r.shape
    nr = kvc.shape[2]
    nb = s // SEL_BLOCK
    tq = WINDOW
    assert s % tq == 0 and nb <= LANE and nr <= LANE
    gw = NSA_HPG * LANE
    eh = jnp.asarray(np.arange(s)[:, None] // SEL_BLOCK == np.arange(LANE)[None, :], BF16)
    ovt = jnp.asarray(_overlap_matrix(s).T, BF16)
    kvspec = lambda off: pl.BlockSpec((None, s, LANE), lambda i, g, j, off=off: (i, 0, off + g))
    cspec = lambda off: pl.BlockSpec((None, None, nr, NSA_DH), lambda i, g, j, off=off: (i, off + g, 0, 0))
    return pl.pallas_call(
        functools.partial(_nsa_kernel, tq=tq),
        grid=(b, NSA_KV_HEADS, s // tq),
        in_specs=[pl.BlockSpec((None, tq, gw), lambda i, g, j: (i, j, g)),
                  cspec(0), cspec(2),
                  kvspec(0), kvspec(2), kvspec(4), kvspec(6),
                  pl.BlockSpec((s, LANE), lambda i, g, j: (0, 0)),
                  pl.BlockSpec((None, tq, LANE), lambda i, g, j: (i, j, 0)),
                  pl.BlockSpec((LANE, LANE), lambda i, g, j: (0, 0))],
        out_specs=pl.BlockSpec((None, tq, gw), lambda i, g, j: (i, j, g)),
        out_shape=jax.ShapeDtypeStruct((b, s, NSA_HEADS * NSA_DH), BF16),
        compiler_params=_cp(("parallel", "parallel", "arbitrary")),
        name="nsa",
    )(q_r, kvc, kvc, kv_r, kv_r, kv_r, kv_r, eh, small3, ovt)


def _merge_xattn_kernel(ya_ref, yb_ref, mg_ref, x_ref, wout_ref, g1_ref, b1_ref,
                        kv_ref, wq_ref, wo_ref, g_ref, b_ref, o_ref):
    mg = mg_ref[...].astype(F32)
    merged = (_sigmoid(mg[:, :D_MODEL]) * ya_ref[...].astype(F32)
              + _sigmoid(mg[:, D_MODEL:]) * yb_ref[...].astype(F32))
    h = _layer_norm(DN_ALPHA * x_ref[...] + _bdot(merged, wout_ref[...]), g1_ref[...], b1_ref[...])
    q = _bdot(h, wq_ref[...]).astype(BF16)
    kv = kv_ref[...]
    outs = []
    for hh in range(XA_HEADS):
        sl = slice(hh * XA_DH, (hh + 1) * XA_DH)
        s = _bdot_nt(q[:, sl], kv[:, sl]) * (XA_DH ** -0.5)
        m = jnp.max(s, -1, keepdims=True)
        e = jnp.exp(s - m)
        pv = _bdot(e, kv[:, D_MODEL + hh * XA_DH:D_MODEL + (hh + 1) * XA_DH])
        outs.append(pv * (1.0 / jnp.sum(e, -1, keepdims=True)))
    o = jnp.concatenate(outs, axis=1)
    o_ref[...] = _layer_norm(DN_ALPHA * h + _bdot(o, wo_ref[...]), g_ref[...], b_ref[...])


def _merge_xattn(y_a, y_b, proj3, x3, w_out, g1, b1, kv3, wq, wo, g2, b2, tm=512):
    bsz, s, _ = x3.shape
    m = kv3.shape[1]
    full = lambda shape: pl.BlockSpec(shape, lambda i, j: (0, 0))
    row = pl.BlockSpec((None, tm, D_MODEL), lambda i, j: (i, j, 0))
    return pl.pallas_call(
        _merge_xattn_kernel,
        grid=(bsz, s // tm),
        in_specs=[row, row,
                  pl.BlockSpec((None, tm, 2 * D_MODEL), lambda i, j: (i, j, CB_MG * LANE // (2 * D_MODEL))),
                  row, full((D_MODEL, D_MODEL)), full((1, D_MODEL)), full((1, D_MODEL)),
                  pl.BlockSpec((None, m, 2 * D_MODEL), lambda i, j: (i, 0, 0)),
                  full((D_MODEL, D_MODEL)), full((D_MODEL, D_MODEL)), full((1, D_MODEL)), full((1, D_MODEL))],
        out_specs=row,
        out_shape=jax.ShapeDtypeStruct((bsz, s, D_MODEL), F32),
        compiler_params=_cp(("parallel", "parallel")),
        name="merge_xattn",
    )(y_a, y_b, proj3, x3, w_out, g1, b1, kv3, wq, wo, g2, b2)


def _route(x, wrt_h, wrt_l, br):
    xh = x.astype(BF16)
    xl = (x - xh.astype(F32)).astype(BF16)
    logits = _bdot_nt(wrt_h, xh) + _bdot_nt(wrt_l, xh) + _bdot_nt(wrt_h, xl) + br
    row = lax.broadcasted_iota(jnp.int32, logits.shape, 0)
    first_row = lambda hit: jnp.min(jnp.where(hit, row, LANE), 0, keepdims=True)
    gmask = row < N_GROUPS
    gl = jnp.where(gmask, logits, -jnp.inf)
    gmax = jnp.max(gl, 0, keepdims=True)
    g_idx = first_row(gl == gmax)
    p_top = 1.0 / jnp.sum(jnp.where(gmask, jnp.exp(gl - gmax), 0.0), 0, keepdims=True)
    e_row = row - N_GROUPS
    emask = (e_row >= 0) & (e_row < N_EXPERTS) & ((e_row >> 3) == g_idx)
    el = jnp.where(emask, logits, -jnp.inf)
    ee = jnp.where(emask, jnp.exp(el - jnp.max(el, 0, keepdims=True)), 0.0)
    pe = ee / jnp.sum(ee, 0, keepdims=True)
    pv = jnp.where(emask, pe, -jnp.inf)
    m1 = jnp.max(pv, 0, keepdims=True)
    i1 = first_row(pv == m1)
    pv2 = jnp.where(row == i1, -jnp.inf, pv)
    m2 = jnp.max(pv2, 0, keepdims=True)
    i2 = first_row(pv2 == m2)
    denom = m1 + m2
    return i1, i2, p_top * m1 / denom, p_top * m2 / denom


META_E1, META_E2, META_R1, META_R2, META_W1, META_W2 = 0, 1, 2, 3, 4, 5
META_ROWS = 8


def _router_kernel(h_ref, wrh_ref, wrl_ref, br_ref, meta_ref, cnt_ref, carry_ref):
    @pl.when(pl.program_id(0) == 0)
    def _():
        carry_ref[...] = jnp.zeros_like(carry_ref)

    tm = h_ref.shape[0]
    i1, i2, w1, w2 = _route(h_ref[...], wrh_ref[...], wrl_ref[...], br_ref[...])
    row = lax.broadcasted_iota(jnp.int32, (LANE, tm), 0)
    onehot = jnp.where((row == i1) | (row == i2), 1.0, 0.0)
    earlier = jnp.where(lax.broadcasted_iota(jnp.int32, (tm, tm), 0) < lax.broadcasted_iota(jnp.int32, (tm, tm), 1),
                        1.0, 0.0).astype(BF16)
    before = jnp.dot(onehot.astype(BF16), earlier, preferred_element_type=F32) + carry_ref[...]
    r1 = jnp.sum(jnp.where(row == i1, before, 0.0), 0, keepdims=True)
    r2 = jnp.sum(jnp.where(row == i2, before, 0.0), 0, keepdims=True)
    carry_ref[...] += jnp.sum(onehot, axis=1, keepdims=True)
    cnt_ref[...] = carry_ref[...]
    rows = {META_E1: (i1 - N_GROUPS).astype(F32), META_E2: (i2 - N_GROUPS).astype(F32),
            META_R1: r1, META_R2: r2, META_W1: w1, META_W2: w2}
    zero = jnp.zeros((1, tm), F32)
    meta_ref[...] = jnp.concatenate([rows.get(r, zero) for r in range(META_ROWS)], axis=0)


def _router(h2, wrt_h, wrt_l, br, tm=1024):
    n = h2.shape[0]
    full = lambda shape: pl.BlockSpec(shape, lambda i: (0, 0))
    return pl.pallas_call(
        _router_kernel,
        grid=(n // tm,),
        in_specs=[pl.BlockSpec((tm, D_MODEL), lambda i: (i, 0)),
                  full((LANE, D_MODEL)), full((LANE, D_MODEL)), full((LANE, 1))],
        out_specs=[pl.BlockSpec((META_ROWS, tm), lambda i: (0, i)), full((LANE, 1))],
        out_shape=[jax.ShapeDtypeStruct((META_ROWS, n), F32), jax.ShapeDtypeStruct((LANE, 1), F32)],
        scratch_shapes=[pltpu.VMEM((LANE, 1), F32)],
        compiler_params=_cp(("arbitrary",)),
        name="moe_router",
    )(h2, wrt_h, wrt_l, br)


def _positions_kernel(meta_ref, cnt_ref, pos_ref):
    tm = meta_ref.shape[1]
    below = jnp.where(lax.broadcasted_iota(jnp.int32, (LANE, LANE), 1) < lax.broadcasted_iota(jnp.int32, (LANE, LANE), 0),
                      1.0, 0.0).astype(BF16)
    cnt = jnp.broadcast_to(cnt_ref[...], (LANE, LANE))
    start = sum(jnp.dot(below, p, preferred_element_type=F32) for p in _split3(cnt))
    start = jnp.concatenate([start] * (tm // LANE), axis=1)
    meta = meta_ref[...]
    row = lax.broadcasted_iota(jnp.int32, (LANE, tm), 0)
    pos = []
    for r_e, r_r in ((META_E1, META_R1), (META_E2, META_R2)):
        e_row = meta[r_e:r_e + 1, :].astype(jnp.int32) + N_GROUPS
        seg = jnp.sum(jnp.where(row == e_row, start, 0.0), 0, keepdims=True)
        pos.append((seg + meta[r_r:r_r + 1, :]).astype(jnp.int32))
    pos_ref[...] = jnp.concatenate(pos + [jnp.zeros((META_ROWS - 2, tm), jnp.int32)], axis=0)


def _positions(meta, counts, tm=2048):
    n = meta.shape[1]
    return pl.pallas_call(
        _positions_kernel,
        grid=(n // tm,),
        in_specs=[pl.BlockSpec((META_ROWS, tm), lambda i: (0, i)), pl.BlockSpec((LANE, 1), lambda i: (0, 0))],
        out_specs=pl.BlockSpec((META_ROWS, tm), lambda i: (0, i)),
        out_shape=jax.ShapeDtypeStruct((META_ROWS, n), jnp.int32),
        compiler_params=_cp(("parallel",)),
        name="moe_positions",
    )(meta, counts)


def _row_copies(n_rows, pos_refs, make_copy):
    def body(r, carry):
        for p in pos_refs:
            make_copy(r, p[r]).start()
        return carry

    lax.fori_loop(0, n_rows, body, 0, unroll=8)


def _dispatch_kernel(pos1_ref, pos2_ref, h_ref, xs_ref, sem):
    tm = h_ref.shape[0]
    _row_copies(tm, (pos1_ref, pos2_ref),
                lambda r, p: pltpu.make_async_copy(h_ref.at[pl.ds(r, 1)], xs_ref.at[pl.ds(p, 1)], sem))
    for _ in range(2):
        pltpu.make_async_copy(h_ref, xs_ref.at[pl.ds(0, tm)], sem).wait()


def _dispatch(h2, pos1, pos2, n_rows, tm=512):
    n = h2.shape[0]
    smem = pl.BlockSpec((tm,), lambda i: (i,), memory_space=pltpu.SMEM)
    return pl.pallas_call(
        _dispatch_kernel,
        grid=(n // tm,),
        in_specs=[smem, smem, pl.BlockSpec((tm, D_MODEL), lambda i: (i, 0))],
        out_specs=pl.BlockSpec(memory_space=pl.ANY),
        out_shape=jax.ShapeDtypeStruct((n_rows, D_MODEL), F32),
        scratch_shapes=[pltpu.SemaphoreType.DMA],
        compiler_params=_cp(("arbitrary",)),
        name="moe_dispatch",
    )(pos1, pos2, h2)


def _ffn_kernel(tile_ref, exp_ref, lo_ref, hi_ref, first_ref, xs_ref, wg_ref, wu_ref, wd_ref, ys_ref):
    i = pl.program_id(0)
    tm = xs_ref.shape[0]
    rows = lax.broadcasted_iota(jnp.int32, (tm, 1), 0)
    mine = (rows >= lo_ref[i]) & (rows < hi_ref[i])
    x = xs_ref[...].astype(BF16)
    gate = jnp.dot(x, wg_ref[...].astype(BF16), preferred_element_type=F32)
    up = jnp.dot(x, wu_ref[...].astype(BF16), preferred_element_type=F32)
    y = _bdot(_silu(gate) * up, wd_ref[...])

    @pl.when(first_ref[i] == 1)
    def _():
        ys_ref[...] = jnp.where(mine, y, 0.0)

    @pl.when(first_ref[i] == 0)
    def _():
        ys_ref[...] = jnp.where(mine, y, ys_ref[...])


def _ffn(xs, items, wg, wu, wd, tm):
    n_rows = xs.shape[0]
    wspec = lambda shape: pl.BlockSpec((None,) + shape, lambda i, tl, ex, lo, hi, fi: (ex[i], 0, 0))
    row_tile = pl.BlockSpec((tm, D_MODEL), lambda i, tl, ex, lo, hi, fi: (tl[i], 0))
    return pl.pallas_call(
        _ffn_kernel,
        grid_spec=pltpu.PrefetchScalarGridSpec(
            num_scalar_prefetch=5,
            grid=(items[0].shape[0],),
            in_specs=[row_tile, wspec((D_MODEL, D_FF)), wspec((D_MODEL, D_FF)), wspec((D_FF, D_MODEL))],
            out_specs=row_tile),
        out_shape=jax.ShapeDtypeStruct((n_rows, D_MODEL), F32),
        compiler_params=_cp(("arbitrary",)),
        name="moe_ffn",
    )(*items, xs, wg, wu, wd)


def _combine_kernel(pos1_ref, pos2_ref, nxt1_ref, nxt2_ref, h_ref, meta_ref, ys_ref, g_ref, b_ref, o_ref,
                    buf_ref, sem):
    i = pl.program_id(0)
    tm = h_ref.shape[0]
    slot = i % 2

    def start_gather(p1_ref, p2_ref, s):
        def copy(k, pos_ref):
            return lambda r, p: pltpu.make_async_copy(ys_ref.at[pl.ds(p, 1)], buf_ref.at[s, k, pl.ds(r, 1)],
                                                      sem.at[s])
        def body(r, carry):
            copy(0, p1_ref)(r, p1_ref[r]).start()
            copy(1, p2_ref)(r, p2_ref[r]).start()
            return carry
        lax.fori_loop(0, tm, body, 0, unroll=8)

    @pl.when(i == 0)
    def _():
        start_gather(pos1_ref, pos2_ref, 0)

    @pl.when(i + 1 < pl.num_programs(0))
    def _():
        start_gather(nxt1_ref, nxt2_ref, 1 - slot)

    for k in range(2):
        pltpu.make_async_copy(ys_ref.at[pl.ds(0, tm)], buf_ref.at[slot, k], sem.at[slot]).wait()
    meta = jnp.concatenate([meta_ref[...], jnp.zeros((LANE - META_ROWS, tm), F32)], axis=0).T
    ffn = (meta[:, META_W1:META_W1 + 1] * buf_ref[slot, 0] + meta[:, META_W2:META_W2 + 1] * buf_ref[slot, 1])
    o_ref[...] = _layer_norm(DN_ALPHA * h_ref[...] + ffn, g_ref[...], b_ref[...])


def _combine(h2, meta, ys, pos1, pos2, g, b, tm=256):
    n = h2.shape[0]
    last = n // tm - 1
    smem = pl.BlockSpec((tm,), lambda i: (i,), memory_space=pltpu.SMEM)
    smem_next = pl.BlockSpec((tm,), lambda i: (jnp.minimum(i + 1, last),), memory_space=pltpu.SMEM)
    full = lambda shape: pl.BlockSpec(shape, lambda i: (0, 0))
    return pl.pallas_call(
        _combine_kernel,
        grid=(n // tm,),
        in_specs=[smem, smem, smem_next, smem_next, pl.BlockSpec((tm, D_MODEL), lambda i: (i, 0)),
                  pl.BlockSpec((META_ROWS, tm), lambda i: (0, i)),
                  pl.BlockSpec(memory_space=pl.ANY), full((1, D_MODEL)), full((1, D_MODEL))],
        out_specs=pl.BlockSpec((tm, D_MODEL), lambda i: (i, 0)),
        out_shape=jax.ShapeDtypeStruct((n, D_MODEL), F32),
        scratch_shapes=[pltpu.VMEM((2, 2, tm, D_MODEL), F32), pltpu.SemaphoreType.DMA((2,))],
        compiler_params=_cp(("arbitrary",)),
        name="moe_combine_ln3",
    )(pos1, pos2, pos1, pos2, h2, meta, ys, g, b)


def _moe(h2, wr_h, wr_l, br, wg, wu, wd, g, b, tile=512):
    n = h2.shape[0]
    i32 = jnp.int32
    meta, counts = _router(h2, wr_h, wr_l, br)
    cnt = counts[N_GROUPS:N_GROUPS + N_EXPERTS, 0].astype(i32)
    seg_end = jnp.cumsum(cnt)
    seg_start = seg_end - cnt
    pos = _positions(meta, counts)
    pos1, pos2 = pos[0], pos[1]
    first_tile = seg_start // tile
    n_items_e = jnp.where(cnt > 0, (seg_end - 1) // tile - first_tile + 1, 0)
    item_end = jnp.cumsum(n_items_e)
    n_items = (2 * n) // tile + N_EXPERTS - 1
    idx = jnp.minimum(jnp.arange(n_items, dtype=i32), item_end[-1] - 1)
    exp = jnp.sum(idx[:, None] >= item_end[None, :], axis=1).astype(i32)
    til = first_tile[exp] + idx - (item_end - n_items_e)[exp]
    lo = jnp.maximum(seg_start[exp], til * tile) - til * tile
    hi = jnp.minimum(seg_end[exp], (til + 1) * tile) - til * tile
    repeat = jnp.arange(n_items, dtype=i32) >= item_end[-1]
    hi = jnp.where(repeat, lo, hi)
    first = jnp.concatenate([jnp.ones((1,), i32), (til[1:] != til[:-1]).astype(i32)])
    xs = _dispatch(h2, pos1, pos2, 2 * n)
    ys = _ffn(xs, (til.astype(i32), exp, lo.astype(i32), hi.astype(i32), first), wg, wu, wd, tile)
    return _combine(h2, meta, ys, pos1, pos2, g, b)


def _regroup_w_in(w):
    sizes = (1024, 1024, 1024, 1024, 8, 8, 1024, 256, 256, 256, 256, 256, 256, 24, 2048)
    offs = [0]
    for sz in sizes:
        offs.append(offs[-1] + sz)
    seg = lambda i: w[:, offs[i]:offs[i + 1]]
    small = jnp.concatenate([seg(4), seg(5), seg(13)], axis=1)
    small = jnp.pad(small, ((0, 0), (0, LANE - small.shape[1])))
    big = jnp.concatenate([seg(0), seg(1), seg(2), seg(3), seg(14), seg(6), seg(7), seg(8), seg(9), seg(10),
                           seg(11), seg(12)], axis=1)
    return big.astype(BF16), small.astype(BF16)


def _overlap_matrix(s):
    nb = s // SEL_BLOCK
    c0 = np.arange(LANE) * CMP_STRIDE
    s0 = np.arange(LANE) * SEL_BLOCK
    ov = np.minimum(c0[:, None] + CMP_BLOCK, s0[None, :] + SEL_BLOCK) - np.maximum(c0[:, None], s0[None, :])
    ov = np.maximum(ov, 0).astype(np.float32) / CMP_BLOCK
    nc = (s - CMP_BLOCK) // CMP_STRIDE + 1
    keep = (np.arange(LANE)[:, None] < nc) & (np.arange(LANE)[None, :] < nb)
    return np.where(keep, ov, 0.0).astype(np.float32)


def _layer(h, mem, pos3, freq2, w_in, conv_w, a_log, dt_bias, norm_w, cmp_pe, cmp_w1, cmp_b1, cmp_w2, w_out,
           ln1_g, ln1_b, xa_wq, xa_wkv, xa_wo, ln2_g, ln2_b, w_group, b_group, w_expert, b_expert,
           w_gate, w_up, w_down, ln3_g, ln3_b):
    b, s, d = h.shape
    n = b * s
    x2 = h.reshape(n, d)
    proj, small = _inproj(x2, *_regroup_w_in(w_in))
    proj3 = proj.reshape(b, s, -1)
    small3 = small.reshape(b, s, LANE)

    y_a = _gdn(proj3, small3, conv_w, a_log, dt_bias, norm_w)

    q_r, ckv_r, kv_r = _rope(proj3, pos3, freq2)
    kvc = _compress(ckv_r, cmp_pe.reshape(2, 1, CMP_BLOCK * NSA_DH).astype(BF16), cmp_w1.astype(BF16),
                    cmp_b1.reshape(2, 1, CMP_HIDDEN), cmp_w2.astype(BF16))
    y_b = _nsa(q_r, kvc, kv_r, small3)

    row = lambda v: v.reshape(1, -1)
    m = mem.shape[1]
    kv = _matmul(mem.reshape(b * m, d).astype(BF16), xa_wkv.astype(BF16), BF16, tm=512, tn=512)
    h2 = _merge_xattn(y_a, y_b, proj3, h, w_out.astype(BF16), row(ln1_g), row(ln1_b),
                      kv.reshape(b, m, 2 * d), xa_wq.astype(BF16), xa_wo.astype(BF16), row(ln2_g), row(ln2_b))

    wrt = jnp.pad(jnp.concatenate([w_group, w_expert], axis=1).T, ((0, LANE - N_GROUPS - N_EXPERTS), (0, 0)))
    wrt_h = wrt.astype(BF16)
    wrt_l = (wrt - wrt_h.astype(F32)).astype(BF16)
    br = jnp.pad(jnp.concatenate([b_group, b_expert]), (0, LANE - N_GROUPS - N_EXPERTS)).reshape(LANE, 1)
    h3 = _moe(h2.reshape(n, d), wrt_h, wrt_l, br, w_gate, w_up, w_down, row(ln3_g), row(ln3_b))
    return h3.reshape(b, s, d)


def kernel(x, mem, positions, w_in, gdn_conv_w, gdn_a_log, gdn_dt_bias, gdn_norm_w, cmp_pe, cmp_w1, cmp_b1, cmp_w2, w_out, ln1_g, ln1_b, xa_wq, xa_wkv, xa_wo, ln2_g, ln2_b, moe_w_group, moe_b_group, moe_w_expert, moe_b_expert, moe_w_gate, moe_w_up, moe_w_down, ln3_g, ln3_b):
    half = NSA_DH // 2
    inv_freq = ROPE_THETA ** (-jnp.arange(half, dtype=F32) / half)
    freq2 = jnp.concatenate([inv_freq, inv_freq]).reshape(1, NSA_DH)
    pos3 = positions.astype(F32)[..., None]
    h = x
    for l in range(DEPTH):
        h = _layer(h, mem, pos3, freq2, w_in[l], gdn_conv_w[l], gdn_a_log[l], gdn_dt_bias[l], gdn_norm_w[l],
                   cmp_pe[l], cmp_w1[l], cmp_b1[l], cmp_w2[l], w_out[l], ln1_g[l], ln1_b[l],
                   xa_wq[l], xa_wkv[l], xa_wo[l], ln2_g[l], ln2_b[l], moe_w_group[l], moe_b_group[l],
                   moe_w_expert[l], moe_b_expert[l], moe_w_gate[l], moe_w_up[l], moe_w_down[l],
                   ln3_g[l], ln3_b[l])
    return h
```

```python
import functools

import jax
import jax.numpy as jnp
import numpy as np
from jax import lax
from jax.experimental import pallas as pl
from jax.experimental.pallas import tpu as pltpu

F32 = jnp.float32
BF16 = jnp.bfloat16

D_MODEL = 1024
LANE = 128
GDN_HEADS = 8
GDN_D = 128
GDN_CONV = 4
NSA_HEADS = 8
NSA_KV_HEADS = 2
NSA_HPG = NSA_HEADS // NSA_KV_HEADS
NSA_DH = 128
CMP_BLOCK = 32
CMP_STRIDE = 16
CMP_HIDDEN = 256
SEL_BLOCK = 64
SEL_SHIFT = 6
SEL_TOPK = 8
WINDOW = 256
XA_HEADS = 4
XA_DH = 256
N_GROUPS = 4
EXPERTS_PER_GROUP = 8
GROUP_SHIFT = 3
N_EXPERTS = N_GROUPS * EXPERTS_PER_GROUP
D_FF = 256
DEPTH = 1
DN_ALPHA = (2.0 * DEPTH) ** 0.25
LN_EPS = 1e-5
RMS_EPS = 1e-6
ROPE_THETA = 10000.0
NEG = -1e30
MASK_BIG = 1e30

CB_Q, CB_K, CB_V, CB_Z = 0, 8, 16, 24
CB_MG = 32
CB_NQ = 48
CB_KV = 56
SM_BETA, SM_DECAY, SM_NGATE = 0, 8, 16

VMEM_LIMIT = 48 * 1024 * 1024


def _cp(sem, vmem=VMEM_LIMIT):
    return pltpu.CompilerParams(dimension_semantics=sem, vmem_limit_bytes=vmem)


def _bdot(a, b):
    return jnp.dot(a.astype(BF16), b.astype(BF16), preferred_element_type=F32)


def _bdot_nt(a, b):
    return lax.dot_general(a.astype(BF16), b.astype(BF16), (((1,), (1,)), ((), ())),
                           preferred_element_type=F32)


def _bdot_tn(a, b):
    return lax.dot_general(a.astype(BF16), b.astype(BF16), (((0,), (0,)), ((), ())),
                           preferred_element_type=F32)


def _split3(x):
    h = x.astype(BF16)
    r = x - h.astype(F32)
    m = r.astype(BF16)
    l = (r - m.astype(F32)).astype(BF16)
    return h, m, l


def _sigmoid(x):
    return 1.0 / (1.0 + jnp.exp(-x))


def _silu(x):
    return x * _sigmoid(x)


def _layer_norm(x, g, b):
    mu = jnp.mean(x, -1, keepdims=True)
    xc = x - mu
    var = jnp.mean(xc * xc, -1, keepdims=True)
    return xc * lax.rsqrt(var + LN_EPS) * g + b


def _mm_kernel(x_ref, w_ref, o_ref):
    o_ref[...] = jnp.dot(x_ref[...], w_ref[...], preferred_element_type=F32).astype(o_ref.dtype)


def _matmul(x, w, out_dtype, tm, tn):
    m, k = x.shape
    n = w.shape[1]
    return pl.pallas_call(
        _mm_kernel,
        grid=(m // tm, n // tn),
        in_specs=[pl.BlockSpec((tm, k), lambda i, j: (i, 0)),
                  pl.BlockSpec((k, tn), lambda i, j: (0, j))],
        out_specs=pl.BlockSpec((tm, tn), lambda i, j: (i, j)),
        out_shape=jax.ShapeDtypeStruct((m, n), out_dtype),
        compiler_params=_cp(("parallel", "parallel")),
        name="matmul",
    )(x, w)


def _inproj_kernel(x_ref, w_ref, ws_ref, o_ref, os_ref, xb_ref):
    @pl.when(pl.program_id(1) == 0)
    def _():
        xb_ref[...] = x_ref[...].astype(BF16)
        os_ref[...] = jnp.dot(xb_ref[...], ws_ref[...], preferred_element_type=F32)

    o_ref[...] = jnp.dot(xb_ref[...], w_ref[...], preferred_element_type=F32).astype(o_ref.dtype)


def _inproj(x2, w_big, w_small, tm=2048, tn=512):
    m, k = x2.shape
    n = w_big.shape[1]
    return pl.pallas_call(
        _inproj_kernel,
        grid=(m // tm, n // tn),
        in_specs=[pl.BlockSpec((tm, k), lambda i, j: (i, 0)),
                  pl.BlockSpec((k, tn), lambda i, j: (0, j)),
                  pl.BlockSpec((k, LANE), lambda i, j: (0, 0))],
        out_specs=[pl.BlockSpec((tm, tn), lambda i, j: (i, j)),
                   pl.BlockSpec((tm, LANE), lambda i, j: (i, 0))],
        out_shape=[jax.ShapeDtypeStruct((m, n), BF16), jax.ShapeDtypeStruct((m, LANE), F32)],
        scratch_shapes=[pltpu.VMEM((tm, k), BF16)],
        compiler_params=_cp(("parallel", "arbitrary")),
        name="inproj",
    )(x2, w_big, w_small)


def _gdn_kernel(q_ref, k_ref, v_ref, z_ref, sm_ref, cwq_ref, cwk_ref, cwv_ref,
                alog_ref, dtb_ref, nw_ref, o_ref, state_ref, tail_ref, *, c_len):
    n_rows = q_ref.shape[0]
    chunks = range(n_rows // c_len)

    @pl.when(pl.program_id(1) == 0)
    def _():
        state_ref[...] = jnp.zeros_like(state_ref)
        tail_ref[...] = jnp.zeros_like(tail_ref)

    row = lax.broadcasted_iota(jnp.int32, (c_len, c_len), 0)
    col = lax.broadcasted_iota(jnp.int32, (c_len, c_len), 1)
    causal = row >= col
    strict = row > col
    eye = jnp.where(row == col, 1.0, 0.0)
    tril_incl = jnp.where(causal, 1.0, 0.0).astype(BF16)

    def conv_silu(idx, x_ref, w_ref):
        cur = x_ref[...].astype(F32)
        xc = jnp.concatenate([tail_ref[idx], cur], axis=0)
        tail_ref[idx] = cur[n_rows - 8:, :]
        w = w_ref[...]
        acc = xc[8:, :] * w[GDN_CONV - 1:GDN_CONV, :]
        for j in range(GDN_CONV - 1):
            off = 8 - (GDN_CONV - 1) + j
            acc = acc + xc[off:off + n_rows, :] * w[j:j + 1, :]
        return _silu(acc)

    q_all = conv_silu(0, q_ref, cwq_ref)
    k_all = conv_silu(1, k_ref, cwk_ref)
    v_all = conv_silu(2, v_ref, cwv_ref)

    sm = sm_ref[...]
    beta_all = _sigmoid(sm)
    a_in = sm + dtb_ref[...]
    softplus = jnp.maximum(a_in, 0.0) + jnp.log(1.0 + jnp.exp(-jnp.abs(a_in)))
    g_all = -jnp.exp(alog_ref[...]) * softplus
    gc_rows, gc_cols = [], []
    for c in chunks:
        gh, gm, gl = _split3(g_all[c * c_len:(c + 1) * c_len, :])
        gc_c = (jnp.dot(tril_incl, gh, preferred_element_type=F32)
                + jnp.dot(tril_incl, gm, preferred_element_type=F32)
                + jnp.dot(tril_incl, gl, preferred_element_type=F32))
        gc_rows.append(gc_c)
        gc_cols.append(gc_c.T)

    heads = range(GDN_HEADS)
    units = [(c, h) for c in chunks for h in heads]
    blk = lambda x, c, h: x[c * c_len:(c + 1) * c_len, h * GDN_D:(h + 1) * GDN_D]
    q = [blk(q_all, c, h) for c, h in units]
    k = [blk(k_all, c, h) for c, h in units]
    v = [blk(v_all, c, h) for c, h in units]
    q = [x * (lax.rsqrt(jnp.sum(x * x, -1, keepdims=True) + RMS_EPS) * (GDN_D ** -0.5)) for x in q]
    k = [x * lax.rsqrt(jnp.sum(x * x, -1, keepdims=True) + RMS_EPS) for x in k]
    beta = [beta_all[c * c_len:(c + 1) * c_len, SM_BETA + h:SM_BETA + h + 1] for c, h in units]
    gc = [gc_rows[c][:, SM_DECAY + h:SM_DECAY + h + 1] for c, h in units]
    g_last = [x[c_len - 1:c_len, :] for x in gc]
    diff = [gc[i] - gc_cols[c][SM_DECAY + h:SM_DECAY + h + 1, :] for i, (c, h) in enumerate(units)]
    decay = [jnp.where(causal, jnp.exp(d), 0.0) for d in diff]
    kk = [_bdot_nt(x, x) for x in k]
    m_pow = [-(jnp.where(strict, kk[i] * decay[i], 0.0) * beta[i]) for i in range(len(units))]
    t_inv = [eye + m for m in m_pow]
    for _ in range((c_len - 1).bit_length() - 1):
        m_pow = [_bdot(m, m) for m in m_pow]
        t_inv = [t + _bdot(t, m) for t, m in zip(t_inv, m_pow)]
    e_gc = [jnp.exp(x) for x in gc]
    u = [_bdot(t_inv[i], v[i] * beta[i]) for i in range(len(units))]
    w = [_bdot(t_inv[i], k[i] * (beta[i] * e_gc[i])) for i in range(len(units))]
    qk = [_bdot_nt(q[i], k[i]) * decay[i] for i in range(len(units))]
    q_dec = [q[i] * e_gc[i] for i in range(len(units))]
    k_dec = [k[i] * jnp.exp(g_last[i] - gc[i]) for i in range(len(units))]
    state = [state_ref[h] for h in heads]
    nw = nw_ref[...]
    out_rows = []
    for c in chunks:
        ids = [c * GDN_HEADS + h for h in heads]
        v_new = [u[i] - _bdot(w[i], state[h]) for h, i in zip(heads, ids)]
        o = [_bdot(q_dec[i], state[h]) + _bdot(qk[i], v_new[h]) for h, i in zip(heads, ids)]
        state = [state[h] * jnp.exp(g_last[i]) + _bdot_tn(k_dec[i], v_new[h]) for h, i in zip(heads, ids)]
        o = [x * lax.rsqrt(jnp.mean(x * x, -1, keepdims=True) + RMS_EPS) * nw for x in o]
        out_rows.append(jnp.concatenate(o, axis=1))
    for h in heads:
        state_ref[h] = state[h]
    o_ref[...] = (jnp.concatenate(out_rows, axis=0) * _silu(z_ref[...].astype(F32))).astype(o_ref.dtype)


def _gdn(proj3, small3, conv_w, a_log, dt_bias, norm_w, c_len=128, chunks_per_step=4):
    b, s, _ = proj3.shape
    width = GDN_HEADS * GDN_D
    rows = c_len * chunks_per_step
    col = lambda off: pl.BlockSpec((None, rows, width), lambda i, j, off=off: (i, j, off // GDN_HEADS))
    cw = lambda k: pl.BlockSpec((GDN_CONV, width), lambda i, j, k=k: (0, k))
    full = lambda shape: pl.BlockSpec(shape, lambda i, j: (0, 0))
    pad = lambda v: jnp.pad(v, (SM_DECAY, LANE - SM_DECAY - GDN_HEADS)).reshape(1, LANE)
    return pl.pallas_call(
        functools.partial(_gdn_kernel, c_len=c_len),
        grid=(b, s // rows),
        in_specs=[col(CB_Q), col(CB_K), col(CB_V), col(CB_Z),
                  pl.BlockSpec((None, rows, LANE), lambda i, j: (i, j, 0)),
                  cw(0), cw(1), cw(2),
                  full((1, LANE)), full((1, LANE)), full((1, GDN_D))],
        out_specs=pl.BlockSpec((None, rows, width), lambda i, j: (i, j, 0)),
        out_shape=jax.ShapeDtypeStruct((b, s, width), BF16),
        scratch_shapes=[pltpu.VMEM((GDN_HEADS, GDN_D, GDN_D), F32), pltpu.VMEM((3, 8, width), F32)],
        compiler_params=_cp(("parallel", "arbitrary")),
        name="gdn",
    )(proj3, proj3, proj3, proj3, small3, conv_w, conv_w, conv_w,
      pad(a_log), pad(dt_bias), norm_w.reshape(1, GDN_D))


def _rope_kernel(nq_ref, kv0_ref, kv1_ref, kv2_ref, pos_ref, freq_ref, q_out, ckv_out, kv_out, tmp_ref):
    ang = pos_ref[...] * freq_ref[...]
    cos2 = jnp.cos(ang)
    sin = jnp.sin(ang)
    sin2 = jnp.where(lax.broadcasted_iota(jnp.int32, ang.shape, 1) < NSA_DH // 2, -sin, sin)

    def rope(x):
        x = x.astype(F32)
        return x * cos2 + pltpu.roll(x, NSA_DH // 2, 1) * sin2

    scale = NSA_DH ** -0.5
    q_out[...] = jnp.concatenate(
        [rope(nq_ref[:, hh * LANE:(hh + 1) * LANE]) * scale for hh in range(NSA_HEADS)],
        axis=1).astype(q_out.dtype)
    kv_refs = (kv0_ref, kv1_ref, kv2_ref)
    blk = lambda i: kv_refs[i // 4][:, (i % 4) * LANE:(i % 4 + 1) * LANE]
    n_out = tmp_ref.shape[0] // CMP_STRIDE
    for slot, val in enumerate((rope(blk(0)), rope(blk(1)), blk(2).astype(F32), blk(3).astype(F32))):
        tmp_ref[...] = val
        for l in range(CMP_STRIDE):
            ckv_out[slot, :, l * LANE:(l + 1) * LANE] = tmp_ref[pl.ds(l, n_out, stride=CMP_STRIDE), :].astype(
                ckv_out.dtype)
    kv_out[...] = jnp.concatenate(
        [rope(blk(4)), rope(blk(5)), blk(6), blk(7), rope(blk(8)), rope(blk(9)), blk(10), blk(11)],
        axis=1).astype(kv_out.dtype)


def _rope(proj3, pos3, freq2, ts=512):
    b, s, _ = proj3.shape
    kvspec = lambda k: pl.BlockSpec((None, ts, 4 * LANE), lambda i, j, k=k: (i, j, CB_KV // 4 + k))
    return pl.pallas_call(
        _rope_kernel,
        grid=(b, s // ts),
        in_specs=[pl.BlockSpec((None, ts, NSA_HEADS * LANE), lambda i, j: (i, j, CB_NQ // NSA_HEADS)),
                  kvspec(0), kvspec(1), kvspec(2),
                  pl.BlockSpec((None, ts, 1), lambda i, j: (i, j, 0)),
                  pl.BlockSpec((1, NSA_DH), lambda i, j: (0, 0))],
        out_specs=[pl.BlockSpec((None, ts, NSA_HEADS * LANE), lambda i, j: (i, j, 0)),
                   pl.BlockSpec((None, 4, ts // CMP_STRIDE, CMP_STRIDE * LANE), lambda i, j: (i, 0, j, 0)),
                   pl.BlockSpec((None, ts, 8 * LANE), lambda i, j: (i, j, 0))],
        out_shape=[jax.ShapeDtypeStruct((b, s, NSA_HEADS * LANE), BF16),
                   jax.ShapeDtypeStruct((b, 4, s // CMP_STRIDE, CMP_STRIDE * LANE), BF16),
                   jax.ShapeDtypeStruct((b, s, 8 * LANE), BF16)],
        scratch_shapes=[pltpu.VMEM((ts, LANE), F32)],
        compiler_params=_cp(("parallel", "parallel")),
        name="rope",
    )(proj3, proj3, proj3, proj3, pos3, freq2)


def _compress_kernel(x_ref, pe_ref, w1_ref, b1_ref, w2_ref, o_ref):
    x = x_ref[...]
    half = CMP_STRIDE * NSA_DH
    first = jnp.dot(x, w1_ref[:half, :], preferred_element_type=F32)
    second = jnp.dot(x, w1_ref[half:, :], preferred_element_type=F32)
    n_rows = x.shape[0]
    second = pltpu.roll(second, n_rows - 1, 0)
    pe = jnp.broadcast_to(pe_ref[...], (8, pe_ref.shape[1]))
    bias = jnp.dot(pe, w1_ref[...], preferred_element_type=F32)[0:1] + b1_ref[...]
    hid = first + second + bias
    gelu = 0.5 * hid * (1.0 + jnp.tanh(0.7978845608028654 * (hid + 0.044715 * hid * hid * hid)))
    o_ref[...] = _bdot(gelu, w2_ref[...]).astype(o_ref.dtype)


def _compress(ckv_r, pe, w1, b1, w2):
    b, _, nr, wd = ckv_r.shape
    return pl.pallas_call(
        _compress_kernel,
        grid=(b, 4),
        in_specs=[pl.BlockSpec((None, None, nr, wd), lambda i, j: (i, j, 0, 0)),
                  pl.BlockSpec((None, 1, CMP_BLOCK * NSA_DH), lambda i, j: (j // 2, 0, 0)),
                  pl.BlockSpec((None, CMP_BLOCK * NSA_DH, CMP_HIDDEN), lambda i, j: (j // 2, 0, 0)),
                  pl.BlockSpec((None, 1, CMP_HIDDEN), lambda i, j: (j // 2, 0, 0)),
                  pl.BlockSpec((None, CMP_HIDDEN, NSA_DH), lambda i, j: (j // 2, 0, 0))],
        out_specs=pl.BlockSpec((None, None, nr, NSA_DH), lambda i, j: (i, j, 0, 0)),
        out_shape=jax.ShapeDtypeStruct((b, 4, nr, NSA_DH), BF16),
        compiler_params=_cp(("parallel", "arbitrary")),
        name="compress",
    )(ckv_r, pe, w1, b1, w2)


def _nsa_kernel(q_ref, kc_ref, vc_ref, ks_ref, vs_ref, kw_ref, vw_ref, eh_ref, sm_ref, ovt_ref, o_ref, *, tq):
    g = pl.program_id(1)
    qi = pl.program_id(2)
    t0 = qi * tq
    heads = range(NSA_HPG)
    nb = eh_ref.shape[0] // SEL_BLOCK
    qs = [q_ref[:, hh * LANE:(hh + 1) * LANE] for hh in heads]
    transpose_bf16 = lambda x: x.astype(F32).T.astype(BF16)
    col_max = lambda s: jnp.max(s, 0, keepdims=True)

    cmask = (lax.broadcasted_iota(jnp.int32, (LANE, tq), 0) * CMP_STRIDE + (CMP_BLOCK - 1)
             <= t0 + lax.broadcasted_iota(jnp.int32, (LANE, tq), 1))
    kc = kc_ref[...]
    s_c = [jnp.where(cmask, _bdot_nt(kc, x), NEG) for x in qs]
    e_c = [jnp.where(cmask, jnp.exp(s - col_max(s)), 0.0) for s in s_c]
    p_c = [e * (1.0 / jnp.maximum(jnp.sum(e, 0, keepdims=True), 1e-30)) for e in e_c]
    vc_t = transpose_bf16(vc_ref[...])
    o_c = [_bdot(vc_t, p) for p in p_c]

    p_sum = p_c[0]
    for p in p_c[1:]:
        p_sum = p_sum + p
    ovt = ovt_ref[...]
    imp_t = sum(jnp.dot(ovt, part, preferred_element_type=F32) for part in _split3(p_sum))[:nb]
    blk = lax.broadcasted_iota(jnp.int32, (nb, tq), 0)
    t = t0 + lax.broadcasted_iota(jnp.int32, (nb, tq), 1)
    cur = t >> SEL_SHIFT
    forced = (blk == 0) | (blk == cur) | (blk == cur - 1)
    val = jnp.where(forced, jnp.inf, jnp.where(blk * SEL_BLOCK <= t, imp_t, -jnp.inf))
    rank = jnp.zeros((nb, tq), F32)
    for i in range(nb):
        vi = val[i:i + 1, :]
        rank = rank + jnp.where((vi > val) | ((vi == val) & (blk > i)), 1.0, 0.0)
    drop_t = jnp.where(rank < SEL_TOPK, 0.0, MASK_BIG)
    drop = jnp.concatenate([drop_t, jnp.zeros((LANE - nb, tq), F32)], axis=0).T
    drop_b = (-drop).astype(BF16)
    q_aug = [jnp.concatenate([x, drop_b], axis=1) for x in qs]

    ones_col = jnp.where(lax.broadcasted_iota(jnp.int32, (tq, LANE), 1) == 0, 1.0, 0.0).astype(BF16)
    key_i = lax.broadcasted_iota(jnp.int32, (tq, tq), 0)
    qry_i = lax.broadcasted_iota(jnp.int32, (tq, tq), 1)
    key_le_query = key_i <= qry_i
    normalise = lambda acc: acc[:NSA_DH, :] * (1.0 / jnp.maximum(acc[NSA_DH:NSA_DH + 1, :], 1e-30))

    prev0 = pl.multiple_of(jnp.maximum(qi - 1, 0) * tq, tq)
    own0 = pl.multiple_of(t0, tq)
    in_prev = key_i > qry_i + jnp.where(qi > 0, 0, tq)
    kw_prev, kw_own = kw_ref[pl.ds(prev0, tq), :], kw_ref[pl.ds(own0, tq), :]
    v_w_t = transpose_bf16(jnp.concatenate(
        [jnp.concatenate([vw_ref[pl.ds(prev0, tq), :], ones_col], axis=1),
         jnp.concatenate([vw_ref[pl.ds(own0, tq), :], ones_col], axis=1)], axis=0))
    s_w = [jnp.concatenate([jnp.where(in_prev, _bdot_nt(kw_prev, x), NEG),
                            jnp.where(key_le_query, _bdot_nt(kw_own, x), NEG)], axis=0) for x in qs]
    e_w = [jnp.exp(s - col_max(s)) for s in s_w]
    o_w = [normalise(_bdot(v_w_t, e)) for e in e_w]

    def sel_chunk(c, carry, diagonal):
        m_i, acc = carry
        k0 = pl.multiple_of(c * tq, tq)
        k_aug = jnp.concatenate([ks_ref[pl.ds(k0, tq), :], eh_ref[pl.ds(k0, tq), :]], axis=1)
        v_t = transpose_bf16(jnp.concatenate([vs_ref[pl.ds(k0, tq), :], ones_col], axis=1))
        s = [lax.dot_general(k_aug, x, (((1,), (1,)), ((), ())), preferred_element_type=F32) for x in q_aug]
        if diagonal:
            s = [jnp.where(key_le_query, x, -MASK_BIG) for x in s]
        m_new = [jnp.maximum(m, col_max(x)) for m, x in zip(m_i, s)]
        p = [jnp.exp(x - m).astype(BF16) for x, m in zip(s, m_new)]
        acc = [jnp.exp(m - mn) * a + jnp.dot(v_t, x, preferred_element_type=F32)
               for m, mn, a, x in zip(m_i, m_new, acc, p)]
        return tuple(m_new), tuple(acc)

    init = (tuple(jnp.full((1, tq), -MASK_BIG, F32) for _ in heads),
            tuple(jnp.zeros((2 * NSA_DH, tq), F32) for _ in heads))
    carry = lax.fori_loop(0, qi, functools.partial(sel_chunk, diagonal=False), init)
    _, acc_s = sel_chunk(qi, carry, True)
    o_s = [normalise(a) for a in acc_s]

    gates_t = _sigmoid(sm_ref[...]).T
    outs = []
    for hh in heads:
        def gate(branch):
            rows = [gates_t[SM_NGATE + (grp * NSA_HPG + hh) * 3 + branch][None, :] for grp in range(NSA_KV_HEADS)]
            return jnp.where(g == 0, rows[0], rows[1])
        assert NSA_KV_HEADS == 2
        outs.append((gate(0) * o_c[hh] + gate(1) * o_s[hh] + gate(2) * o_w[hh]).T)
    o_ref[...] = jnp.concatenate(outs, axis=1).astype(o_ref.dtype)


def _nsa(q_r, kvc, kv_r, small3):
    b, s, _ = q_r.shape
    nr = kvc.shape[2]
    nb = s // SEL_BLOCK
    tq = WINDOW
    assert s % tq == 0 and nb <= LANE and nr <= LANE
    gw = NSA_HPG * LANE
    eh = jnp.asarray(np.arange(s)[:, None] // SEL_BLOCK == np.arange(LANE)[None, :], BF16)
    ovt = jnp.asarray(_overlap_matrix(s).T, BF16)
    kvspec = lambda off: pl.BlockSpec((None, s, LANE), lambda i, g, j, off=off: (i, 0, off + g))
    cspec = lambda off: pl.BlockSpec((None, None, nr, NSA_DH), lambda i, g, j, off=off: (i, off + g, 0, 0))
    return pl.pallas_call(
        functools.partial(_nsa_kernel, tq=tq),
        grid=(b, NSA_KV_HEADS, s // tq),
        in_specs=[pl.BlockSpec((None, tq, gw), lambda i, g, j: (i, j, g)),
                  cspec(0), cspec(2),
                  kvspec(0), kvspec(2), kvspec(4), kvspec(6),
                  pl.BlockSpec((s, LANE), lambda i, g, j: (0, 0)),
                  pl.BlockSpec((None, tq, LANE), lambda i, g, j: (i, j, 0)),
                  pl.BlockSpec((LANE, LANE), lambda i, g, j: (0, 0))],
        out_specs=pl.BlockSpec((None, tq, gw), lambda i, g, j: (i, j, g)),
        out_shape=jax.ShapeDtypeStruct((b, s, NSA_HEADS * NSA_DH), BF16),
        compiler_params=_cp(("parallel", "parallel", "arbitrary")),
        name="nsa",
    )(q_r, kvc, kvc, kv_r, kv_r, kv_r, kv_r, eh, small3, ovt)


def _merge_xattn_kernel(ya_ref, yb_ref, mg_ref, x_ref, wout_ref, g1_ref, b1_ref,
                        kv_ref, wq_ref, wo_ref, g_ref, b_ref, o_ref):
    mg = mg_ref[...].astype(F32)
    merged = (_sigmoid(mg[:, :D_MODEL]) * ya_ref[...].astype(F32)
              + _sigmoid(mg[:, D_MODEL:]) * yb_ref[...].astype(F32))
    h = _layer_norm(DN_ALPHA * x_ref[...] + _bdot(merged, wout_ref[...]), g1_ref[...], b1_ref[...])
    q = _bdot(h, wq_ref[...]).astype(BF16)
    kv = kv_ref[...]
    outs = []
    for hh in range(XA_HEADS):
        sl = slice(hh * XA_DH, (hh + 1) * XA_DH)
        s = _bdot_nt(q[:, sl], kv[:, sl]) * (XA_DH ** -0.5)
        m = jnp.max(s, -1, keepdims=True)
        e = jnp.exp(s - m)
        pv = _bdot(e, kv[:, D_MODEL + hh * XA_DH:D_MODEL + (hh + 1) * XA_DH])
        outs.append(pv * (1.0 / jnp.sum(e, -1, keepdims=True)))
    o = jnp.concatenate(outs, axis=1)
    o_ref[...] = _layer_norm(DN_ALPHA * h + _bdot(o, wo_ref[...]), g_ref[...], b_ref[...])


def _merge_xattn(y_a, y_b, proj3, x3, w_out, g1, b1, kv3, wq, wo, g2, b2, tm=512):
    bsz, s, _ = x3.shape
    m = kv3.shape[1]
    full = lambda shape: pl.BlockSpec(shape, lambda i, j: (0, 0))
    row = pl.BlockSpec((None, tm, D_MODEL), lambda i, j: (i, j, 0))
    return pl.pallas_call(
        _merge_xattn_kernel,
        grid=(bsz, s // tm),
        in_specs=[row, row,
                  pl.BlockSpec((None, tm, 2 * D_MODEL), lambda i, j: (i, j, CB_MG * LANE // (2 * D_MODEL))),
                  row, full((D_MODEL, D_MODEL)), full((1, D_MODEL)), full((1, D_MODEL)),
                  pl.BlockSpec((None, m, 2 * D_MODEL), lambda i, j: (i, 0, 0)),
                  full((D_MODEL, D_MODEL)), full((D_MODEL, D_MODEL)), full((1, D_MODEL)), full((1, D_MODEL))],
        out_specs=row,
        out_shape=jax.ShapeDtypeStruct((bsz, s, D_MODEL), F32),
        compiler_params=_cp(("parallel", "parallel")),
        name="merge_xattn",
    )(y_a, y_b, proj3, x3, w_out, g1, b1, kv3, wq, wo, g2, b2)


def _route(x, wrt_h, wrt_l, br):
    xh = x.astype(BF16)
    xl = (x - xh.astype(F32)).astype(BF16)
    logits = _bdot_nt(wrt_h, xh) + _bdot_nt(wrt_l, xh) + _bdot_nt(wrt_h, xl) + br
    row = lax.broadcasted_iota(jnp.int32, logits.shape, 0)
    first_row = lambda hit: jnp.min(jnp.where(hit, row, LANE), 0, keepdims=True)
    gmask = row < N_GROUPS
    gl = jnp.where(gmask, logits, -jnp.inf)
    gmax = jnp.max(gl, 0, keepdims=True)
    g_idx = first_row(gl == gmax)
    p_top = 1.0 / jnp.sum(jnp.where(gmask, jnp.exp(gl - gmax), 0.0), 0, keepdims=True)
    e_row = row - N_GROUPS
    emask = (e_row >= 0) & (e_row < N_EXPERTS) & ((e_row >> GROUP_SHIFT) == g_idx)
    el = jnp.where(emask, logits, -jnp.inf)
    ee = jnp.where(emask, jnp.exp(el - jnp.max(el, 0, keepdims=True)), 0.0)
    pe = ee / jnp.sum(ee, 0, keepdims=True)
    pv = jnp.where(emask, pe, -jnp.inf)
    m1 = jnp.max(pv, 0, keepdims=True)
    i1 = first_row(pv == m1)
    pv2 = jnp.where(row == i1, -jnp.inf, pv)
    m2 = jnp.max(pv2, 0, keepdims=True)
    i2 = first_row(pv2 == m2)
    denom = m1 + m2
    return i1, i2, p_top * m1 / denom, p_top * m2 / denom


META_E1, META_E2, META_R1, META_R2, META_W1, META_W2 = 0, 1, 2, 3, 4, 5
META_ROWS = 8


def _router_kernel(h_ref, wrh_ref, wrl_ref, br_ref, meta_ref, cnt_ref, carry_ref):
    @pl.when(pl.program_id(0) == 0)
    def _():
        carry_ref[...] = jnp.zeros_like(carry_ref)

    tm = h_ref.shape[0]
    i1, i2, w1, w2 = _route(h_ref[...], wrh_ref[...], wrl_ref[...], br_ref[...])
    row = lax.broadcasted_iota(jnp.int32, (LANE, tm), 0)
    onehot = jnp.where((row == i1) | (row == i2), 1.0, 0.0)
    earlier = jnp.where(lax.broadcasted_iota(jnp.int32, (tm, tm), 0) < lax.broadcasted_iota(jnp.int32, (tm, tm), 1),
                        1.0, 0.0).astype(BF16)
    before = jnp.dot(onehot.astype(BF16), earlier, preferred_element_type=F32) + carry_ref[...]
    r1 = jnp.sum(jnp.where(row == i1, before, 0.0), 0, keepdims=True)
    r2 = jnp.sum(jnp.where(row == i2, before, 0.0), 0, keepdims=True)
    carry_ref[...] += jnp.sum(onehot, axis=1, keepdims=True)
    cnt_ref[...] = carry_ref[...]
    rows = {META_E1: (i1 - N_GROUPS).astype(F32), META_E2: (i2 - N_GROUPS).astype(F32),
            META_R1: r1, META_R2: r2, META_W1: w1, META_W2: w2}
    zero = jnp.zeros((1, tm), F32)
    meta_ref[...] = jnp.concatenate([rows.get(r, zero) for r in range(META_ROWS)], axis=0)


def _router(h2, wrt_h, wrt_l, br, tm=1024):
    n = h2.shape[0]
    full = lambda shape: pl.BlockSpec(shape, lambda i: (0, 0))
    return pl.pallas_call(
        _router_kernel,
        grid=(n // tm,),
        in_specs=[pl.BlockSpec((tm, D_MODEL), lambda i: (i, 0)),
                  full((LANE, D_MODEL)), full((LANE, D_MODEL)), full((LANE, 1))],
        out_specs=[pl.BlockSpec((META_ROWS, tm), lambda i: (0, i)), full((LANE, 1))],
        out_shape=[jax.ShapeDtypeStruct((META_ROWS, n), F32), jax.ShapeDtypeStruct((LANE, 1), F32)],
        scratch_shapes=[pltpu.VMEM((LANE, 1), F32)],
        compiler_params=_cp(("arbitrary",)),
        name="moe_router",
    )(h2, wrt_h, wrt_l, br)


def _positions_kernel(meta_ref, cnt_ref, pos_ref):
    tm = meta_ref.shape[1]
    below = jnp.where(lax.broadcasted_iota(jnp.int32, (LANE, LANE), 1) < lax.broadcasted_iota(jnp.int32, (LANE, LANE), 0),
                      1.0, 0.0).astype(BF16)
    cnt = jnp.broadcast_to(cnt_ref[...], (LANE, LANE))
    start = sum(jnp.dot(below, p, preferred_element_type=F32) for p in _split3(cnt))
    start = jnp.concatenate([start] * (tm // LANE), axis=1)
    meta = meta_ref[...]
    row = lax.broadcasted_iota(jnp.int32, (LANE, tm), 0)
    pos = []
    for r_e, r_r in ((META_E1, META_R1), (META_E2, META_R2)):
        e_row = meta[r_e:r_e + 1, :].astype(jnp.int32) + N_GROUPS
        seg = jnp.sum(jnp.where(row == e_row, start, 0.0), 0, keepdims=True)
        pos.append((seg + meta[r_r:r_r + 1, :]).astype(jnp.int32))
    pos_ref[...] = jnp.concatenate(pos + [jnp.zeros((META_ROWS - 2, tm), jnp.int32)], axis=0)


def _positions(meta, counts, tm=2048):
    n = meta.shape[1]
    return pl.pallas_call(
        _positions_kernel,
        grid=(n // tm,),
        in_specs=[pl.BlockSpec((META_ROWS, tm), lambda i: (0, i)), pl.BlockSpec((LANE, 1), lambda i: (0, 0))],
        out_specs=pl.BlockSpec((META_ROWS, tm), lambda i: (0, i)),
        out_shape=jax.ShapeDtypeStruct((META_ROWS, n), jnp.int32),
        compiler_params=_cp(("parallel",)),
        name="moe_positions",
    )(meta, counts)


def _row_copies(n_rows, pos_refs, make_copy):
    def body(r, carry):
        for p in pos_refs:
            make_copy(r, p[r]).start()
        return carry

    lax.fori_loop(0, n_rows, body, 0, unroll=8)


def _dispatch_kernel(pos1_ref, pos2_ref, h_ref, xs_ref, sem):
    tm = h_ref.shape[0]
    _row_copies(tm, (pos1_ref, pos2_ref),
                lambda r, p: pltpu.make_async_copy(h_ref.at[pl.ds(r, 1)], xs_ref.at[pl.ds(p, 1)], sem))
    for _ in range(2):
        pltpu.make_async_copy(h_ref, xs_ref.at[pl.ds(0, tm)], sem).wait()


def _dispatch(h2, pos1, pos2, n_rows, tm=512):
    n = h2.shape[0]
    smem = pl.BlockSpec((tm,), lambda i: (i,), memory_space=pltpu.SMEM)
    return pl.pallas_call(
        _dispatch_kernel,
        grid=(n // tm,),
        in_specs=[smem, smem, pl.BlockSpec((tm, D_MODEL), lambda i: (i, 0))],
        out_specs=pl.BlockSpec(memory_space=pl.ANY),
        out_shape=jax.ShapeDtypeStruct((n_rows, D_MODEL), F32),
        scratch_shapes=[pltpu.SemaphoreType.DMA],
        compiler_params=_cp(("arbitrary",)),
        name="moe_dispatch",
    )(pos1, pos2, h2)


def _ffn_kernel(tile_ref, exp_ref, lo_ref, hi_ref, first_ref, xs_ref, wg_ref, wu_ref, wd_ref, ys_ref):
    i = pl.program_id(0)
    tm = xs_ref.shape[0]
    rows = lax.broadcasted_iota(jnp.int32, (tm, 1), 0)
    mine = (rows >= lo_ref[i]) & (rows < hi_ref[i])
    x = xs_ref[...].astype(BF16)
    gate = jnp.dot(x, wg_ref[...].astype(BF16), preferred_element_type=F32)
    up = jnp.dot(x, wu_ref[...].astype(BF16), preferred_element_type=F32)
    y = _bdot(_silu(gate) * up, wd_ref[...])

    @pl.when(first_ref[i] == 1)
    def _():
        ys_ref[...] = jnp.where(mine, y, 0.0)

    @pl.when(first_ref[i] == 0)
    def _():
        ys_ref[...] = jnp.where(mine, y, ys_ref[...])


def _ffn(xs, items, wg, wu, wd, tm):
    n_rows = xs.shape[0]
    wspec = lambda shape: pl.BlockSpec((None,) + shape, lambda i, tl, ex, lo, hi, fi: (ex[i], 0, 0))
    row_tile = pl.BlockSpec((tm, D_MODEL), lambda i, tl, ex, lo, hi, fi: (tl[i], 0))
    return pl.pallas_call(
        _ffn_kernel,
        grid_spec=pltpu.PrefetchScalarGridSpec(
            num_scalar_prefetch=5,
            grid=(items[0].shape[0],),
            in_specs=[row_tile, wspec((D_MODEL, D_FF)), wspec((D_MODEL, D_FF)), wspec((D_FF, D_MODEL))],
            out_specs=row_tile),
        out_shape=jax.ShapeDtypeStruct((n_rows, D_MODEL), F32),
        compiler_params=_cp(("arbitrary",)),
        name="moe_ffn",
    )(*items, xs, wg, wu, wd)


def _combine_kernel(pos1_ref, pos2_ref, nxt1_ref, nxt2_ref, h_ref, meta_ref, ys_ref, g_ref, b_ref, o_ref,
                    buf_ref, sem):
    i = pl.program_id(0)
    tm = h_ref.shape[0]
    slot = i % 2

    def start_gather(p1_ref, p2_ref, s):
        def copy(k, pos_ref):
            return lambda r, p: pltpu.make_async_copy(ys_ref.at[pl.ds(p, 1)], buf_ref.at[s, k, pl.ds(r, 1)],
                                                      sem.at[s])
        def body(r, carry):
            copy(0, p1_ref)(r, p1_ref[r]).start()
            copy(1, p2_ref)(r, p2_ref[r]).start()
            return carry
        lax.fori_loop(0, tm, body, 0, unroll=8)

    @pl.when(i == 0)
    def _():
        start_gather(pos1_ref, pos2_ref, 0)

    @pl.when(i + 1 < pl.num_programs(0))
    def _():
        start_gather(nxt1_ref, nxt2_ref, 1 - slot)

    for k in range(2):
        pltpu.make_async_copy(ys_ref.at[pl.ds(0, tm)], buf_ref.at[slot, k], sem.at[slot]).wait()
    meta = jnp.concatenate([meta_ref[...], jnp.zeros((LANE - META_ROWS, tm), F32)], axis=0).T
    ffn = (meta[:, META_W1:META_W1 + 1] * buf_ref[slot, 0] + meta[:, META_W2:META_W2 + 1] * buf_ref[slot, 1])
    o_ref[...] = _layer_norm(DN_ALPHA * h_ref[...] + ffn, g_ref[...], b_ref[...])


def _combine(h2, meta, ys, pos1, pos2, g, b, tm=256):
    n = h2.shape[0]
    last = n // tm - 1
    smem = pl.BlockSpec((tm,), lambda i: (i,), memory_space=pltpu.SMEM)
    smem_next = pl.BlockSpec((tm,), lambda i: (jnp.minimum(i + 1, last),), memory_space=pltpu.SMEM)
    full = lambda shape: pl.BlockSpec(shape, lambda i: (0, 0))
    return pl.pallas_call(
        _combine_kernel,
        grid=(n // tm,),
        in_specs=[smem, smem, smem_next, smem_next, pl.BlockSpec((tm, D_MODEL), lambda i: (i, 0)),
                  pl.BlockSpec((META_ROWS, tm), lambda i: (0, i)),
                  pl.BlockSpec(memory_space=pl.ANY), full((1, D_MODEL)), full((1, D_MODEL))],
        out_specs=pl.BlockSpec((tm, D_MODEL), lambda i: (i, 0)),
        out_shape=jax.ShapeDtypeStruct((n, D_MODEL), F32),
        scratch_shapes=[pltpu.VMEM((2, 2, tm, D_MODEL), F32), pltpu.SemaphoreType.DMA((2,))],
        compiler_params=_cp(("arbitrary",)),
        name="moe_combine_ln3",
    )(pos1, pos2, pos1, pos2, h2, meta, ys, g, b)


def _moe(h2, wr_h, wr_l, br, wg, wu, wd, g, b, tile=512):
    n = h2.shape[0]
    i32 = jnp.int32
    meta, counts = _router(h2, wr_h, wr_l, br)
    cnt = counts[N_GROUPS:N_GROUPS + N_EXPERTS, 0].astype(i32)
    seg_end = jnp.cumsum(cnt)
    seg_start = seg_end - cnt
    pos = _positions(meta, counts)
    pos1, pos2 = pos[0], pos[1]
    first_tile = seg_start // tile
    n_items_e = jnp.where(cnt > 0, (seg_end - 1) // tile - first_tile + 1, 0)
    item_end = jnp.cumsum(n_items_e)
    n_items = (2 * n) // tile + N_EXPERTS - 1
    idx = jnp.minimum(jnp.arange(n_items, dtype=i32), item_end[-1] - 1)
    exp = jnp.sum(idx[:, None] >= item_end[None, :], axis=1).astype(i32)
    til = first_tile[exp] + idx - (item_end - n_items_e)[exp]
    lo = jnp.maximum(seg_start[exp], til * tile) - til * tile
    hi = jnp.minimum(seg_end[exp], (til + 1) * tile) - til * tile
    repeat = jnp.arange(n_items, dtype=i32) >= item_end[-1]
    hi = jnp.where(repeat, lo, hi)
    first = jnp.concatenate([jnp.ones((1,), i32), (til[1:] != til[:-1]).astype(i32)])
    xs = _dispatch(h2, pos1, pos2, 2 * n)
    ys = _ffn(xs, (til.astype(i32), exp, lo.astype(i32), hi.astype(i32), first), wg, wu, wd, tile)
    return _combine(h2, meta, ys, pos1, pos2, g, b)


def _regroup_w_in(w):
    sizes = (1024, 1024, 1024, 1024, 8, 8, 1024, 256, 256, 256, 256, 256, 256, 24, 2048)
    offs = [0]
    for sz in sizes:
        offs.append(offs[-1] + sz)
    seg = lambda i: w[:, offs[i]:offs[i + 1]]
    small = jnp.concatenate([seg(4), seg(5), seg(13)], axis=1)
    small = jnp.pad(small, ((0, 0), (0, LANE - small.shape[1])))
    big = jnp.concatenate([seg(0), seg(1), seg(2), seg(3), seg(14), seg(6), seg(7), seg(8), seg(9), seg(10),
                           seg(11), seg(12)], axis=1)
    return big.astype(BF16), small.astype(BF16)


def _overlap_matrix(s):
    nb = s // SEL_BLOCK
    c0 = np.arange(LANE) * CMP_STRIDE
    s0 = np.arange(LANE) * SEL_BLOCK
    ov = np.minimum(c0[:, None] + CMP_BLOCK, s0[None, :] + SEL_BLOCK) - np.maximum(c0[:, None], s0[None, :])
    ov = np.maximum(ov, 0).astype(np.float32) / CMP_BLOCK
    nc = (s - CMP_BLOCK) // CMP_STRIDE + 1
    keep = (np.arange(LANE)[:, None] < nc) & (np.arange(LANE)[None, :] < nb)
    return np.where(keep, ov, 0.0).astype(np.float32)


def _layer(h, mem, pos3, freq2, w_in, conv_w, a_log, dt_bias, norm_w, cmp_pe, cmp_w1, cmp_b1, cmp_w2, w_out,
           ln1_g, ln1_b, xa_wq, xa_wkv, xa_wo, ln2_g, ln2_b, w_group, b_group, w_expert, b_expert,
           w_gate, w_up, w_down, ln3_g, ln3_b):
    b, s, d = h.shape
    n = b * s
    x2 = h.reshape(n, d)
    proj, small = _inproj(x2, *_regroup_w_in(w_in))
    proj3 = proj.reshape(b, s, -1)
    small3 = small.reshape(b, s, LANE)

    y_a = _gdn(proj3, small3, conv_w, a_log, dt_bias, norm_w)

    q_r, ckv_r, kv_r = _rope(proj3, pos3, freq2)
    kvc = _compress(ckv_r, cmp_pe.reshape(2, 1, CMP_BLOCK * NSA_DH).astype(BF16), cmp_w1.astype(BF16),
                    cmp_b1.reshape(2, 1, CMP_HIDDEN), cmp_w2.astype(BF16))
    y_b = _nsa(q_r, kvc, kv_r, small3)

    row = lambda v: v.reshape(1, -1)
    m = mem.shape[1]
    kv = _matmul(mem.reshape(b * m, d).astype(BF16), xa_wkv.astype(BF16), BF16, tm=512, tn=512)
    h2 = _merge_xattn(y_a, y_b, proj3, h, w_out.astype(BF16), row(ln1_g), row(ln1_b),
                      kv.reshape(b, m, 2 * d), xa_wq.astype(BF16), xa_wo.astype(BF16), row(ln2_g), row(ln2_b))

    wrt = jnp.pad(jnp.concatenate([w_group, w_expert], axis=1).T, ((0, LANE - N_GROUPS - N_EXPERTS), (0, 0)))
    wrt_h = wrt.astype(BF16)
    wrt_l = (wrt - wrt_h.astype(F32)).astype(BF16)
    br = jnp.pad(jnp.concatenate([b_group, b_expert]), (0, LANE - N_GROUPS - N_EXPERTS)).reshape(LANE, 1)
    h3 = _moe(h2.reshape(n, d), wrt_h, wrt_l, br, w_gate, w_up, w_down, row(ln3_g), row(ln3_b))
    return h3.reshape(b, s, d)


def kernel(x, mem, positions, w_in, gdn_conv_w, gdn_a_log, gdn_dt_bias, gdn_norm_w, cmp_pe, cmp_w1, cmp_b1, cmp_w2, w_out, ln1_g, ln1_b, xa_wq, xa_wkv, xa_wo, ln2_g, ln2_b, moe_w_group, moe_b_group, moe_w_expert, moe_b_expert, moe_w_gate, moe_w_up, moe_w_down, ln3_g, ln3_b):
    half = NSA_DH // 2
    inv_freq = ROPE_THETA ** (-jnp.arange(half, dtype=F32) / half)
    freq2 = jnp.concatenate([inv_freq, inv_freq]).reshape(1, NSA_DH)
    pos3 = positions.astype(F32)[..., None]
    h = x
    for l in range(DEPTH):
        h = _layer(h, mem, pos3, freq2, w_in[l], gdn_conv_w[l], gdn_a_log[l], gdn_dt_bias[l], gdn_norm_w[l],
                   cmp_pe[l], cmp_w1[l], cmp_b1[l], cmp_w2[l], w_out[l], ln1_g[l], ln1_b[l],
                   xa_wq[l], xa_wkv[l], xa_wo[l], ln2_g[l], ln2_b[l], moe_w_group[l], moe_b_group[l],
                   moe_w_expert[l], moe_b_expert[l], moe_w_gate[l], moe_w_up[l], moe_w_down[l],
                   ln3_g[l], ln3_b[l])
    return h
```

```python
import functools

import jax
import jax.numpy as jnp
import numpy as np
from jax import lax
from jax.experimental import pallas as pl
from jax.experimental.pallas import tpu as pltpu

F32 = jnp.float32
BF16 = jnp.bfloat16

D_MODEL = 1024
LANE = 128
GDN_HEADS = 8
GDN_D = 128
GDN_CONV = 4
NSA_HEADS = 8
NSA_KV_HEADS = 2
NSA_HPG = NSA_HEADS // NSA_KV_HEADS
NSA_DH = 128
CMP_BLOCK = 32
CMP_STRIDE = 16
CMP_HIDDEN = 256
SEL_BLOCK = 64
SEL_SHIFT = 6
SEL_TOPK = 8
WINDOW = 256
XA_HEADS = 4
XA_DH = 256
N_GROUPS = 4
EXPERTS_PER_GROUP = 8
GROUP_SHIFT = 3
N_EXPERTS = N_GROUPS * EXPERTS_PER_GROUP
D_FF = 256
DEPTH = 1
DN_ALPHA = (2.0 * DEPTH) ** 0.25
LN_EPS = 1e-5
RMS_EPS = 1e-6
ROPE_THETA = 10000.0
NEG = -1e30
MASK_BIG = 1e30

CB_Q, CB_K, CB_V, CB_Z = 0, 8, 16, 24
CB_MG = 32
CB_NQ = 48
CB_KV = 56
SM_BETA, SM_DECAY, SM_NGATE = 0, 8, 16

VMEM_LIMIT = 48 * 1024 * 1024


def _cp(sem, vmem=VMEM_LIMIT):
    return pltpu.CompilerParams(dimension_semantics=sem, vmem_limit_bytes=vmem)


def _bdot(a, b):
    return jnp.dot(a.astype(BF16), b.astype(BF16), preferred_element_type=F32)


def _bdot_nt(a, b):
    return lax.dot_general(a.astype(BF16), b.astype(BF16), (((1,), (1,)), ((), ())),
                           preferred_element_type=F32)


def _bdot_tn(a, b):
    return lax.dot_general(a.astype(BF16), b.astype(BF16), (((0,), (0,)), ((), ())),
                           preferred_element_type=F32)


def _split3(x):
    h = x.astype(BF16)
    r = x - h.astype(F32)
    m = r.astype(BF16)
    l = (r - m.astype(F32)).astype(BF16)
    return h, m, l


def _sigmoid(x):
    return 1.0 / (1.0 + jnp.exp(-x))


def _silu(x):
    return x * _sigmoid(x)


def _layer_norm(x, g, b):
    mu = jnp.mean(x, -1, keepdims=True)
    xc = x - mu
    var = jnp.mean(xc * xc, -1, keepdims=True)
    return xc * lax.rsqrt(var + LN_EPS) * g + b


def _mm_kernel(x_ref, w_ref, o_ref):
    o_ref[...] = jnp.dot(x_ref[...], w_ref[...], preferred_element_type=F32).astype(o_ref.dtype)


def _matmul(x, w, out_dtype, tm, tn):
    m, k = x.shape
    n = w.shape[1]
    return pl.pallas_call(
        _mm_kernel,
        grid=(m // tm, n // tn),
        in_specs=[pl.BlockSpec((tm, k), lambda i, j: (i, 0)),
                  pl.BlockSpec((k, tn), lambda i, j: (0, j))],
        out_specs=pl.BlockSpec((tm, tn), lambda i, j: (i, j)),
        out_shape=jax.ShapeDtypeStruct((m, n), out_dtype),
        compiler_params=_cp(("parallel", "parallel")),
        name="matmul",
    )(x, w)


def _inproj_kernel(x_ref, w_ref, ws_ref, o_ref, os_ref, xb_ref):
    @pl.when(pl.program_id(1) == 0)
    def _():
        xb_ref[...] = x_ref[...].astype(BF16)
        os_ref[...] = jnp.dot(xb_ref[...], ws_ref[...], preferred_element_type=F32)

    o_ref[...] = jnp.dot(xb_ref[...], w_ref[...], preferred_element_type=F32).astype(o_ref.dtype)


def _inproj(x2, w_big, w_small, tm=2048, tn=512):
    m, k = x2.shape
    n = w_big.shape[1]
    return pl.pallas_call(
        _inproj_kernel,
        grid=(m // tm, n // tn),
        in_specs=[pl.BlockSpec((tm, k), lambda i, j: (i, 0)),
                  pl.BlockSpec((k, tn), lambda i, j: (0, j)),
                  pl.BlockSpec((k, LANE), lambda i, j: (0, 0))],
        out_specs=[pl.BlockSpec((tm, tn), lambda i, j: (i, j)),
                   pl.BlockSpec((tm, LANE), lambda i, j: (i, 0))],
        out_shape=[jax.ShapeDtypeStruct((m, n), BF16), jax.ShapeDtypeStruct((m, LANE), F32)],
        scratch_shapes=[pltpu.VMEM((tm, k), BF16)],
        compiler_params=_cp(("parallel", "arbitrary")),
        name="inproj",
    )(x2, w_big, w_small)


def _gdn_kernel(q_ref, k_ref, v_ref, z_ref, sm_ref, cwq_ref, cwk_ref, cwv_ref,
                alog_ref, dtb_ref, nw_ref, o_ref, state_ref, tail_ref, *, c_len):
    n_rows = q_ref.shape[0]
    chunks = range(n_rows // c_len)

    @pl.when(pl.program_id(1) == 0)
    def _():
        state_ref[...] = jnp.zeros_like(state_ref)
        tail_ref[...] = jnp.zeros_like(tail_ref)

    row = lax.broadcasted_iota(jnp.int32, (c_len, c_len), 0)
    col = lax.broadcasted_iota(jnp.int32, (c_len, c_len), 1)
    causal = row >= col
    strict = row > col
    eye = jnp.where(row == col, 1.0, 0.0)
    tril_incl = jnp.where(causal, 1.0, 0.0).astype(BF16)

    def conv_silu(idx, x_ref, w_ref):
        cur = x_ref[...].astype(F32)
        xc = jnp.concatenate([tail_ref[idx], cur], axis=0)
        tail_ref[idx] = cur[n_rows - 8:, :]
        w = w_ref[...]
        acc = xc[8:, :] * w[GDN_CONV - 1:GDN_CONV, :]
        for j in range(GDN_CONV - 1):
            off = 8 - (GDN_CONV - 1) + j
            acc = acc + xc[off:off + n_rows, :] * w[j:j + 1, :]
        return _silu(acc)

    q_all = conv_silu(0, q_ref, cwq_ref)
    k_all = conv_silu(1, k_ref, cwk_ref)
    v_all = conv_silu(2, v_ref, cwv_ref)

    sm = sm_ref[...]
    beta_all = _sigmoid(sm)
    a_in = sm + dtb_ref[...]
    softplus = jnp.maximum(a_in, 0.0) + jnp.log(1.0 + jnp.exp(-jnp.abs(a_in)))
    g_all = -jnp.exp(alog_ref[...]) * softplus
    gc_rows, gc_cols = [], []
    for c in chunks:
        gh, gm, gl = _split3(g_all[c * c_len:(c + 1) * c_len, :])
        gc_c = (jnp.dot(tril_incl, gh, preferred_element_type=F32)
                + jnp.dot(tril_incl, gm, preferred_element_type=F32)
                + jnp.dot(tril_incl, gl, preferred_element_type=F32))
        gc_rows.append(gc_c)
        gc_cols.append(gc_c.T)

    heads = range(GDN_HEADS)
    units = [(c, h) for c in chunks for h in heads]
    blk = lambda x, c, h: x[c * c_len:(c + 1) * c_len, h * GDN_D:(h + 1) * GDN_D]
    q = [blk(q_all, c, h) for c, h in units]
    k = [blk(k_all, c, h) for c, h in units]
    v = [blk(v_all, c, h) for c, h in units]
    q = [x * (lax.rsqrt(jnp.sum(x * x, -1, keepdims=True) + RMS_EPS) * (GDN_D ** -0.5)) for x in q]
    k = [x * lax.rsqrt(jnp.sum(x * x, -1, keepdims=True) + RMS_EPS) for x in k]
    beta = [beta_all[c * c_len:(c + 1) * c_len, SM_BETA + h:SM_BETA + h + 1] for c, h in units]
    gc = [gc_rows[c][:, SM_DECAY + h:SM_DECAY + h + 1] for c, h in units]
    g_last = [x[c_len - 1:c_len, :] for x in gc]
    diff = [gc[i] - gc_cols[c][SM_DECAY + h:SM_DECAY + h + 1, :] for i, (c, h) in enumerate(units)]
    decay = [jnp.where(causal, jnp.exp(d), 0.0) for d in diff]
    kk = [_bdot_nt(x, x) for x in k]
    m_pow = [-(jnp.where(strict, kk[i] * decay[i], 0.0) * beta[i]) for i in range(len(units))]
    t_inv = [eye + m for m in m_pow]
    for _ in range((c_len - 1).bit_length() - 1):
        m_pow = [_bdot(m, m) for m in m_pow]
        t_inv = [t + _bdot(t, m) for t, m in zip(t_inv, m_pow)]
    e_gc = [jnp.exp(x) for x in gc]
    u = [_bdot(t_inv[i], v[i] * beta[i]) for i in range(len(units))]
    w = [_bdot(t_inv[i], k[i] * (beta[i] * e_gc[i])) for i in range(len(units))]
    qk = [_bdot_nt(q[i], k[i]) * decay[i] for i in range(len(units))]
    q_dec = [q[i] * e_gc[i] for i in range(len(units))]
    k_dec = [k[i] * jnp.exp(g_last[i] - gc[i]) for i in range(len(units))]
    state = [state_ref[h] for h in heads]
    nw = nw_ref[...]
    out_rows = []
    for c in chunks:
        ids = [c * GDN_HEADS + h for h in heads]
        v_new = [u[i] - _bdot(w[i], state[h]) for h, i in zip(heads, ids)]
        o = [_bdot(q_dec[i], state[h]) + _bdot(qk[i], v_new[h]) for h, i in zip(heads, ids)]
        state = [state[h] * jnp.exp(g_last[i]) + _bdot_tn(k_dec[i], v_new[h]) for h, i in zip(heads, ids)]
        o = [x * lax.rsqrt(jnp.mean(x * x, -1, keepdims=True) + RMS_EPS) * nw for x in o]
        out_rows.append(jnp.concatenate(o, axis=1))
    for h in heads:
        state_ref[h] = state[h]
    o_ref[...] = (jnp.concatenate(out_rows, axis=0) * _silu(z_ref[...].astype(F32))).astype(o_ref.dtype)


def _gdn(proj3, small3, conv_w, a_log, dt_bias, norm_w, c_len=128, chunks_per_step=2):
    b, s, _ = proj3.shape
    width = GDN_HEADS * GDN_D
    rows = c_len * chunks_per_step
    col = lambda off: pl.BlockSpec((None, rows, width), lambda i, j, off=off: (i, j, off // GDN_HEADS))
    cw = lambda k: pl.BlockSpec((GDN_CONV, width), lambda i, j, k=k: (0, k))
    full = lambda shape: pl.BlockSpec(shape, lambda i, j: (0, 0))
    pad = lambda v: jnp.pad(v, (SM_DECAY, LANE - SM_DECAY - GDN_HEADS)).reshape(1, LANE)
    return pl.pallas_call(
        functools.partial(_gdn_kernel, c_len=c_len),
        grid=(b, s // rows),
        in_specs=[col(CB_Q), col(CB_K), col(CB_V), col(CB_Z),
                  pl.BlockSpec((None, rows, LANE), lambda i, j: (i, j, 0)),
                  cw(0), cw(1), cw(2),
                  full((1, LANE)), full((1, LANE)), full((1, GDN_D))],
        out_specs=pl.BlockSpec((None, rows, width), lambda i, j: (i, j, 0)),
        out_shape=jax.ShapeDtypeStruct((b, s, width), BF16),
        scratch_shapes=[pltpu.VMEM((GDN_HEADS, GDN_D, GDN_D), F32), pltpu.VMEM((3, 8, width), F32)],
        compiler_params=_cp(("parallel", "arbitrary")),
        name="gdn",
    )(proj3, proj3, proj3, proj3, small3, conv_w, conv_w, conv_w,
      pad(a_log), pad(dt_bias), norm_w.reshape(1, GDN_D))


def _rope_kernel(nq_ref, kv0_ref, kv1_ref, kv2_ref, pos_ref, freq_ref, q_out, ckv_out, kv_out, tmp_ref):
    ang = pos_ref[...] * freq_ref[...]
    cos2 = jnp.cos(ang)
    sin = jnp.sin(ang)
    sin2 = jnp.where(lax.broadcasted_iota(jnp.int32, ang.shape, 1) < NSA_DH // 2, -sin, sin)

    def rope(x):
        x = x.astype(F32)
        return x * cos2 + pltpu.roll(x, NSA_DH // 2, 1) * sin2

    scale = NSA_DH ** -0.5
    q_out[...] = jnp.concatenate(
        [rope(nq_ref[:, hh * LANE:(hh + 1) * LANE]) * scale for hh in range(NSA_HEADS)],
        axis=1).astype(q_out.dtype)
    kv_refs = (kv0_ref, kv1_ref, kv2_ref)
    blk = lambda i: kv_refs[i // 4][:, (i % 4) * LANE:(i % 4 + 1) * LANE]
    n_out = tmp_ref.shape[0] // CMP_STRIDE
    for slot, val in enumerate((rope(blk(0)), rope(blk(1)), blk(2).astype(F32), blk(3).astype(F32))):
        tmp_ref[...] = val
        for l in range(CMP_STRIDE):
            ckv_out[slot, :, l * LANE:(l + 1) * LANE] = tmp_ref[pl.ds(l, n_out, stride=CMP_STRIDE), :].astype(
                ckv_out.dtype)
    kv_out[...] = jnp.concatenate(
        [rope(blk(4)), rope(blk(5)), blk(6), blk(7), rope(blk(8)), rope(blk(9)), blk(10), blk(11)],
        axis=1).astype(kv_out.dtype)


def _rope(proj3, pos3, freq2, ts=512):
    b, s, _ = proj3.shape
    kvspec = lambda k: pl.BlockSpec((None, ts, 4 * LANE), lambda i, j, k=k: (i, j, CB_KV // 4 + k))
    return pl.pallas_call(
        _rope_kernel,
        grid=(b, s // ts),
        in_specs=[pl.BlockSpec((None, ts, NSA_HEADS * LANE), lambda i, j: (i, j, CB_NQ // NSA_HEADS)),
                  kvspec(0), kvspec(1), kvspec(2),
                  pl.BlockSpec((None, ts, 1), lambda i, j: (i, j, 0)),
                  pl.BlockSpec((1, NSA_DH), lambda i, j: (0, 0))],
        out_specs=[pl.BlockSpec((None, ts, NSA_HEADS * LANE), lambda i, j: (i, j, 0)),
                   pl.BlockSpec((None, 4, ts // CMP_STRIDE, CMP_STRIDE * LANE), lambda i, j: (i, 0, j, 0)),
                   pl.BlockSpec((None, ts, 8 * LANE), lambda i, j: (i, j, 0))],
        out_shape=[jax.ShapeDtypeStruct((b, s, NSA_HEADS * LANE), BF16),
                   jax.ShapeDtypeStruct((b, 4, s // CMP_STRIDE, CMP_STRIDE * LANE), BF16),
                   jax.ShapeDtypeStruct((b, s, 8 * LANE), BF16)],
        scratch_shapes=[pltpu.VMEM((ts, LANE), F32)],
        compiler_params=_cp(("parallel", "parallel")),
        name="rope",
    )(proj3, proj3, proj3, proj3, pos3, freq2)


def _compress_kernel(x_ref, pe_ref, w1_ref, b1_ref, w2_ref, o_ref):
    x = x_ref[...]
    half = CMP_STRIDE * NSA_DH
    first = jnp.dot(x, w1_ref[:half, :], preferred_element_type=F32)
    second = jnp.dot(x, w1_ref[half:, :], preferred_element_type=F32)
    n_rows = x.shape[0]
    second = pltpu.roll(second, n_rows - 1, 0)
    pe = jnp.broadcast_to(pe_ref[...], (8, pe_ref.shape[1]))
    bias = jnp.dot(pe, w1_ref[...], preferred_element_type=F32)[0:1] + b1_ref[...]
    hid = first + second + bias
    gelu = 0.5 * hid * (1.0 + jnp.tanh(0.7978845608028654 * (hid + 0.044715 * hid * hid * hid)))
    o_ref[...] = _bdot(gelu, w2_ref[...]).astype(o_ref.dtype)


def _compress(ckv_r, pe, w1, b1, w2):
    b, _, nr, wd = ckv_r.shape
    return pl.pallas_call(
        _compress_kernel,
        grid=(b, 4),
        in_specs=[pl.BlockSpec((None, None, nr, wd), lambda i, j: (i, j, 0, 0)),
                  pl.BlockSpec((None, 1, CMP_BLOCK * NSA_DH), lambda i, j: (j // 2, 0, 0)),
                  pl.BlockSpec((None, CMP_BLOCK * NSA_DH, CMP_HIDDEN), lambda i, j: (j // 2, 0, 0)),
                  pl.BlockSpec((None, 1, CMP_HIDDEN), lambda i, j: (j // 2, 0, 0)),
                  pl.BlockSpec((None, CMP_HIDDEN, NSA_DH), lambda i, j: (j // 2, 0, 0))],
        out_specs=pl.BlockSpec((None, None, nr, NSA_DH), lambda i, j: (i, j, 0, 0)),
        out_shape=jax.ShapeDtypeStruct((b, 4, nr, NSA_DH), BF16),
        compiler_params=_cp(("parallel", "arbitrary")),
        name="compress",
    )(ckv_r, pe, w1, b1, w2)


def _nsa_kernel(q_ref, kc_ref, vc_ref, ks_ref, vs_ref, kw_ref, vw_ref, eh_ref, sm_ref, ovt_ref, o_ref, *, tq):
    g = pl.program_id(1)
    qi = pl.program_id(2)
    t0 = qi * tq
    heads = range(NSA_HPG)
    nb = eh_ref.shape[0] // SEL_BLOCK
    qs = [q_ref[:, hh * LANE:(hh + 1) * LANE] for hh in heads]
    transpose_bf16 = lambda x: x.astype(F32).T.astype(BF16)
    col_max = lambda s: jnp.max(s, 0, keepdims=True)

    cmask = (lax.broadcasted_iota(jnp.int32, (LANE, tq), 0) * CMP_STRIDE + (CMP_BLOCK - 1)
             <= t0 + lax.broadcasted_iota(jnp.int32, (LANE, tq), 1))
    kc = kc_ref[...]
    s_c = [jnp.where(cmask, _bdot_nt(kc, x), NEG) for x in qs]
    e_c = [jnp.where(cmask, jnp.exp(s - col_max(s)), 0.0) for s in s_c]
    p_c = [e * (1.0 / jnp.maximum(jnp.sum(e, 0, keepdims=True), 1e-30)) for e in e_c]
    vc_t = transpose_bf16(vc_ref[...])
    o_c = [_bdot(vc_t, p) for p in p_c]

    p_sum = p_c[0]
    for p in p_c[1:]:
        p_sum = p_sum + p
    ovt = ovt_ref[...]
    imp_t = sum(jnp.dot(ovt, part, preferred_element_type=F32) for part in _split3(p_sum))[:nb]
    blk = lax.broadcasted_iota(jnp.int32, (nb, tq), 0)
    t = t0 + lax.broadcasted_iota(jnp.int32, (nb, tq), 1)
    cur = t >> SEL_SHIFT
    forced = (blk == 0) | (blk == cur) | (blk == cur - 1)
    val = jnp.where(forced, jnp.inf, jnp.where(blk * SEL_BLOCK <= t, imp_t, -jnp.inf))
    rank = jnp.zeros((nb, tq), F32)
    for i in range(nb):
        vi = val[i:i + 1, :]
        rank = rank + jnp.where((vi > val) | ((vi == val) & (blk > i)), 1.0, 0.0)
    drop_t = jnp.where(rank < SEL_TOPK, 0.0, MASK_BIG)
    drop = jnp.concatenate([drop_t, jnp.zeros((LANE - nb, tq), F32)], axis=0).T
    drop_b = (-drop).astype(BF16)
    q_aug = [jnp.concatenate([x, drop_b], axis=1) for x in qs]

    ones_col = jnp.where(lax.broadcasted_iota(jnp.int32, (tq, LANE), 1) == 0, 1.0, 0.0).astype(BF16)
    key_i = lax.broadcasted_iota(jnp.int32, (tq, tq), 0)
    qry_i = lax.broadcasted_iota(jnp.int32, (tq, tq), 1)
    key_le_query = key_i <= qry_i
    normalise = lambda acc: acc[:NSA_DH, :] * (1.0 / jnp.maximum(acc[NSA_DH:NSA_DH + 1, :], 1e-30))

    prev0 = pl.multiple_of(jnp.maximum(qi - 1, 0) * tq, tq)
    own0 = pl.multiple_of(t0, tq)
    in_prev = key_i > qry_i + jnp.where(qi > 0, 0, tq)
    kw_prev, kw_own = kw_ref[pl.ds(prev0, tq), :], kw_ref[pl.ds(own0, tq), :]
    v_w_t = transpose_bf16(jnp.concatenate(
        [jnp.concatenate([vw_ref[pl.ds(prev0, tq), :], ones_col], axis=1),
         jnp.concatenate([vw_ref[pl.ds(own0, tq), :], ones_col], axis=1)], axis=0))
    s_w = [jnp.concatenate([jnp.where(in_prev, _bdot_nt(kw_prev, x), NEG),
                            jnp.where(key_le_query, _bdot_nt(kw_own, x), NEG)], axis=0) for x in qs]
    e_w = [jnp.exp(s - col_max(s)) for s in s_w]
    o_w = [normalise(_bdot(v_w_t, e)) for e in e_w]

    def sel_chunk(c, carry, diagonal):
        m_i, acc = carry
        k0 = pl.multiple_of(c * tq, tq)
        k_aug = jnp.concatenate([ks_ref[pl.ds(k0, tq), :], eh_ref[pl.ds(k0, tq), :]], axis=1)
        v_t = transpose_bf16(jnp.concatenate([vs_ref[pl.ds(k0, tq), :], ones_col], axis=1))
        s = [lax.dot_general(k_aug, x, (((1,), (1,)), ((), ())), preferred_element_type=F32) for x in q_aug]
        if diagonal:
            s = [jnp.where(key_le_query, x, -MASK_BIG) for x in s]
        m_new = [jnp.maximum(m, col_max(x)) for m, x in zip(m_i, s)]
        p = [jnp.exp(x - m).astype(BF16) for x, m in zip(s, m_new)]
        acc = [jnp.exp(m - mn) * a + jnp.dot(v_t, x, preferred_element_type=F32)
               for m, mn, a, x in zip(m_i, m_new, acc, p)]
        return tuple(m_new), tuple(acc)

    init = (tuple(jnp.full((1, tq), -MASK_BIG, F32) for _ in heads),
            tuple(jnp.zeros((2 * NSA_DH, tq), F32) for _ in heads))
    carry = lax.fori_loop(0, qi, functools.partial(sel_chunk, diagonal=False), init)
    _, acc_s = sel_chunk(qi, carry, True)
    o_s = [normalise(a) for a in acc_s]

    gates_t = _sigmoid(sm_ref[...]).T
    outs = []
    for hh in heads:
        def gate(branch):
            rows = [gates_t[SM_NGATE + (grp * NSA_HPG + hh) * 3 + branch][None, :] for grp in range(NSA_KV_HEADS)]
            return jnp.where(g == 0, rows[0], rows[1])
        assert NSA_KV_HEADS == 2
        outs.append((gate(0) * o_c[hh] + gate(1) * o_s[hh] + gate(2) * o_w[hh]).T)
    o_ref[...] = jnp.concatenate(outs, axis=1).astype(o_ref.dtype)


def _nsa(q_r, kvc, kv_r, small3):
    b, s, _ = q_r.shape
    nr = kvc.shape[2]
    nb = s // SEL_BLOCK
    tq = WINDOW
    assert s % tq == 0 and nb <= LANE and nr <= LANE
    gw = NSA_HPG * LANE
    eh = jnp.asarray(np.arange(s)[:, None] // SEL_BLOCK == np.arange(LANE)[None, :], BF16)
    ovt = jnp.asarray(_overlap_matrix(s).T, BF16)
    kvspec = lambda off: pl.BlockSpec((None, s, LANE), lambda i, g, j, off=off: (i, 0, off + g))
    cspec = lambda off: pl.BlockSpec((None, None, nr, NSA_DH), lambda i, g, j, off=off: (i, off + g, 0, 0))
    return pl.pallas_call(
        functools.partial(_nsa_kernel, tq=tq),
        grid=(b, NSA_KV_HEADS, s // tq),
        in_specs=[pl.BlockSpec((None, tq, gw), lambda i, g, j: (i, j, g)),
                  cspec(0), cspec(2),
                  kvspec(0), kvspec(2), kvspec(4), kvspec(6),
                  pl.BlockSpec((s, LANE), lambda i, g, j: (0, 0)),
                  pl.BlockSpec((None, tq, LANE), lambda i, g, j: (i, j, 0)),
                  pl.BlockSpec((LANE, LANE), lambda i, g, j: (0, 0))],
        out_specs=pl.BlockSpec((None, tq, gw), lambda i, g, j: (i, j, g)),
        out_shape=jax.ShapeDtypeStruct((b, s, NSA_HEADS * NSA_DH), BF16),
        compiler_params=_cp(("parallel", "parallel", "arbitrary")),
        name="nsa",
    )(q_r, kvc, kvc, kv_r, kv_r, kv_r, kv_r, eh, small3, ovt)


def _merge_xattn_kernel(ya_ref, yb_ref, mg_ref, x_ref, wout_ref, g1_ref, b1_ref,
                        kv_ref, wq_ref, wo_ref, g_ref, b_ref, o_ref):
    mg = mg_ref[...].astype(F32)
    merged = (_sigmoid(mg[:, :D_MODEL]) * ya_ref[...].astype(F32)
              + _sigmoid(mg[:, D_MODEL:]) * yb_ref[...].astype(F32))
    h = _layer_norm(DN_ALPHA * x_ref[...] + _bdot(merged, wout_ref[...]), g1_ref[...], b1_ref[...])
    q = _bdot(h, wq_ref[...]).astype(BF16)
    kv = kv_ref[...]
    outs = []
    for hh in range(XA_HEADS):
        sl = slice(hh * XA_DH, (hh + 1) * XA_DH)
        s = _bdot_nt(q[:, sl], kv[:, sl]) * (XA_DH ** -0.5)
        m = jnp.max(s, -1, keepdims=True)
        e = jnp.exp(s - m)
        pv = _bdot(e, kv[:, D_MODEL + hh * XA_DH:D_MODEL + (hh + 1) * XA_DH])
        outs.append(pv * (1.0 / jnp.sum(e, -1, keepdims=True)))
    o = jnp.concatenate(outs, axis=1)
    o_ref[...] = _layer_norm(DN_ALPHA * h + _bdot(o, wo_ref[...]), g_ref[...], b_ref[...])


def _merge_xattn(y_a, y_b, proj3, x3, w_out, g1, b1, kv3, wq, wo, g2, b2, tm=512):
    bsz, s, _ = x3.shape
    m = kv3.shape[1]
    full = lambda shape: pl.BlockSpec(shape, lambda i, j: (0, 0))
    row = pl.BlockSpec((None, tm, D_MODEL), lambda i, j: (i, j, 0))
    return pl.pallas_call(
        _merge_xattn_kernel,
        grid=(bsz, s // tm),
        in_specs=[row, row,
                  pl.BlockSpec((None, tm, 2 * D_MODEL), lambda i, j: (i, j, CB_MG * LANE // (2 * D_MODEL))),
                  row, full((D_MODEL, D_MODEL)), full((1, D_MODEL)), full((1, D_MODEL)),
                  pl.BlockSpec((None, m, 2 * D_MODEL), lambda i, j: (i, 0, 0)),
                  full((D_MODEL, D_MODEL)), full((D_MODEL, D_MODEL)), full((1, D_MODEL)), full((1, D_MODEL))],
        out_specs=row,
        out_shape=jax.ShapeDtypeStruct((bsz, s, D_MODEL), F32),
        compiler_params=_cp(("parallel", "parallel")),
        name="merge_xattn",
    )(y_a, y_b, proj3, x3, w_out, g1, b1, kv3, wq, wo, g2, b2)


def _route(x, wrt_h, wrt_l, br):
    xh = x.astype(BF16)
    xl = (x - xh.astype(F32)).astype(BF16)
    logits = _bdot_nt(wrt_h, xh) + _bdot_nt(wrt_l, xh) + _bdot_nt(wrt_h, xl) + br
    row = lax.broadcasted_iota(jnp.int32, logits.shape, 0)
    first_row = lambda hit: jnp.min(jnp.where(hit, row, LANE), 0, keepdims=True)
    gmask = row < N_GROUPS
    gl = jnp.where(gmask, logits, -jnp.inf)
    gmax = jnp.max(gl, 0, keepdims=True)
    g_idx = first_row(gl == gmax)
    p_top = 1.0 / jnp.sum(jnp.where(gmask, jnp.exp(gl - gmax), 0.0), 0, keepdims=True)
    e_row = row - N_GROUPS
    emask = (e_row >= 0) & (e_row < N_EXPERTS) & ((e_row >> GROUP_SHIFT) == g_idx)
    el = jnp.where(emask, logits, -jnp.inf)
    ee = jnp.where(emask, jnp.exp(el - jnp.max(el, 0, keepdims=True)), 0.0)
    pe = ee / jnp.sum(ee, 0, keepdims=True)
    pv = jnp.where(emask, pe, -jnp.inf)
    m1 = jnp.max(pv, 0, keepdims=True)
    i1 = first_row(pv == m1)
    pv2 = jnp.where(row == i1, -jnp.inf, pv)
    m2 = jnp.max(pv2, 0, keepdims=True)
    i2 = first_row(pv2 == m2)
    denom = m1 + m2
    return i1, i2, p_top * m1 / denom, p_top * m2 / denom


META_E1, META_E2, META_R1, META_R2, META_W1, META_W2 = 0, 1, 2, 3, 4, 5
META_ROWS = 8


def _router_kernel(h_ref, wrh_ref, wrl_ref, br_ref, meta_ref, cnt_ref, carry_ref):
    @pl.when(pl.program_id(0) == 0)
    def _():
        carry_ref[...] = jnp.zeros_like(carry_ref)

    tm = h_ref.shape[0]
    i1, i2, w1, w2 = _route(h_ref[...], wrh_ref[...], wrl_ref[...], br_ref[...])
    row = lax.broadcasted_iota(jnp.int32, (LANE, tm), 0)
    onehot = jnp.where((row == i1) | (row == i2), 1.0, 0.0)
    earlier = jnp.where(lax.broadcasted_iota(jnp.int32, (tm, tm), 0) < lax.broadcasted_iota(jnp.int32, (tm, tm), 1),
                        1.0, 0.0).astype(BF16)
    before = jnp.dot(onehot.astype(BF16), earlier, preferred_element_type=F32) + carry_ref[...]
    r1 = jnp.sum(jnp.where(row == i1, before, 0.0), 0, keepdims=True)
    r2 = jnp.sum(jnp.where(row == i2, before, 0.0), 0, keepdims=True)
    carry_ref[...] += jnp.sum(onehot, axis=1, keepdims=True)
    cnt_ref[...] = carry_ref[...]
    rows = {META_E1: (i1 - N_GROUPS).astype(F32), META_E2: (i2 - N_GROUPS).astype(F32),
            META_R1: r1, META_R2: r2, META_W1: w1, META_W2: w2}
    zero = jnp.zeros((1, tm), F32)
    meta_ref[...] = jnp.concatenate([rows.get(r, zero) for r in range(META_ROWS)], axis=0)


def _router(h2, wrt_h, wrt_l, br, tm=1024):
    n = h2.shape[0]
    full = lambda shape: pl.BlockSpec(shape, lambda i: (0, 0))
    return pl.pallas_call(
        _router_kernel,
        grid=(n // tm,),
        in_specs=[pl.BlockSpec((tm, D_MODEL), lambda i: (i, 0)),
                  full((LANE, D_MODEL)), full((LANE, D_MODEL)), full((LANE, 1))],
        out_specs=[pl.BlockSpec((META_ROWS, tm), lambda i: (0, i)), full((LANE, 1))],
        out_shape=[jax.ShapeDtypeStruct((META_ROWS, n), F32), jax.ShapeDtypeStruct((LANE, 1), F32)],
        scratch_shapes=[pltpu.VMEM((LANE, 1), F32)],
        compiler_params=_cp(("arbitrary",)),
        name="moe_router",
    )(h2, wrt_h, wrt_l, br)


def _positions_kernel(meta_ref, cnt_ref, pos_ref):
    tm = meta_ref.shape[1]
    below = jnp.where(lax.broadcasted_iota(jnp.int32, (LANE, LANE), 1) < lax.broadcasted_iota(jnp.int32, (LANE, LANE), 0),
                      1.0, 0.0).astype(BF16)
    cnt = jnp.broadcast_to(cnt_ref[...], (LANE, LANE))
    start = sum(jnp.dot(below, p, preferred_element_type=F32) for p in _split3(cnt))
    start = jnp.concatenate([start] * (tm // LANE), axis=1)
    meta = meta_ref[...]
    row = lax.broadcasted_iota(jnp.int32, (LANE, tm), 0)
    pos = []
    for r_e, r_r in ((META_E1, META_R1), (META_E2, META_R2)):
        e_row = meta[r_e:r_e + 1, :].astype(jnp.int32) + N_GROUPS
        seg = jnp.sum(jnp.where(row == e_row, start, 0.0), 0, keepdims=True)
        pos.append((seg + meta[r_r:r_r + 1, :]).astype(jnp.int32))
    pos_ref[...] = jnp.concatenate(pos + [jnp.zeros((META_ROWS - 2, tm), jnp.int32)], axis=0)


def _positions(meta, counts, tm=2048):
    n = meta.shape[1]
    return pl.pallas_call(
        _positions_kernel,
        grid=(n // tm,),
        in_specs=[pl.BlockSpec((META_ROWS, tm), lambda i: (0, i)), pl.BlockSpec((LANE, 1), lambda i: (0, 0))],
        out_specs=pl.BlockSpec((META_ROWS, tm), lambda i: (0, i)),
        out_shape=jax.ShapeDtypeStruct((META_ROWS, n), jnp.int32),
        compiler_params=_cp(("parallel",)),
        name="moe_positions",
    )(meta, counts)


def _row_copies(n_rows, pos_refs, make_copy):
    def body(r, carry):
        for p in pos_refs:
            make_copy(r, p[r]).start()
        return carry

    lax.fori_loop(0, n_rows, body, 0, unroll=8)


def _dispatch_kernel(pos1_ref, pos2_ref, h_ref, xs_ref, sem):
    tm = h_ref.shape[0]
    _row_copies(tm, (pos1_ref, pos2_ref),
                lambda r, p: pltpu.make_async_copy(h_ref.at[pl.ds(r, 1)], xs_ref.at[pl.ds(p, 1)], sem))
    for _ in range(2):
        pltpu.make_async_copy(h_ref, xs_ref.at[pl.ds(0, tm)], sem).wait()


def _dispatch(h2, pos1, pos2, n_rows, tm=512):
    n = h2.shape[0]
    smem = pl.BlockSpec((tm,), lambda i: (i,), memory_space=pltpu.SMEM)
    return pl.pallas_call(
        _dispatch_kernel,
        grid=(n // tm,),
        in_specs=[smem, smem, pl.BlockSpec((tm, D_MODEL), lambda i: (i, 0))],
        out_specs=pl.BlockSpec(memory_space=pl.ANY),
        out_shape=jax.ShapeDtypeStruct((n_rows, D_MODEL), F32),
        scratch_shapes=[pltpu.SemaphoreType.DMA],
        compiler_params=_cp(("arbitrary",)),
        name="moe_dispatch",
    )(pos1, pos2, h2)


def _ffn_kernel(tile_ref, exp_ref, lo_ref, hi_ref, first_ref, xs_ref, wg_ref, wu_ref, wd_ref, ys_ref):
    i = pl.program_id(0)
    tm = xs_ref.shape[0]
    rows = lax.broadcasted_iota(jnp.int32, (tm, 1), 0)
    mine = (rows >= lo_ref[i]) & (rows < hi_ref[i])
    x = xs_ref[...].astype(BF16)
    gate = jnp.dot(x, wg_ref[...].astype(BF16), preferred_element_type=F32)
    up = jnp.dot(x, wu_ref[...].astype(BF16), preferred_element_type=F32)
    y = _bdot(_silu(gate) * up, wd_ref[...])

    @pl.when(first_ref[i] == 1)
    def _():
        ys_ref[...] = jnp.where(mine, y, 0.0)

    @pl.when(first_ref[i] == 0)
    def _():
        ys_ref[...] = jnp.where(mine, y, ys_ref[...])


def _ffn(xs, items, wg, wu, wd, tm):
    n_rows = xs.shape[0]
    wspec = lambda shape: pl.BlockSpec((None,) + shape, lambda i, tl, ex, lo, hi, fi: (ex[i], 0, 0))
    row_tile = pl.BlockSpec((tm, D_MODEL), lambda i, tl, ex, lo, hi, fi: (tl[i], 0))
    return pl.pallas_call(
        _ffn_kernel,
        grid_spec=pltpu.PrefetchScalarGridSpec(
            num_scalar_prefetch=5,
            grid=(items[0].shape[0],),
            in_specs=[row_tile, wspec((D_MODEL, D_FF)), wspec((D_MODEL, D_FF)), wspec((D_FF, D_MODEL))],
            out_specs=row_tile),
        out_shape=jax.ShapeDtypeStruct((n_rows, D_MODEL), F32),
        compiler_params=_cp(("arbitrary",)),
        name="moe_ffn",
    )(*items, xs, wg, wu, wd)


def _combine_kernel(pos1_ref, pos2_ref, nxt1_ref, nxt2_ref, h_ref, meta_ref, ys_ref, g_ref, b_ref, o_ref,
                    buf_ref, sem):
    i = pl.program_id(0)
    tm = h_ref.shape[0]
    slot = i % 2

    def start_gather(p1_ref, p2_ref, s):
        def copy(k, pos_ref):
            return lambda r, p: pltpu.make_async_copy(ys_ref.at[pl.ds(p, 1)], buf_ref.at[s, k, pl.ds(r, 1)],
                                                      sem.at[s])
        def body(r, carry):
            copy(0, p1_ref)(r, p1_ref[r]).start()
            copy(1, p2_ref)(r, p2_ref[r]).start()
            return carry
        lax.fori_loop(0, tm, body, 0, unroll=8)

    @pl.when(i == 0)
    def _():
        start_gather(pos1_ref, pos2_ref, 0)

    @pl.when(i + 1 < pl.num_programs(0))
    def _():
        start_gather(nxt1_ref, nxt2_ref, 1 - slot)

    for k in range(2):
        pltpu.make_async_copy(ys_ref.at[pl.ds(0, tm)], buf_ref.at[slot, k], sem.at[slot]).wait()
    meta = jnp.concatenate([meta_ref[...], jnp.zeros((LANE - META_ROWS, tm), F32)], axis=0).T
    ffn = (meta[:, META_W1:META_W1 + 1] * buf_ref[slot, 0] + meta[:, META_W2:META_W2 + 1] * buf_ref[slot, 1])
    o_ref[...] = _layer_norm(DN_ALPHA * h_ref[...] + ffn, g_ref[...], b_ref[...])


def _combine(h2, meta, ys, pos1, pos2, g, b, tm=256):
    n = h2.shape[0]
    last = n // tm - 1
    smem = pl.BlockSpec((tm,), lambda i: (i,), memory_space=pltpu.SMEM)
    smem_next = pl.BlockSpec((tm,), lambda i: (jnp.minimum(i + 1, last),), memory_space=pltpu.SMEM)
    full = lambda shape: pl.BlockSpec(shape, lambda i: (0, 0))
    return pl.pallas_call(
        _combine_kernel,
        grid=(n // tm,),
        in_specs=[smem, smem, smem_next, smem_next, pl.BlockSpec((tm, D_MODEL), lambda i: (i, 0)),
                  pl.BlockSpec((META_ROWS, tm), lambda i: (0, i)),
                  pl.BlockSpec(memory_space=pl.ANY), full((1, D_MODEL)), full((1, D_MODEL))],
        out_specs=pl.BlockSpec((tm, D_MODEL), lambda i: (i, 0)),
        out_shape=jax.ShapeDtypeStruct((n, D_MODEL), F32),
        scratch_shapes=[pltpu.VMEM((2, 2, tm, D_MODEL), F32), pltpu.SemaphoreType.DMA((2,))],
        compiler_params=_cp(("arbitrary",)),
        name="moe_combine_ln3",
    )(pos1, pos2, pos1, pos2, h2, meta, ys, g, b)


def _moe(h2, wr_h, wr_l, br, wg, wu, wd, g, b, tile=512):
    n = h2.shape[0]
    i32 = jnp.int32
    meta, counts = _router(h2, wr_h, wr_l, br)
    cnt = counts[N_GROUPS:N_GROUPS + N_EXPERTS, 0].astype(i32)
    seg_end = jnp.cumsum(cnt)
    seg_start = seg_end - cnt
    pos = _positions(meta, counts)
    pos1, pos2 = pos[0], pos[1]
    first_tile = seg_start // tile
    n_items_e = jnp.where(cnt > 0, (seg_end - 1) // tile - first_tile + 1, 0)
    item_end = jnp.cumsum(n_items_e)
    n_items = (2 * n) // tile + N_EXPERTS - 1
    idx = jnp.minimum(jnp.arange(n_items, dtype=i32), item_end[-1] - 1)
    exp = jnp.sum(idx[:, None] >= item_end[None, :], axis=1).astype(i32)
    til = first_tile[exp] + idx - (item_end - n_items_e)[exp]
    lo = jnp.maximum(seg_start[exp], til * tile) - til * tile
    hi = jnp.minimum(seg_end[exp], (til + 1) * tile) - til * tile
    repeat = jnp.arange(n_items, dtype=i32) >= item_end[-1]
    hi = jnp.where(repeat, lo, hi)
    first = jnp.concatenate([jnp.ones((1,), i32), (til[1:] != til[:-1]).astype(i32)])
    xs = _dispatch(h2, pos1, pos2, 2 * n)
    ys = _ffn(xs, (til.astype(i32), exp, lo.astype(i32), hi.astype(i32), first), wg, wu, wd, tile)
    return _combine(h2, meta, ys, pos1, pos2, g, b)


def _regroup_w_in(w):
    sizes = (1024, 1024, 1024, 1024, 8, 8, 1024, 256, 256, 256, 256, 256, 256, 24, 2048)
    offs = [0]
    for sz in sizes:
        offs.append(offs[-1] + sz)
    seg = lambda i: w[:, offs[i]:offs[i + 1]]
    small = jnp.concatenate([seg(4), seg(5), seg(13)], axis=1)
    small = jnp.pad(small, ((0, 0), (0, LANE - small.shape[1])))
    big = jnp.concatenate([seg(0), seg(1), seg(2), seg(3), seg(14), seg(6), seg(7), seg(8), seg(9), seg(10),
                           seg(11), seg(12)], axis=1)
    return big.astype(BF16), small.astype(BF16)


def _overlap_matrix(s):
    nb = s // SEL_BLOCK
    c0 = np.arange(LANE) * CMP_STRIDE
    s0 = np.arange(LANE) * SEL_BLOCK
    ov = np.minimum(c0[:, None] + CMP_BLOCK, s0[None, :] + SEL_BLOCK) - np.maximum(c0[:, None], s0[None, :])
    ov = np.maximum(ov, 0).astype(np.float32) / CMP_BLOCK
    nc = (s - CMP_BLOCK) // CMP_STRIDE + 1
    keep = (np.arange(LANE)[:, None] < nc) & (np.arange(LANE)[None, :] < nb)
    return np.where(keep, ov, 0.0).astype(np.float32)


def _layer(h, mem, pos3, freq2, w_in, conv_w, a_log, dt_bias, norm_w, cmp_pe, cmp_w1, cmp_b1, cmp_w2, w_out,
           ln1_g, ln1_b, xa_wq, xa_wkv, xa_wo, ln2_g, ln2_b, w_group, b_group, w_expert, b_expert,
           w_gate, w_up, w_down, ln3_g, ln3_b):
    b, s, d = h.shape
    n = b * s
    x2 = h.reshape(n, d)
    proj, small = _inproj(x2, *_regroup_w_in(w_in))
    proj3 = proj.reshape(b, s, -1)
    small3 = small.reshape(b, s, LANE)

    y_a = _gdn(proj3, small3, conv_w, a_log, dt_bias, norm_w)

    q_r, ckv_r, kv_r = _rope(proj3, pos3, freq2)
    kvc = _compress(ckv_r, cmp_pe.reshape(2, 1, CMP_BLOCK * NSA_DH).astype(BF16), cmp_w1.astype(BF16),
                    cmp_b1.reshape(2, 1, CMP_HIDDEN), cmp_w2.astype(BF16))
    y_b = _nsa(q_r, kvc, kv_r, small3)

    row = lambda v: v.reshape(1, -1)
    m = mem.shape[1]
    kv = _matmul(mem.reshape(b * m, d).astype(BF16), xa_wkv.astype(BF16), BF16, tm=512, tn=512)
    h2 = _merge_xattn(y_a, y_b, proj3, h, w_out.astype(BF16), row(ln1_g), row(ln1_b),
                      kv.reshape(b, m, 2 * d), xa_wq.astype(BF16), xa_wo.astype(BF16), row(ln2_g), row(ln2_b))

    wrt = jnp.pad(jnp.concatenate([w_group, w_expert], axis=1).T, ((0, LANE - N_GROUPS - N_EXPERTS), (0, 0)))
    wrt_h = wrt.astype(BF16)
    wrt_l = (wrt - wrt_h.astype(F32)).astype(BF16)
    br = jnp.pad(jnp.concatenate([b_group, b_expert]), (0, LANE - N_GROUPS - N_EXPERTS)).reshape(LANE, 1)
    h3 = _moe(h2.reshape(n, d), wrt_h, wrt_l, br, w_gate, w_up, w_down, row(ln3_g), row(ln3_b))
    return h3.reshape(b, s, d)


def kernel(x, mem, positions, w_in, gdn_conv_w, gdn_a_log, gdn_dt_bias, gdn_norm_w, cmp_pe, cmp_w1, cmp_b1, cmp_w2, w_out, ln1_g, ln1_b, xa_wq, xa_wkv, xa_wo, ln2_g, ln2_b, moe_w_group, moe_b_group, moe_w_expert, moe_b_expert, moe_w_gate, moe_w_up, moe_w_down, ln3_g, ln3_b):
    half = NSA_DH // 2
    inv_freq = ROPE_THETA ** (-jnp.arange(half, dtype=F32) / half)
    freq2 = jnp.concatenate([inv_freq, inv_freq]).reshape(1, NSA_DH)
    pos3 = positions.astype(F32)[..., None]
    h = x
    for l in range(DEPTH):
        h = _layer(h, mem, pos3, freq2, w_in[l], gdn_conv_w[l], gdn_a_log[l], gdn_dt_bias[l], gdn_norm_w[l],
                   cmp_pe[l], cmp_w1[l], cmp_b1[l], cmp_w2[l], w_out[l], ln1_g[l], ln1_b[l],
                   xa_wq[l], xa_wkv[l], xa_wo[l], ln2_g[l], ln2_b[l], moe_w_group[l], moe_b_group[l],
                   moe_w_expert[l], moe_b_expert[l], moe_w_gate[l], moe_w_up[l], moe_w_down[l],
                   ln3_g[l], ln3_b[l])
    return h
```
